```python
import math
import jax, jax.numpy as jnp
from jax import lax
import numpy as np

D_MODEL = 2048
BATCH = 4
SEQ = 2048
DEPTH = 4
DEC_BATCH = 128
DEC_SEQ = 1
PAST_LEN = 16384
PAGE_SIZE = 128

HGRN_HEADS = 8
HGRN_DK = 128
HGRN_DV = 128
HGRN_WIDTH = HGRN_HEADS * HGRN_DK
GLA_HEADS = 4
GLA_DK = 128
GLA_DV = 256
GLA_K_WIDTH = GLA_HEADS * GLA_DK
GLA_V_WIDTH = GLA_HEADS * GLA_DV
GLA_GATE_RANK = 16
GLA_GATE_NORM = 16.0
D_FF = 4 * D_MODEL
PLE_DIM = 256
CHUNK = 64
EPS = 1e-6

IN_WIDTHS = [HGRN_WIDTH, HGRN_WIDTH, HGRN_WIDTH, HGRN_WIDTH,
             GLA_K_WIDTH, GLA_K_WIDTH, GLA_V_WIDTH, GLA_V_WIDTH,
             GLA_GATE_RANK,
             D_MODEL, D_MODEL]
N_IN = sum(IN_WIDTHS)
IN_SPLITS = np.cumsum(IN_WIDTHS)[:-1].tolist()

kernel_name = "hgrn2_gla_gated_hybrid_step"


def _rmsnorm(x, g):
    xf = x.astype(jnp.float32)
    y = xf * lax.rsqrt(jnp.mean(xf * xf, axis=-1, keepdims=True) + EPS)
    return (y * g.astype(jnp.float32)).astype(x.dtype)


def _gated_recurrence(q, k, v, log_a, s0):
    B, L, H, K = q.shape
    V = v.shape[-1]
    c = math.gcd(CHUNK, L)
    n = L // c

    def to_blocks(t):
        return t.astype(jnp.float32).reshape(B, n, c, H, t.shape[-1]).transpose(1, 0, 3, 2, 4)

    qc, kc, vc, gc = to_blocks(q), to_blocks(k), to_blocks(v), to_blocks(log_a)
    causal = jnp.tril(jnp.ones((c, c), dtype=bool))[:, :, None]

    def step(s, inp):
        qi, ki, vi, gi = inp
        b = jnp.cumsum(gi, axis=-2)
        diff = b[..., :, None, :] - b[..., None, :, :]
        decay = jnp.exp(jnp.where(causal, diff, -jnp.inf))
        scores = jnp.einsum('bhtk,bhsk,bhtsk->bhts', qi, ki, decay)
        o = (jnp.einsum('bhts,bhsv->bhtv', scores, vi)
             + jnp.einsum('bhtk,bhkv->bhtv', qi * jnp.exp(b), s))
        b_last = b[..., -1:, :]
        s_new = (jnp.exp(b_last[..., 0, :])[..., None] * s
                 + jnp.einsum('bhsk,bhsv->bhkv', ki * jnp.exp(b_last - b), vi))
        return s_new, o

    s_fin, oc = lax.scan(step, s0.astype(jnp.float32), (qc, kc, vc, gc))
    o = oc.transpose(1, 0, 3, 2, 4).reshape(B, L, H, V)
    return o, s_fin


def _layer(h, p_i, s_h0, s_g0, lb_i, n_pre_mix, n_post_mix, n_pre_ffn, n_post_ffn,
           w_in, hgrn_norm, w_hgrn_up, w_gla_gate, b_gla_gate, gla_norm, w_gla_up,
           w_out, w_ff1, w_ff2, w_ple, w_ple_gate):
    B, L, _ = h.shape
    a = _rmsnorm(h, n_pre_mix)
    z = jnp.einsum('bld,dn->bln', a, w_in)
    (zq, zf, zi, zg, gq, gk, gv, gr, glr, m_h, m_g) = jnp.split(z, IN_SPLITS, axis=-1)

    lb = lb_i.reshape(HGRN_HEADS, HGRN_DK).astype(jnp.float32)
    fl = zf.reshape(B, L, HGRN_HEADS, HGRN_DK).astype(jnp.float32)
    log_f = jnp.logaddexp(jnp.log(lb), jnp.log1p(-lb) + jax.nn.log_sigmoid(fl))
    k_h = (1.0 - lb) * jax.nn.sigmoid(-fl)
    q_h = jax.nn.silu(zq).reshape(B, L, HGRN_HEADS, HGRN_DK)
    i_h = zi.reshape(B, L, HGRN_HEADS, HGRN_DV)
    o_h, s_h = _gated_recurrence(q_h, k_h, i_h, log_f, s_h0)
    o_h = _rmsnorm(o_h, hgrn_norm).astype(h.dtype) * jax.nn.silu(zg).reshape(B, L, HGRN_HEADS, HGRN_DV)
    y_h = jnp.einsum('blc,cd->bld', o_h.reshape(B, L, HGRN_WIDTH), w_hgrn_up)

    log_a = jax.nn.log_sigmoid((jnp.einsum('blr,rk->blk', glr, w_gla_gate) + b_gla_gate)
                               .astype(jnp.float32)) / GLA_GATE_NORM
    q_g = (gq * (GLA_DK ** -0.5)).reshape(B, L, GLA_HEADS, GLA_DK)
    k_g = gk.reshape(B, L, GLA_HEADS, GLA_DK)
    v_g = gv.reshape(B, L, GLA_HEADS, GLA_DV)
    o_g, s_g = _gated_recurrence(q_g, k_g, v_g, log_a.reshape(B, L, GLA_HEADS, GLA_DK), s_g0)
    o_g = _rmsnorm(o_g, gla_norm).astype(h.dtype) * jax.nn.silu(gr).reshape(B, L, GLA_HEADS, GLA_DV)
    y_g = jnp.einsum('blc,cd->bld', o_g.reshape(B, L, GLA_V_WIDTH), w_gla_up)

    merged = jax.nn.sigmoid(m_h) * y_h + jax.nn.sigmoid(m_g) * y_g
    h = h + _rmsnorm(jnp.einsum('bld,de->ble', merged, w_out), n_post_mix)

    c = _rmsnorm(h, n_pre_ffn)
    u = jnp.square(jax.nn.relu(jnp.einsum('bld,df->blf', c, w_ff1)))
    h = h + _rmsnorm(jnp.einsum('blf,fd->bld', u, w_ff2), n_post_ffn)

    pe = jnp.einsum('blp,pd->bld', p_i, w_ple)
    h = h + jax.nn.sigmoid(jnp.einsum('bld,de->ble', h, w_ple_gate)) * pe
    return h, s_h, s_g


def setup_inputs(seed: int = 0) -> dict:
    key = jax.random.key(seed)
    ks = jax.random.split(key, 24)
    f32 = jnp.float32
    nrm = lambda k, shape, scale: jax.random.normal(k, shape, f32) * scale
    gain = lambda k, shape: 1.0 + 0.05 * jax.random.normal(k, shape, f32)
    return {
        "x_prompt": nrm(ks[0], (BATCH, SEQ, D_MODEL), 1.0),
        "x_sample": nrm(ks[1], (DEC_BATCH, DEC_SEQ, D_MODEL), 1.0),
        "p_prompt": nrm(ks[2], (DEPTH, BATCH, SEQ, PLE_DIM), 1.0),
        "p_sample": nrm(ks[3], (DEPTH, DEC_BATCH, DEC_SEQ, PLE_DIM), 1.0),
        "state_hgrn": nrm(ks[4], (DEPTH, DEC_BATCH, HGRN_HEADS, HGRN_DK, HGRN_DV), 0.5),
        "state_gla": nrm(ks[5], (DEPTH, DEC_BATCH, GLA_HEADS, GLA_DK, GLA_DV), 0.5),
        "norm_pre_mix": gain(ks[6], (DEPTH, D_MODEL)),
        "norm_post_mix": gain(ks[7], (DEPTH, D_MODEL)),
        "norm_pre_ffn": gain(ks[8], (DEPTH, D_MODEL)),
        "norm_post_ffn": gain(ks[9], (DEPTH, D_MODEL)),
        "w_in": nrm(ks[10], (DEPTH, D_MODEL, N_IN), D_MODEL ** -0.5),
        "lb_param": nrm(ks[11], (DEPTH, HGRN_WIDTH), 0.1),
        "hgrn_norm": gain(ks[12], (DEPTH, HGRN_DV)),
        "w_hgrn_up": nrm(ks[13], (DEPTH, HGRN_WIDTH, D_MODEL), HGRN_WIDTH ** -0.5),
        "w_gla_gate": nrm(ks[14], (DEPTH, GLA_GATE_RANK, GLA_K_WIDTH), GLA_GATE_RANK ** -0.5),
        "b_gla_gate": nrm(ks[15], (DEPTH, GLA_K_WIDTH), 0.1),
        "gla_norm": gain(ks[16], (DEPTH, GLA_DV)),
        "w_gla_up": nrm(ks[17], (DEPTH, GLA_V_WIDTH, D_MODEL), GLA_V_WIDTH ** -0.5),
        "w_out": nrm(ks[18], (DEPTH, D_MODEL, D_MODEL), D_MODEL ** -0.5),
        "w_ff1": nrm(ks[19], (DEPTH, D_MODEL, D_FF), D_MODEL ** -0.5),
        "w_ff2": nrm(ks[20], (DEPTH, D_FF, D_MODEL), D_FF ** -0.5),
        "w_ple": nrm(ks[21], (DEPTH, PLE_DIM, D_MODEL), PLE_DIM ** -0.5),
        "w_ple_gate": nrm(ks[22], (DEPTH, D_MODEL, D_MODEL), D_MODEL ** -0.5),
    }


def reference(x_prompt, x_sample, p_prompt, p_sample, state_hgrn, state_gla,
              norm_pre_mix, norm_post_mix, norm_pre_ffn, norm_post_ffn, w_in, lb_param,
              hgrn_norm, w_hgrn_up, w_gla_gate, b_gla_gate, gla_norm, w_gla_up, w_out,
              w_ff1, w_ff2, w_ple, w_ple_gate):
    lbs = jnp.cumsum(jax.nn.softmax(lb_param.astype(jnp.float32), axis=0), axis=0)
    lbs = lbs - lbs[0:1]
    Bp = x_prompt.shape[0]
    hp, hs = x_prompt, x_sample
    hgrn_p, gla_p, hgrn_s, gla_s = [], [], [], []
    for i in range(DEPTH):
        w_i = (lbs[i], norm_pre_mix[i], norm_post_mix[i], norm_pre_ffn[i], norm_post_ffn[i],
               w_in[i], hgrn_norm[i], w_hgrn_up[i], w_gla_gate[i], b_gla_gate[i], gla_norm[i],
               w_gla_up[i], w_out[i], w_ff1[i], w_ff2[i], w_ple[i], w_ple_gate[i])
        zh = jnp.zeros((Bp, HGRN_HEADS, HGRN_DK, HGRN_DV), jnp.float32)
        zg = jnp.zeros((Bp, GLA_HEADS, GLA_DK, GLA_DV), jnp.float32)
        hp, sh, sg = _layer(hp, p_prompt[i], zh, zg, *w_i)
        hgrn_p.append(sh.astype(state_hgrn.dtype))
        gla_p.append(sg.astype(state_gla.dtype))
        hs, sh, sg = _layer(hs, p_sample[i], state_hgrn[i], state_gla[i], *w_i)
        hgrn_s.append(sh.astype(state_hgrn.dtype))
        gla_s.append(sg.astype(state_gla.dtype))
    return (hp, hs, jnp.stack(hgrn_p), jnp.stack(gla_p), jnp.stack(hgrn_s), jnp.stack(gla_s))
```

```python
import functools

import jax
import jax.numpy as jnp
from jax import lax
from jax.experimental import pallas as pl
from jax.experimental.pallas import tpu as pltpu

EPS = 1e-6
GLA_GATE_NORM = 16.0
LANES = 128
SUB = 16
VMEM_LIMIT = 56 * 1024 * 1024

f32 = jnp.float32
bf16 = jnp.bfloat16


def _sigmoid(x):
    return 1.0 / (1.0 + jnp.exp(-x))


def _silu(x):
    return x * _sigmoid(x)


def _log_sigmoid(x):
    return jnp.minimum(x, 0.0) - jnp.log1p(jnp.exp(-jnp.abs(x)))


def _rms(x, g):
    return x * lax.rsqrt(jnp.mean(x * x, axis=-1, keepdims=True) + EPS) * g


def _dot(a, b):
    return jnp.dot(a, b, preferred_element_type=f32)


def _dot_nt(a, b):
    return lax.dot_general(a, b, (((1,), (1,)), ((), ())), preferred_element_type=f32)


def _dot_tn(a, b):
    return lax.dot_general(a, b, (((0,), (0,)), ((), ())), preferred_element_type=f32)


def _split2(x):
    hi = x.astype(bf16)
    lo = (x - hi.astype(f32)).astype(bf16)
    return hi, lo


def _split3(x):
    hi = x.astype(bf16)
    r = x - hi.astype(f32)
    mid = r.astype(bf16)
    lo = (r - mid.astype(f32)).astype(bf16)
    return hi, mid, lo


def _chunk_cumsum(x, T):
    r = lax.broadcasted_iota(jnp.int32, (T, T), 0)
    c = lax.broadcasted_iota(jnp.int32, (T, T), 1)
    tri = jnp.where((r // SUB == c // SUB) & (c <= r), 1.0, 0.0).astype(bf16)
    hi, mid, lo = _split3(x)
    return _dot(tri, hi) + _dot(tri, mid) + _dot(tri, lo)


def _params(sem):
    return pltpu.CompilerParams(dimension_semantics=sem, vmem_limit_bytes=VMEM_LIMIT)


def _const_spec(shape, index_map):
    return pl.BlockSpec(shape, index_map, pipeline_mode=pl.Buffered(1))


def _pick(n, cands):
    for c in cands:
        if n % c == 0:
            return c
    return n


def _norm_kernel(x_ref, g_ref, o_ref):
    o_ref[...] = _rms(x_ref[...], g_ref[...]).astype(o_ref.dtype)


def _norm(x, gains, layer):
    M, D = x.shape
    tm = _pick(M, (512, 256, 128))
    return pl.pallas_call(
        _norm_kernel,
        grid=(M // tm,),
        in_specs=[pl.BlockSpec((tm, D), lambda m: (m, 0)),
                  pl.BlockSpec((None, 1, D), lambda m: (layer, 0, 0))],
        out_specs=pl.BlockSpec((tm, D), lambda m: (m, 0)),
        out_shape=jax.ShapeDtypeStruct((M, D), bf16),
        compiler_params=_params(("parallel",)),
        name="rmsnorm",
    )(x, gains)


def _mm_kernel(a_ref, w_ref, o_ref):
    o_ref[...] = _dot(a_ref[...], w_ref[...]).astype(o_ref.dtype)


def _mm(a, w, layer, out_dtype, name):
    M, K = a.shape
    N = w.shape[2]
    tm = _pick(M, (1024, 512, 256, 128))
    tn = N if N <= 1280 else _pick(N, (1024, 512, 256, 128))
    return pl.pallas_call(
        _mm_kernel,
        grid=(N // tn, M // tm),
        in_specs=[pl.BlockSpec((tm, K), lambda n, m: (m, 0)),
                  pl.BlockSpec((None, K, tn), lambda n, m: (layer, 0, n))],
        out_specs=pl.BlockSpec((tm, tn), lambda n, m: (m, n)),
        out_shape=jax.ShapeDtypeStruct((M, N), out_dtype),
        compiler_params=_params(("parallel", "parallel")),
        name=name,
    )(a, w)


def _hgrn_gates(fl, lb):
    e = jnp.exp(-jnp.abs(fl))
    x2 = jnp.log1p(-lb) + (jnp.minimum(fl, 0.0) - jnp.log1p(e))
    x1 = jnp.log(lb)
    log_f = jnp.maximum(x1, x2) + jnp.log1p(jnp.exp(-jnp.abs(x1 - x2)))
    r = 1.0 / (1.0 + e)
    k = (1.0 - lb) * jnp.where(fl >= 0.0, e * r, r)
    return log_f, k


def _gla_log_decay(glr, wg, bg):
    g_hi, g_lo = _split2(glr)
    w_hi, w_lo = _split2(wg)
    pre = _dot(g_hi, w_hi) + _dot(g_hi, w_lo) + _dot(g_lo, w_hi) + bg
    return _log_sigmoid(pre) / GLA_GATE_NORM


def _recurrence_chunks(T, H, K, V, load_q, load_k, b_ref, v_ref, g_ref, gain, st_ref, o_ref):
    ones = jnp.ones((K, LANES), bf16)
    row = lax.broadcasted_iota(jnp.int32, (SUB, K), 0)
    nv = V // LANES

    def chunk(c, carry):
        r0 = pl.multiple_of(c * SUB, SUB)
        rows = pl.ds(r0, SUB)
        for h in range(H):
            cs = slice(h * K, (h + 1) * K)
            vs = slice(h * V, (h + 1) * V)
            bc = b_ref[rows, cs]
            kc = load_k(rows, cs)
            qc = load_q(rows, cs)
            vb = v_ref[rows, vs]
            vf = vb.astype(f32)
            bl = bc[SUB - 1:SUB, :]
            st = st_ref[h]

            qe = (qc * jnp.exp(bc)).astype(bf16)
            o = _dot_nt(qe, st.astype(bf16))

            ps = []
            for s in range(SUB):
                d = jnp.minimum(bc - bc[s:s + 1, :], 0.0)
                dec = jnp.where(row >= s, jnp.exp(d), 0.0)
                ps.append(qc * (kc[s:s + 1, :] * dec))
            p = jnp.concatenate(ps, axis=0).astype(bf16)
            a = _dot(p, ones)
            parts = []
            for j in range(nv):
                acc = a[0:SUB, :] * vf[0:1, j * LANES:(j + 1) * LANES]
                for s in range(1, SUB):
                    acc = acc + a[s * SUB:(s + 1) * SUB, :] * vf[s:s + 1, j * LANES:(j + 1) * LANES]
                parts.append(acc)
            o = o + (parts[0] if nv == 1 else jnp.concatenate(parts, axis=1))

            kd = (kc * jnp.exp(bl - bc)).astype(bf16)
            st_ref[h] = st * jnp.exp(bl) + _dot_tn(vb, kd)

            on = _rms(o, gain)
            gate = _silu(g_ref[rows, vs].astype(f32))
            o_ref[rows, vs] = (on * gate).astype(o_ref.dtype)
        return carry

    lax.fori_loop(0, T // SUB, chunk, 0)


def _write_state(tb, st_ref, s_ref, H):
    @pl.when(tb == pl.num_programs(1) - 1)
    def _():
        for h in range(H):
            s_ref[0, h] = st_ref[h].T


def _hgrn_prompt_kernel(zq_ref, zf_ref, zi_ref, zg_ref, lb_ref, gain_ref, o_ref, s_ref,
                        st_ref, b_ref, k_ref, *, T, H, K, V):
    tb = pl.program_id(1)

    @pl.when(tb == 0)
    def _():
        st_ref[...] = jnp.zeros_like(st_ref)

    log_f, k = _hgrn_gates(zf_ref[...], lb_ref[...])
    b_ref[...] = _chunk_cumsum(log_f, T)
    k_ref[...] = k
    _recurrence_chunks(
        T, H, K, V,
        lambda rows, cs: _silu(zq_ref[rows, cs].astype(f32)),
        lambda rows, cs: k_ref[rows, cs],
        b_ref, zi_ref, zg_ref, gain_ref[...], st_ref, o_ref)
    _write_state(tb, st_ref, s_ref, H)


def _gla_prompt_kernel(gq_ref, gk_ref, gv_ref, gr_ref, glr_ref, wg_ref, bg_ref, gain_ref,
                       o_ref, s_ref, st_ref, b_ref, *, T, H, K, V):
    tb = pl.program_id(1)

    @pl.when(tb == 0)
    def _():
        st_ref[...] = jnp.zeros_like(st_ref)

    log_a = _gla_log_decay(glr_ref[...], wg_ref[...], bg_ref[...])
    b_ref[...] = _chunk_cumsum(log_a, T)
    scale = K ** -0.5
    _recurrence_chunks(
        T, H, K, V,
        lambda rows, cs: gq_ref[rows, cs].astype(f32) * scale,
        lambda rows, cs: gk_ref[rows, cs].astype(f32),
        b_ref, gv_ref, gr_ref, gain_ref[...], st_ref, o_ref)
    _write_state(tb, st_ref, s_ref, H)


def _hgrn_prompt(zb, zf, lbs, hgrn_norm, layer, B, L, H, K, V, off):
    HW = H * K
    T = min(256, L)
    nT = L // T
    tok = lambda col: (lambda b, t: (b * nT + t, col))
    kern = functools.partial(_hgrn_prompt_kernel, T=T, H=H, K=K, V=V)
    return pl.pallas_call(
        kern,
        grid=(B, nT),
        in_specs=[pl.BlockSpec((T, HW), tok(off["q"] // HW)),
                  pl.BlockSpec((T, HW), tok(0)),
                  pl.BlockSpec((T, HW), tok(off["i"] // HW)),
                  pl.BlockSpec((T, HW), tok(off["g"] // HW)),
                  pl.BlockSpec((None, 1, HW), lambda b, t: (layer, 0, 0)),
                  pl.BlockSpec((None, 1, V), lambda b, t: (layer, 0, 0))],
        out_specs=[pl.BlockSpec((T, HW), tok(0)),
                   pl.BlockSpec((1, H, K, V), lambda b, t: (b, 0, 0, 0))],
        out_shape=[jax.ShapeDtypeStruct((B * L, H * V), bf16),
                   jax.ShapeDtypeStruct((B, H, K, V), f32)],
        scratch_shapes=[pltpu.VMEM((H, V, K), f32),
                        pltpu.VMEM((T, HW), f32),
                        pltpu.VMEM((T, HW), f32)],
        compiler_params=_params(("parallel", "arbitrary")),
        name="hgrn_prompt",
    )(zb, zf, zb, zb, lbs, hgrn_norm)


def _gla_prompt(zb, zf, wgg, bgg, gla_norm, layer, B, L, H, K, V, off):
    KW, VW = H * K, H * V
    T = min(256, L)
    nT = L // T
    tok = lambda col: (lambda b, t: (b * nT + t, col))
    kern = functools.partial(_gla_prompt_kernel, T=T, H=H, K=K, V=V)
    return pl.pallas_call(
        kern,
        grid=(B, nT),
        in_specs=[pl.BlockSpec((T, KW), tok(off["gq"] // KW)),
                  pl.BlockSpec((T, KW), tok(off["gk"] // KW)),
                  pl.BlockSpec((T, VW), tok(off["gv"] // VW)),
                  pl.BlockSpec((T, VW), tok(off["gr"] // VW)),
                  pl.BlockSpec((T, LANES), tok(off["glr"] // LANES)),
                  pl.BlockSpec((None, LANES, KW), lambda b, t: (layer, 0, 0)),
                  pl.BlockSpec((None, 1, KW), lambda b, t: (layer, 0, 0)),
                  pl.BlockSpec((None, 1, V), lambda b, t: (layer, 0, 0))],
        out_specs=[pl.BlockSpec((T, VW), tok(0)),
                   pl.BlockSpec((1, H, K, V), lambda b, t: (b, 0, 0, 0))],
        out_shape=[jax.ShapeDtypeStruct((B * L, VW), bf16),
                   jax.ShapeDtypeStruct((B, H, K, V), f32)],
        scratch_shapes=[pltpu.VMEM((H, V, K), f32),
                        pltpu.VMEM((T, KW), f32)],
        compiler_params=_params(("parallel", "arbitrary")),
        name="gla_prompt",
    )(zb, zb, zb, zb, zf, wgg, bgg, gla_norm)


def _step_rows(Bt, H, K, V, a_ref, k_ref, qa_ref, v_ref, s_in_ref, s_out_ref, oi_ref):
    r = lax.broadcasted_iota(jnp.int32, (K, K), 0)
    c = lax.broadcasted_iota(jnp.int32, (K, K), 1)
    eye = r == c
    ones = jnp.ones((2 * K, LANES), bf16)
    nv = V // LANES

    def diag_pieces(ref, b, cs):
        x = ref[b, :, cs]
        hi = x.astype(bf16).astype(f32)
        lo = x - hi
        dh = jnp.where(eye, jnp.broadcast_to(hi, (K, K)), 0.0)
        dl = jnp.where(eye, jnp.broadcast_to(lo, (K, K)), 0.0)
        return jnp.concatenate([dh, dl], axis=1).astype(bf16)

    def body(b, carry):
        for h in range(H):
            cs = slice(h * K, (h + 1) * K)
            lhs = jnp.concatenate([diag_pieces(a_ref, b, cs),
                                   diag_pieces(k_ref, b, cs),
                                   diag_pieces(qa_ref, b, cs)], axis=0)
            cb = _dot(lhs, ones)
            a_col, k_col, q_col = cb[0:K], cb[K:2 * K], cb[2 * K:3 * K]
            for j in range(nv):
                ls = slice(h * V + j * LANES, h * V + (j + 1) * LANES)
                vj = slice(j * LANES, (j + 1) * LANES)
                s = s_in_ref[b, h, :, vj]
                s_out_ref[b, h, :, vj] = a_col * s + k_col * v_ref[b, :, ls]
                oi_ref[b, :, ls] = jnp.sum(q_col * s, axis=0, keepdims=True)
        return carry

    lax.fori_loop(0, Bt, body, 0)


def _store_rows(ref, x):
    for b in range(x.shape[0]):
        ref[b] = x[b:b + 1, :]


def _step_finish(H, K, V, q, k, v, g, gain, oi_ref, o_ref):
    ones = jnp.ones((K, LANES), bf16)
    nv = V // LANES
    for h in range(H):
        cs = slice(h * K, (h + 1) * K)
        vs = slice(h * V, (h + 1) * V)
        qk = _dot((q[:, cs] * k[:, cs]).astype(bf16), ones)
        if nv > 1:
            qk = jnp.concatenate([qk] * nv, axis=1)
        oi = jnp.concatenate([oi_ref[b, :, vs] for b in range(q.shape[0])], axis=0)
        o = qk * v[:, vs] + oi
        o_ref[:, vs] = (_rms(o, gain) * _silu(g[:, vs])).astype(o_ref.dtype)


def _hgrn_step_kernel(zq_ref, zf_ref, zi_ref, zg_ref, lb_ref, gain_ref, s_in_ref,
                      o_ref, s_out_ref, a_ref, k_ref, qa_ref, v_ref, oi_ref, *, Bt, H, K, V):
    log_f, k = _hgrn_gates(zf_ref[...], lb_ref[...])
    a = jnp.exp(log_f)
    q = _silu(zq_ref[...].astype(f32))
    v = zi_ref[...].astype(f32)
    _store_rows(a_ref, a)
    _store_rows(k_ref, k)
    _store_rows(qa_ref, q * a)
    _store_rows(v_ref, v)
    _step_rows(Bt, H, K, V, a_ref, k_ref, qa_ref, v_ref, s_in_ref, s_out_ref, oi_ref)
    _step_finish(H, K, V, q, k, v, zg_ref[...].astype(f32), gain_ref[...], oi_ref, o_ref)


def _gla_step_kernel(gq_ref, gk_ref, gv_ref, gr_ref, glr_ref, wg_ref, bg_ref, gain_ref,
                     s_in_ref, o_ref, s_out_ref, a_ref, k_ref, qa_ref, v_ref, oi_ref,
                     *, Bt, H, K, V):
    a = jnp.exp(_gla_log_decay(glr_ref[...], wg_ref[...], bg_ref[...]))
    q = gq_ref[...].astype(f32) * (K ** -0.5)
    k = gk_ref[...].astype(f32)
    v = gv_ref[...].astype(f32)
    _store_rows(a_ref, a)
    _store_rows(k_ref, k)
    _store_rows(qa_ref, q * a)
    _store_rows(v_ref, v)
    _step_rows(Bt, H, K, V, a_ref, k_ref, qa_ref, v_ref, s_in_ref, s_out_ref, oi_ref)
    _step_finish(H, K, V, q, k, v, gr_ref[...].astype(f32), gain_ref[...], oi_ref, o_ref)


def _hgrn_step(zb, zf, lbs, hgrn_norm, state, layer, H, K, V, off):
    Bs = zb.shape[0]
    HW = H * K
    Bt = min(16, Bs)
    row = lambda col: (lambda i: (i, col))
    kern = functools.partial(_hgrn_step_kernel, Bt=Bt, H=H, K=K, V=V)
    return pl.pallas_call(
        kern,
        grid=(Bs // Bt,),
        in_specs=[pl.BlockSpec((Bt, HW), row(off["q"] // HW)),
                  pl.BlockSpec((Bt, HW), row(0)),
                  pl.BlockSpec((Bt, HW), row(off["i"] // HW)),
                  pl.BlockSpec((Bt, HW), row(off["g"] // HW)),
                  pl.BlockSpec((None, 1, HW), lambda i: (layer, 0, 0)),
                  pl.BlockSpec((None, 1, V), lambda i: (layer, 0, 0)),
                  pl.BlockSpec((None, Bt, H, K, V), lambda i: (layer, i, 0, 0, 0))],
        out_specs=[pl.BlockSpec((Bt, HW), row(0)),
                   pl.BlockSpec((Bt, H, K, V), lambda i: (i, 0, 0, 0))],
        out_shape=[jax.ShapeDtypeStruct((Bs, H * V), bf16),
                   jax.ShapeDtypeStruct((Bs, H, K, V), f32)],
        scratch_shapes=[pltpu.VMEM((Bt, 1, HW), f32)] * 3 + [pltpu.VMEM((Bt, 1, H * V), f32)] * 2,
        compiler_params=_params(("parallel",)),
        name="hgrn_step",
    )(zb, zf, zb, zb, lbs, hgrn_norm, state)


def _gla_step(zb, zf, wgg, bgg, gla_norm, state, layer, H, K, V, off):
    Bs = zb.shape[0]
    KW, VW = H * K, H * V
    Bt = min(16, Bs)
    row = lambda col: (lambda i: (i, col))
    kern = functools.partial(_gla_step_kernel, Bt=Bt, H=H, K=K, V=V)
    return pl.pallas_call(
        kern,
        grid=(Bs // Bt,),
        in_specs=[pl.BlockSpec((Bt, KW), row(off["gq"] // KW)),
                  pl.BlockSpec((Bt, KW), row(off["gk"] // KW)),
                  pl.BlockSpec((Bt, VW), row(off["gv"] // VW)),
                  pl.BlockSpec((Bt, VW), row(off["gr"] // VW)),
                  pl.BlockSpec((Bt, LANES), row(off["glr"] // LANES)),
                  pl.BlockSpec((None, LANES, KW), lambda i: (layer, 0, 0)),
                  pl.BlockSpec((None, 1, KW), lambda i: (layer, 0, 0)),
                  pl.BlockSpec((None, 1, V), lambda i: (layer, 0, 0)),
                  pl.BlockSpec((None, Bt, H, K, V), lambda i: (layer, i, 0, 0, 0))],
        out_specs=[pl.BlockSpec((Bt, VW), row(0)),
                   pl.BlockSpec((Bt, H, K, V), lambda i: (i, 0, 0, 0))],
        out_shape=[jax.ShapeDtypeStruct((Bs, VW), bf16),
                   jax.ShapeDtypeStruct((Bs, H, K, V), f32)],
        scratch_shapes=[pltpu.VMEM((Bt, 1, KW), f32)] * 3 + [pltpu.VMEM((Bt, 1, VW), f32)] * 2,
        compiler_params=_params(("parallel",)),
        name="gla_step",
    )(zb, zb, zb, zb, zf, wgg, bgg, gla_norm, state)


def _postmix_kernel(oh_ref, og_ref, mh_ref, mg_ref, h_ref, whu_ref, wgu_ref, wout_ref,
                    npost_ref, npre_ref, h1_ref, c_ref):
    yh = _dot(oh_ref[...], whu_ref[...])
    yg = _dot(og_ref[...], wgu_ref[...])
    merged = (_sigmoid(mh_ref[...].astype(f32)) * yh + _sigmoid(mg_ref[...].astype(f32)) * yg)
    t = _dot(merged.astype(bf16), wout_ref[...])
    h1 = h_ref[...] + _rms(t, npost_ref[...])
    h1_ref[...] = h1
    c_ref[...] = _rms(h1, npre_ref[...]).astype(c_ref.dtype)


def _postmix(oh, og, zb, h, whu, wgu, wout, npost, npre, layer, off):
    M, D = h.shape
    HW, VW = oh.shape[1], og.shape[1]
    tm = _pick(M, (256, 128))
    lay = lambda m: (layer, 0, 0)
    return pl.pallas_call(
        _postmix_kernel,
        grid=(M // tm,),
        in_specs=[pl.BlockSpec((tm, HW), lambda m: (m, 0)),
                  pl.BlockSpec((tm, VW), lambda m: (m, 0)),
                  pl.BlockSpec((tm, D), lambda m: (m, off["mh"] // D)),
                  pl.BlockSpec((tm, D), lambda m: (m, off["mg"] // D)),
                  pl.BlockSpec((tm, D), lambda m: (m, 0)),
                  _const_spec((None, HW, D), lay),
                  _const_spec((None, VW, D), lay),
                  _const_spec((None, D, D), lay),
                  pl.BlockSpec((None, 1, D), lay),
                  pl.BlockSpec((None, 1, D), lay)],
        out_specs=[pl.BlockSpec((tm, D), lambda m: (m, 0)),
                   pl.BlockSpec((tm, D), lambda m: (m, 0))],
        out_shape=[jax.ShapeDtypeStruct((M, D), f32),
                   jax.ShapeDtypeStruct((M, D), bf16)],
        compiler_params=_params(("parallel",)),
        name="postmix",
    )(oh, og, zb, zb, h, whu, wgu, wout, npost, npre)


def _ffn_kernel(c_ref, w1_ref, w2_ref, h1_ref, npost_ref, o_ref, acc_ref):
    f = pl.program_id(1)
    u = _dot(c_ref[...], w1_ref[...])
    u = jnp.square(jnp.maximum(u, 0.0)).astype(bf16)
    part = _dot(u, w2_ref[...])

    @pl.when(f == 0)
    def _():
        acc_ref[...] = part

    @pl.when(f > 0)
    def _():
        acc_ref[...] += part

    @pl.when(f == pl.num_programs(1) - 1)
    def _():
        o_ref[...] = h1_ref[...] + _rms(acc_ref[...], npost_ref[...])


def _ffn(c, h1, w1, w2, npost, layer):
    M, D = h1.shape
    F = w1.shape[2]
    tm = _pick(M, (512, 256, 128))
    tf = _pick(F, (1024, 512, 256, 128))
    return pl.pallas_call(
        _ffn_kernel,
        grid=(M // tm, F // tf),
        in_specs=[pl.BlockSpec((tm, D), lambda m, f: (m, 0)),
                  pl.BlockSpec((None, D, tf), lambda m, f: (layer, 0, f)),
                  pl.BlockSpec((None, tf, D), lambda m, f: (layer, f, 0)),
                  pl.BlockSpec((tm, D), lambda m, f: (m, 0)),
                  pl.BlockSpec((None, 1, D), lambda m, f: (layer, 0, 0))],
        out_specs=pl.BlockSpec((tm, D), lambda m, f: (m, 0)),
        out_shape=jax.ShapeDtypeStruct((M, D), f32),
        scratch_shapes=[pltpu.VMEM((tm, D), f32)],
        compiler_params=_params(("parallel", "arbitrary")),
        name="ffn",
    )(c, w1, w2, h1, npost)


def _ple_kernel(h_ref, p_ref, wg_ref, wp_ref, nnext_ref, h3_ref, a_ref):
    h = h_ref[...]
    gate = _sigmoid(_dot(h.astype(bf16), wg_ref[...]))
    pe = _dot(p_ref[...].astype(bf16), wp_ref[...])
    h3 = h + gate * pe
    h3_ref[...] = h3
    a_ref[...] = _rms(h3, nnext_ref[...]).astype(a_ref.dtype)


def _ple(h, p, wg, wp, nnext, layer, next_layer):
    M, D = h.shape
    P = p.shape[1]
    tm = _pick(M, (512, 256, 128))
    lay = lambda m: (layer, 0, 0)
    return pl.pallas_call(
        _ple_kernel,
        grid=(M // tm,),
        in_specs=[pl.BlockSpec((tm, D), lambda m: (m, 0)),
                  pl.BlockSpec((tm, P), lambda m: (m, 0)),
                  _const_spec((None, D, D), lay),
                  _const_spec((None, P, D), lay),
                  pl.BlockSpec((None, 1, D), lambda m: (next_layer, 0, 0))],
        out_specs=[pl.BlockSpec((tm, D), lambda m: (m, 0)),
                   pl.BlockSpec((tm, D), lambda m: (m, 0))],
        out_shape=[jax.ShapeDtypeStruct((M, D), f32),
                   jax.ShapeDtypeStruct((M, D), bf16)],
        compiler_params=_params(("parallel",)),
        name="ple",
    )(h, p, wg, wp, nnext)


def kernel(x_prompt, x_sample, p_prompt, p_sample, state_hgrn, state_gla, norm_pre_mix,
           norm_post_mix, norm_pre_ffn, norm_post_ffn, w_in, lb_param, hgrn_norm, w_hgrn_up,
           w_gla_gate, b_gla_gate, gla_norm, w_gla_up, w_out, w_ff1, w_ff2, w_ple, w_ple_gate):
    B, L, D = x_prompt.shape
    Bs = x_sample.shape[0]
    depth = w_in.shape[0]
    _, _, HH, HK, HV = state_hgrn.shape
    _, _, GH, GK, GV = state_gla.shape
    HW, GKW, GVW = HH * HK, GH * GK, GH * GV
    R = w_gla_gate.shape[1]
    assert HK == LANES and GK == LANES and HV % LANES == 0 and GV % LANES == 0
    assert HH * HV == HW and R <= LANES and x_sample.shape[1] == 1 and L % SUB == 0

    widths = [HW, HW, HW, HW, GKW, GKW, GVW, GVW, R, D, D]
    names = ["q", "f", "i", "g", "gq", "gk", "gv", "gr", "glr", "mh", "mg"]
    starts = {}
    pos = 0
    for nme, w in zip(names, widths):
        starts[nme] = (pos, pos + w)
        pos += w
    col = lambda nme: w_in[:, :, starts[nme][0]:starts[nme][1]]

    b_names = ["q", "i", "g", "gq", "gk", "gv", "gr", "mh", "mg"]
    off = {}
    pos = 0
    for nme in b_names:
        off[nme] = pos
        pos += starts[nme][1] - starts[nme][0]
    off["glr"] = HW
    for nme, blk in (("q", HW), ("i", HW), ("g", HW), ("gq", GKW), ("gk", GKW), ("gv", GVW),
                     ("gr", GVW), ("mh", D), ("mg", D)):
        assert off[nme] % blk == 0
    wb = jnp.concatenate([col(n) for n in b_names], axis=2).astype(bf16)
    wf = jnp.concatenate([col("f"), col("glr"), jnp.zeros((depth, D, LANES - R), w_in.dtype)],
                         axis=2).astype(bf16)
    whu = w_hgrn_up.astype(bf16)
    wgu = w_gla_up.astype(bf16)
    wout = w_out.astype(bf16)
    w1 = w_ff1.astype(bf16)
    w2 = w_ff2.astype(bf16)
    wpg = w_ple_gate.astype(bf16)
    wp = w_ple.astype(bf16)
    wgg = jnp.concatenate([w_gla_gate.astype(f32), jnp.zeros((depth, LANES - R, GKW), f32)], axis=1)

    lbs = jnp.cumsum(jax.nn.softmax(lb_param.astype(f32), axis=0), axis=0)
    lbs = (lbs - lbs[0:1]).reshape(depth, 1, HW)
    r3 = lambda t: t.astype(f32).reshape(depth, 1, t.shape[-1])
    n_pre_mix, n_post_mix, n_pre_ffn, n_post_ffn = map(
        r3, (norm_pre_mix, norm_post_mix, norm_pre_ffn, norm_post_ffn))
    hn, gn, bgg = r3(hgrn_norm), r3(gla_norm), r3(b_gla_gate)

    hp = x_prompt.reshape(B * L, D)
    hs = x_sample.reshape(Bs, D)
    pp = p_prompt.reshape(depth, B * L, -1)
    ps = p_sample.reshape(depth, Bs, -1)
    ap = _norm(hp, n_pre_mix, 0)
    as_ = _norm(hs, n_pre_mix, 0)

    hgrn_p, gla_p, hgrn_s, gla_s = [], [], [], []
    for i in range(depth):
        nxt = (i + 1) % depth

        def dense_tail(h, a_unused, oh, og, zb, p):
            h1, c = _postmix(oh, og, zb, h, whu, wgu, wout, n_post_mix, n_pre_ffn, i, off)
            h2 = _ffn(c, h1, w1, w2, n_post_ffn, i)
            return _ple(h2, p, wpg, wp, n_pre_mix, i, nxt)

        zb = _mm(ap, wb, i, bf16, "in_proj")
        zf = _mm(ap, wf, i, f32, "in_proj_gates")
        oh, sh = _hgrn_prompt(zb, zf, lbs, hn, i, B, L, HH, HK, HV, off)
        og, sg = _gla_prompt(zb, zf, wgg, bgg, gn, i, B, L, GH, GK, GV, off)
        hp, ap = dense_tail(hp, ap, oh, og, zb, pp[i])
        hgrn_p.append(sh)
        gla_p.append(sg)

        zb = _mm(as_, wb, i, bf16, "in_proj_s")
        zf = _mm(as_, wf, i, f32, "in_proj_gates_s")
        oh, sh = _hgrn_step(zb, zf, lbs, hn, state_hgrn, i, HH, HK, HV, off)
        og, sg = _gla_step(zb, zf, wgg, bgg, gn, state_gla, i, GH, GK, GV, off)
        hs, as_ = dense_tail(hs, as_, oh, og, zb, ps[i])
        hgrn_s.append(sh)
        gla_s.append(sg)

    return (hp.reshape(B, L, D), hs.reshape(Bs, 1, D),
            jnp.stack(hgrn_p).astype(state_hgrn.dtype), jnp.stack(gla_p).astype(state_gla.dtype),
            jnp.stack(hgrn_s).astype(state_hgrn.dtype), jnp.stack(gla_s).astype(state_gla.dtype))
```

```python
import functools

import numpy as np
import jax
import jax.numpy as jnp
from jax import lax
from jax.experimental import pallas as pl
from jax.experimental.pallas import tpu as pltpu

EPS = 1e-6
GLA_GATE_NORM = 16.0
LOG2E = 1.4426950408889634
LANES = 128
CHUNK = 64
N_LEVELS = 6
VMEM_LIMIT = 56 * 1024 * 1024

f32 = jnp.float32
bf16 = jnp.bfloat16


def _sigmoid(x):
    return 1.0 / (1.0 + jnp.exp(-x))


def _silu(x):
    return x * _sigmoid(x)


def _log_sigmoid(x):
    return jnp.minimum(x, 0.0) - jnp.log(1.0 + jnp.exp(-jnp.abs(x)))


def _rms(x, g):
    return x * lax.rsqrt(jnp.mean(x * x, axis=-1, keepdims=True) + EPS) * g


def _dot(a, b):
    return jnp.dot(a, b, preferred_element_type=f32)


def _dot_nt(a, b):
    return lax.dot_general(a, b, (((1,), (1,)), ((), ())), preferred_element_type=f32)


def _dot_tn(a, b):
    return lax.dot_general(a, b, (((0,), (0,)), ((), ())), preferred_element_type=f32)


def _split2(x):
    hi = x.astype(bf16)
    lo = (x - hi.astype(f32)).astype(bf16)
    return hi, lo


def _params(sem):
    return pltpu.CompilerParams(dimension_semantics=sem, vmem_limit_bytes=VMEM_LIMIT)


def _const_spec(shape, index_map):
    return pl.BlockSpec(shape, index_map, pipeline_mode=pl.Buffered(1))


def _pick(n, cands):
    for c in cands:
        if n % c == 0:
            return c
    return n


def _level_tables():
    C = CHUNK
    t = np.arange(C)[:, None]
    j = np.arange(C)[None, :]
    mats = [(j <= t), (j > t)]
    lvl = np.full((C, C), -1, np.int32)
    lvl[t == j] = 0
    w = C // 2
    level = 1
    while w >= 1:
        start = (t // (2 * w)) * (2 * w)
        m = start + w - 1
        second = (t - start) >= w
        mats.append(np.where(second, (j > m) & (j <= t), (j > t) & (j <= m)))
        same = (t // (2 * w)) == (j // (2 * w))
        lvl[same & second & ((j - (j // (2 * w)) * (2 * w)) < w)] = level
        w //= 2
        level += 1
    mall = np.concatenate(mats, axis=0).astype(np.float32)
    return jnp.asarray(np.concatenate([mall, mall], axis=1), bf16), jnp.asarray(lvl)


def _norm_kernel(x_ref, g_ref, o_ref):
    o_ref[...] = _rms(x_ref[...], g_ref[...]).astype(o_ref.dtype)


def _norm(x, gains, layer):
    M, D = x.shape
    tm = _pick(M, (512, 256, 128))
    return pl.pallas_call(
        _norm_kernel,
        grid=(M // tm,),
        in_specs=[pl.BlockSpec((tm, D), lambda m: (m, 0)),
                  pl.BlockSpec((None, 1, D), lambda m: (layer, 0, 0))],
        out_specs=pl.BlockSpec((tm, D), lambda m: (m, 0)),
        out_shape=jax.ShapeDtypeStruct((M, D), bf16),
        compiler_params=_params(("parallel",)),
        name="rmsnorm",
    )(x, gains)


def _proj_kernel(*refs, n_extra, epilogue):
    a_ref, w_ref = refs[0], refs[1]
    extra = refs[2:2 + n_extra]
    outs = refs[2 + n_extra:-1]
    wbf_ref = refs[-1]

    @pl.when(pl.program_id(1) == 0)
    def _():
        wbf_ref[...] = w_ref[...].astype(bf16)

    acc = _dot(a_ref[...], wbf_ref[...])
    epilogue(pl.program_id(0), acc, extra, outs)


def _proj(a, w, layer, col_tile, n_tiles, tn, epilogue, extra, extra_specs, out_widths,
          out_dtypes, name, tm_cands=(1024, 512, 256, 128), w_single=False):
    M, K = a.shape
    tm = _pick(M, tm_cands)
    w_map = lambda n, m: (layer, 0, col_tile(n))
    w_spec = _const_spec((None, K, tn), w_map) if w_single else pl.BlockSpec((None, K, tn), w_map)
    kern = functools.partial(_proj_kernel, n_extra=len(extra), epilogue=epilogue)
    return pl.pallas_call(
        kern,
        grid=(n_tiles, M // tm),
        in_specs=[pl.BlockSpec((tm, K), lambda n, m: (m, 0)), w_spec] + list(extra_specs),
        out_specs=[pl.BlockSpec((tm, ow), lambda n, m: (m, n)) for ow in out_widths],
        out_shape=[jax.ShapeDtypeStruct((M, n_tiles * ow), dt)
                   for ow, dt in zip(out_widths, out_dtypes)],
        scratch_shapes=[pltpu.VMEM((K, tn), bf16)],
        compiler_params=_params(("parallel", "arbitrary")),
        name=name,
    )(a, w, *extra)


def _hgrn_gates(fl, lb):
    e = jnp.exp(-jnp.abs(fl))
    x2 = jnp.log(1.0 - lb) + (jnp.minimum(fl, 0.0) - jnp.log(1.0 + e))
    x1 = jnp.log(lb)
    log_f = jnp.maximum(x1, x2) + jnp.log(1.0 + jnp.exp(-jnp.abs(x1 - x2)))
    r = 1.0 / (1.0 + e)
    k = (1.0 - lb) * jnp.where(fl >= 0.0, e * r, r)
    return log_f, k


def _ep_main(n, acc, extra, outs, *, scale):
    o_ref, = outs
    half = acc.shape[1] // 2

    @pl.when((n == 0) | (n == 2) | (n == 5))
    def _():
        o_ref[...] = _silu(acc).astype(o_ref.dtype)

    @pl.when((n == 1) | (n == 4))
    def _():
        o_ref[...] = acc.astype(o_ref.dtype)

    @pl.when(n == 3)
    def _():
        o_ref[:, :half] = (acc[:, :half] * scale).astype(o_ref.dtype)
        o_ref[:, half:] = acc[:, half:].astype(o_ref.dtype)


def _ep_forget(n, acc, extra, outs):
    hi_ref, lo_ref, k_ref = outs
    log_f, k = _hgrn_gates(acc, extra[0][...])
    hi, lo = _split2(log_f * LOG2E)
    hi_ref[...] = hi
    lo_ref[...] = lo
    k_ref[...] = k


def _ep_sigmoid(n, acc, extra, outs):
    outs[0][...] = _sigmoid(acc).astype(outs[0].dtype)


def _ep_copy(n, acc, extra, outs):
    outs[0][...] = acc.astype(outs[0].dtype)


def _ep_relu2(n, acc, extra, outs):
    outs[0][...] = jnp.square(jnp.maximum(acc, 0.0)).astype(outs[0].dtype)


def _recurrence_chunks(T, H, K, V, q_ref, load_k, ldhi_ref, ldlo_ref, v_ref, g_ref, gain,
                       mall_ref, lvl_ref, st_ref, o_ref):
    C = CHUNK

    def chunk(c, carry):
        r0 = pl.multiple_of(c * C, C)
        rows = pl.ds(r0, C)
        lvl = lvl_ref[...]
        for h in range(H):
            cs = slice(h * K, (h + 1) * K)
            vs = slice(h * V, (h + 1) * V)
            ld2 = jnp.concatenate([ldhi_ref[rows, cs], ldlo_ref[rows, cs]], axis=0)
            e_all = jnp.exp2(_dot(mall_ref[...], ld2))
            qc = q_ref[rows, cs].astype(f32)
            kc = load_k(rows, cs)
            vb = v_ref[rows, vs]
            st = st_ref[h]

            o = _dot_nt((qc * e_all[0:C]).astype(bf16), st.astype(bf16))

            a = jnp.where(lvl == 0, _dot_nt(qc.astype(bf16), kc.astype(bf16)), 0.0)
            for l in range(1, N_LEVELS + 1):
                el = e_all[(l + 1) * C:(l + 2) * C]
                al = _dot_nt((qc * el).astype(bf16), (kc * el).astype(bf16))
                a = jnp.where(lvl == l, al, a)
            o = o + _dot(a.astype(bf16), vb)

            kd = (kc * e_all[C:2 * C]).astype(bf16)
            st_ref[h] = st * e_all[C - 1:C] + _dot_tn(vb, kd)

            gate = g_ref[rows, vs].astype(f32)
            o_ref[rows, vs] = (_rms(o, gain) * gate).astype(o_ref.dtype)
        return carry

    lax.fori_loop(0, T // C, chunk, 0)


def _write_state(tb, st_ref, s_ref, H):
    @pl.when(tb == pl.num_programs(1) - 1)
    def _():
        for h in range(H):
            s_ref[0, h] = st_ref[h].T


def _hgrn_prompt_kernel(zq_ref, zi_ref, zg_ref, ldhi_ref, ldlo_ref, k_ref, gain_ref, mall_ref,
                        lvl_ref, o_ref, s_ref, st_ref, *, T, H, K, V):
    tb = pl.program_id(1)

    @pl.when(tb == 0)
    def _():
        st_ref[...] = jnp.zeros_like(st_ref)

    _recurrence_chunks(T, H, K, V, zq_ref, lambda rows, cs: k_ref[rows, cs], ldhi_ref, ldlo_ref,
                       zi_ref, zg_ref, gain_ref[...], mall_ref, lvl_ref, st_ref, o_ref)
    _write_state(tb, st_ref, s_ref, H)


def _gla_log2_decay(glr, wg, bg):
    g_hi, g_lo = _split2(glr)
    w_hi, w_lo = _split2(wg)
    pre = _dot(g_hi, w_hi) + _dot(g_hi, w_lo) + _dot(g_lo, w_hi) + bg
    return _log_sigmoid(pre) * (LOG2E / GLA_GATE_NORM)


def _gla_prompt_kernel(gqk_ref, gv_ref, gr_ref, glr_ref, wg_ref, bg_ref, gain_ref, mall_ref,
                       lvl_ref, o_ref, s_ref, st_ref, ldhi_ref, ldlo_ref, *, T, H, K, V):
    tb = pl.program_id(1)

    @pl.when(tb == 0)
    def _():
        st_ref[...] = jnp.zeros_like(st_ref)

    hi, lo = _split2(_gla_log2_decay(glr_ref[...], wg_ref[...], bg_ref[...]))
    ldhi_ref[...] = hi
    ldlo_ref[...] = lo
    KW = H * K
    k_ref = gqk_ref.at[:, KW:2 * KW]
    _recurrence_chunks(T, H, K, V, gqk_ref, lambda rows, cs: k_ref[rows, cs].astype(f32),
                       ldhi_ref, ldlo_ref, gv_ref, gr_ref, gain_ref[...], mall_ref, lvl_ref,
                       st_ref, o_ref)
    _write_state(tb, st_ref, s_ref, H)


def _hgrn_prompt(zb, ldhi, ldlo, kh, hgrn_norm, mall, lvl, layer, B, L, H, K, V):
    HW = H * K
    T = min(256, L)
    nT = L // T
    tok = lambda col: (lambda b, t: (b * nT + t, col))
    cst = lambda b, t: (0, 0)
    kern = functools.partial(_hgrn_prompt_kernel, T=T, H=H, K=K, V=V)
    return pl.pallas_call(
        kern,
        grid=(B, nT),
        in_specs=[pl.BlockSpec((T, HW), tok(0)),
                  pl.BlockSpec((T, HW), tok(1)),
                  pl.BlockSpec((T, HW), tok(2)),
                  pl.BlockSpec((T, HW), tok(0)),
                  pl.BlockSpec((T, HW), tok(0)),
                  pl.BlockSpec((T, HW), tok(0)),
                  pl.BlockSpec((None, 1, V), lambda b, t: (layer, 0, 0)),
                  pl.BlockSpec(mall.shape, cst),
                  pl.BlockSpec(lvl.shape, cst)],
        out_specs=[pl.BlockSpec((T, HW), tok(0)),
                   pl.BlockSpec((1, H, K, V), lambda b, t: (b, 0, 0, 0))],
        out_shape=[jax.ShapeDtypeStruct((B * L, H * V), bf16),
                   jax.ShapeDtypeStruct((B, H, K, V), f32)],
        scratch_shapes=[pltpu.VMEM((H, V, K), f32)],
        compiler_params=_params(("parallel", "arbitrary")),
        name="hgrn_prompt",
    )(zb, zb, zb, ldhi, ldlo, kh, hgrn_norm, mall, lvl)


def _gla_prompt(zb, glr, wgg, bgg, gla_norm, mall, lvl, layer, B, L, H, K, V):
    KW, VW = H * K, H * V
    T = min(256, L)
    nT = L // T
    tok = lambda col: (lambda b, t: (b * nT + t, col))
    cst = lambda b, t: (0, 0)
    kern = functools.partial(_gla_prompt_kernel, T=T, H=H, K=K, V=V)
    return pl.pallas_call(
        kern,
        grid=(B, nT),
        in_specs=[pl.BlockSpec((T, 2 * KW), tok(3)),
                  pl.BlockSpec((T, VW), tok(4)),
                  pl.BlockSpec((T, VW), tok(5)),
                  pl.BlockSpec((T, LANES), tok(0)),
                  pl.BlockSpec((None, LANES, KW), lambda b, t: (layer, 0, 0)),
                  pl.BlockSpec((None, 1, KW), lambda b, t: (layer, 0, 0)),
                  pl.BlockSpec((None, 1, V), lambda b, t: (layer, 0, 0)),
                  pl.BlockSpec(mall.shape, cst),
                  pl.BlockSpec(lvl.shape, cst)],
        out_specs=[pl.BlockSpec((T, VW), tok(0)),
                   pl.BlockSpec((1, H, K, V), lambda b, t: (b, 0, 0, 0))],
        out_shape=[jax.ShapeDtypeStruct((B * L, VW), bf16),
                   jax.ShapeDtypeStruct((B, H, K, V), f32)],
        scratch_shapes=[pltpu.VMEM((H, V, K), f32),
                        pltpu.VMEM((T, KW), bf16),
                        pltpu.VMEM((T, KW), bf16)],
        compiler_params=_params(("parallel", "arbitrary")),
        name="gla_prompt",
    )(zb, zb, zb, glr, wgg, bgg, gla_norm, mall, lvl)


def _step_rows(Bt, H, K, V, a_ref, k_ref, qa_ref, v_ref, s_in_ref, s_out_ref, oi_ref):
    r = lax.broadcasted_iota(jnp.int32, (K, K), 0)
    c = lax.broadcasted_iota(jnp.int32, (K, K), 1)
    eye = r == c
    ones = jnp.ones((2 * K, LANES), bf16)
    nv = V // LANES

    def diag_pieces(ref, b, cs):
        x = ref[b, :, cs]
        hi = x.astype(bf16).astype(f32)
        lo = x - hi
        dh = jnp.where(eye, jnp.broadcast_to(hi, (K, K)), 0.0)
        dl = jnp.where(eye, jnp.broadcast_to(lo, (K, K)), 0.0)
        return jnp.concatenate([dh, dl], axis=1).astype(bf16)

    def body(b, carry):
        for h in range(H):
            cs = slice(h * K, (h + 1) * K)
            lhs = jnp.concatenate([diag_pieces(a_ref, b, cs),
                                   diag_pieces(k_ref, b, cs),
                                   diag_pieces(qa_ref, b, cs)], axis=0)
            cb = _dot(lhs, ones)
            a_col, k_col, q_col = cb[0:K], cb[K:2 * K], cb[2 * K:3 * K]
            for j in range(nv):
                ls = slice(h * V + j * LANES, h * V + (j + 1) * LANES)
                vj = slice(j * LANES, (j + 1) * LANES)
                s = s_in_ref[b, h, :, vj]
                s_out_ref[b, h, :, vj] = a_col * s + k_col * v_ref[b, :, ls]
                oi_ref[b, :, ls] = jnp.sum(q_col * s, axis=0, keepdims=True)
        return carry

    lax.fori_loop(0, Bt, body, 0)


def _store_rows(ref, x):
    for b in range(x.shape[0]):
        ref[b] = x[b:b + 1, :]


def _step_finish(H, K, V, q, k, v, g, gain, oi_ref, o_ref):
    ones = jnp.ones((K, LANES), bf16)
    nv = V // LANES
    for h in range(H):
        cs = slice(h * K, (h + 1) * K)
        vs = slice(h * V, (h + 1) * V)
        qk = _dot((q[:, cs] * k[:, cs]).astype(bf16), ones)
        if nv > 1:
            qk = jnp.concatenate([qk] * nv, axis=1)
        oi = jnp.concatenate([oi_ref[b, :, vs] for b in range(q.shape[0])], axis=0)
        o = qk * v[:, vs] + oi
        o_ref[:, vs] = (_rms(o, gain) * g[:, vs]).astype(o_ref.dtype)


def _step_common(Bt, H, K, V, a, q, k, v, g, gain, a_ref, k_ref, qa_ref, v_ref, oi_ref,
                 s_in_ref, s_out_ref, o_ref):
    _store_rows(a_ref, a)
    _store_rows(k_ref, k)
    _store_rows(qa_ref, q * a)
    _store_rows(v_ref, v)
    _step_rows(Bt, H, K, V, a_ref, k_ref, qa_ref, v_ref, s_in_ref, s_out_ref, oi_ref)
    _step_finish(H, K, V, q, k, v, g, gain, oi_ref, o_ref)


def _hgrn_step_kernel(*refs, Bt, H, K, V, aliased):
    (zq_ref, zi_ref, zg_ref, ldhi_ref, ldlo_ref, kh_ref, gain_ref, s_in_ref) = refs[:8]
    o_ref, s_out_ref, a_ref, k_ref, qa_ref, v_ref, oi_ref = refs[8 + aliased:]
    a = jnp.exp2(ldhi_ref[...].astype(f32) + ldlo_ref[...].astype(f32))
    _step_common(Bt, H, K, V, a, zq_ref[...].astype(f32), kh_ref[...], zi_ref[...].astype(f32),
                 zg_ref[...].astype(f32), gain_ref[...], a_ref, k_ref, qa_ref, v_ref, oi_ref,
                 s_in_ref, s_out_ref, o_ref)


def _gla_step_kernel(*refs, Bt, H, K, V, aliased):
    (gqk_ref, gv_ref, gr_ref, glr_ref, wg_ref, bg_ref, gain_ref, s_in_ref) = refs[:8]
    o_ref, s_out_ref, a_ref, k_ref, qa_ref, v_ref, oi_ref = refs[8 + aliased:]
    KW = H * K
    a = jnp.exp2(_gla_log2_decay(glr_ref[...], wg_ref[...], bg_ref[...]))
    _step_common(Bt, H, K, V, a, gqk_ref[:, :KW].astype(f32), gqk_ref[:, KW:].astype(f32),
                 gv_ref[...].astype(f32), gr_ref[...].astype(f32), gain_ref[...],
                 a_ref, k_ref, qa_ref, v_ref, oi_ref, s_in_ref, s_out_ref, o_ref)


def _step_call(kern, name, ins, in_specs, state, prev, layer, Bs, Bt, H, K, V, KW):
    depth = state.shape[0]
    VW = H * V
    aliased = prev is not None
    in_specs = list(in_specs) + [pl.BlockSpec((None, Bt, H, K, V), lambda i: (layer, i, 0, 0, 0))]
    args = list(ins) + [state]
    aliases = {}
    if aliased:
        in_specs.append(pl.BlockSpec(memory_space=pl.ANY))
        args.append(prev)
        aliases = {len(args) - 1: 1}
    return pl.pallas_call(
        functools.partial(kern, Bt=Bt, H=H, K=K, V=V, aliased=int(aliased)),
        grid=(Bs // Bt,),
        in_specs=in_specs,
        out_specs=[pl.BlockSpec((Bt, VW), lambda i: (i, 0)),
                   pl.BlockSpec((None, Bt, H, K, V), lambda i: (layer, i, 0, 0, 0))],
        out_shape=[jax.ShapeDtypeStruct((Bs, VW), bf16),
                   jax.ShapeDtypeStruct((depth, Bs, H, K, V), f32)],
        scratch_shapes=[pltpu.VMEM((Bt, 1, KW), f32)] * 3 + [pltpu.VMEM((Bt, 1, VW), f32)] * 2,
        input_output_aliases=aliases,
        compiler_params=_params(("arbitrary",)),
        name=name,
    )(*args)


def _hgrn_step(zb, ldhi, ldlo, kh, hgrn_norm, state, prev, layer, H, K, V):
    Bs = zb.shape[0]
    HW = H * K
    Bt = min(16, Bs)
    row = lambda col: (lambda i: (i, col))
    in_specs = [pl.BlockSpec((Bt, HW), row(0)),
                pl.BlockSpec((Bt, HW), row(1)),
                pl.BlockSpec((Bt, HW), row(2)),
                pl.BlockSpec((Bt, HW), row(0)),
                pl.BlockSpec((Bt, HW), row(0)),
                pl.BlockSpec((Bt, HW), row(0)),
                pl.BlockSpec((None, 1, V), lambda i: (layer, 0, 0))]
    return _step_call(_hgrn_step_kernel, "hgrn_step", (zb, zb, zb, ldhi, ldlo, kh, hgrn_norm),
                      in_specs, state, prev, layer, Bs, Bt, H, K, V, HW)


def _gla_step(zb, glr, wgg, bgg, gla_norm, state, prev, layer, H, K, V):
    Bs = zb.shape[0]
    KW, VW = H * K, H * V
    Bt = min(16, Bs)
    row = lambda col: (lambda i: (i, col))
    in_specs = [pl.BlockSpec((Bt, 2 * KW), row(3)),
                pl.BlockSpec((Bt, VW), row(4)),
                pl.BlockSpec((Bt, VW), row(5)),
                pl.BlockSpec((Bt, LANES), row(0)),
                pl.BlockSpec((None, LANES, KW), lambda i: (layer, 0, 0)),
                pl.BlockSpec((None, 1, KW), lambda i: (layer, 0, 0)),
                pl.BlockSpec((None, 1, V), lambda i: (layer, 0, 0))]
    return _step_call(_gla_step_kernel, "gla_step", (zb, zb, zb, glr, wgg, bgg, gla_norm),
                      in_specs, state, prev, layer, Bs, Bt, H, K, V, KW)


def _postmix_kernel(oh_ref, og_ref, mh_ref, mg_ref, h_ref, whu_ref, wgu_ref, wout_ref,
                    npost_ref, npre_ref, h1_ref, c_ref):
    yh = _dot(oh_ref[...], whu_ref[...])
    yg = _dot(og_ref[...], wgu_ref[...])
    merged = mh_ref[...].astype(f32) * yh + mg_ref[...].astype(f32) * yg
    t = _dot(merged.astype(bf16), wout_ref[...])
    h1 = h_ref[...] + _rms(t, npost_ref[...])
    h1_ref[...] = h1
    c_ref[...] = _rms(h1, npre_ref[...]).astype(c_ref.dtype)


def _postmix(oh, og, zm, h, whu, wgu, wout, npost, npre, layer):
    M, D = h.shape
    HW, VW = oh.shape[1], og.shape[1]
    tm = _pick(M, (256, 128))
    lay = lambda m: (layer, 0, 0)
    return pl.pallas_call(
        _postmix_kernel,
        grid=(M // tm,),
        in_specs=[pl.BlockSpec((tm, HW), lambda m: (m, 0)),
                  pl.BlockSpec((tm, VW), lambda m: (m, 0)),
                  pl.BlockSpec((tm, D), lambda m: (m, 0)),
                  pl.BlockSpec((tm, D), lambda m: (m, 1)),
                  pl.BlockSpec((tm, D), lambda m: (m, 0)),
                  _const_spec((None, HW, D), lay),
                  _const_spec((None, VW, D), lay),
                  _const_spec((None, D, D), lay),
                  pl.BlockSpec((None, 1, D), lay),
                  pl.BlockSpec((None, 1, D), lay)],
        out_specs=[pl.BlockSpec((tm, D), lambda m: (m, 0)),
                   pl.BlockSpec((tm, D), lambda m: (m, 0))],
        out_shape=[jax.ShapeDtypeStruct((M, D), f32),
                   jax.ShapeDtypeStruct((M, D), bf16)],
        compiler_params=_params(("parallel",)),
        name="postmix",
    )(oh, og, zm, zm, h, whu, wgu, wout, npost, npre)


def _ple_kernel(t_ref, h_ref, p_ref, wg_ref, wp_ref, npost_ref, nnext_ref, h3_ref, a_ref):
    h = h_ref[...] + _rms(t_ref[...], npost_ref[...])
    gate = _sigmoid(_dot(h.astype(bf16), wg_ref[...]))
    pe = _dot(p_ref[...].astype(bf16), wp_ref[...])
    h3 = h + gate * pe
    h3_ref[...] = h3
    a_ref[...] = _rms(h3, nnext_ref[...]).astype(a_ref.dtype)


def _ple(t2, h1, p, wg, wp, npost, nnext, layer, next_layer):
    M, D = h1.shape
    P = p.shape[-1]
    tm = _pick(M, (512, 256, 128))
    lay = lambda m: (layer, 0, 0)
    return pl.pallas_call(
        _ple_kernel,
        grid=(M // tm,),
        in_specs=[pl.BlockSpec((tm, D), lambda m: (m, 0)),
                  pl.BlockSpec((tm, D), lambda m: (m, 0)),
                  pl.BlockSpec((None, tm, P), lambda m: (layer, m, 0)),
                  _const_spec((None, D, D), lay),
                  _const_spec((None, P, D), lay),
                  pl.BlockSpec((None, 1, D), lay),
                  pl.BlockSpec((None, 1, D), lambda m: (next_layer, 0, 0))],
        out_specs=[pl.BlockSpec((tm, D), lambda m: (m, 0)),
                   pl.BlockSpec((tm, D), lambda m: (m, 0))],
        out_shape=[jax.ShapeDtypeStruct((M, D), f32),
                   jax.ShapeDtypeStruct((M, D), bf16)],
        compiler_params=_params(("parallel",)),
        name="ple",
    )(t2, h1, p, wg, wp, npost, nnext)


def kernel(x_prompt, x_sample, p_prompt, p_sample, state_hgrn, state_gla, norm_pre_mix,
           norm_post_mix, norm_pre_ffn, norm_post_ffn, w_in, lb_param, hgrn_norm, w_hgrn_up,
           w_gla_gate, b_gla_gate, gla_norm, w_gla_up, w_out, w_ff1, w_ff2, w_ple, w_ple_gate):
    B, L, D = x_prompt.shape
    Bs = x_sample.shape[0]
    depth = w_in.shape[0]
    _, _, HH, HK, HV = state_hgrn.shape
    _, _, GH, GK, GV = state_gla.shape
    HW, GKW, GVW = HH * HK, GH * GK, GH * GV
    R = w_gla_gate.shape[1]
    F = w_ff1.shape[2]
    assert HK == LANES and GK == LANES and HV % LANES == 0 and GV % LANES == 0
    assert HH * HV == HW and 2 * GKW == HW and GVW == HW and R <= LANES
    assert x_sample.shape[1] == 1 and L % CHUNK == 0 and D % HW == 0
    mixer_cols = 4 * HW + 2 * GKW + 2 * GVW

    w_glr = jnp.concatenate([w_in[:, :, mixer_cols:mixer_cols + R],
                             jnp.zeros((depth, D, LANES - R), w_in.dtype)], axis=2)
    w_mg = w_in[:, :, mixer_cols + R:]
    whu = w_hgrn_up.astype(bf16)
    wgu = w_gla_up.astype(bf16)
    wout = w_out.astype(bf16)
    wpg = w_ple_gate.astype(bf16)
    wp = w_ple.astype(bf16)
    wgg = jnp.concatenate([w_gla_gate.astype(f32), jnp.zeros((depth, LANES - R, GKW), f32)], axis=1)

    lbs = jnp.cumsum(jax.nn.softmax(lb_param.astype(f32), axis=0), axis=0)
    lbs = (lbs - lbs[0:1]).reshape(depth, 1, HW)
    r3 = lambda t: t.astype(f32).reshape(depth, 1, t.shape[-1])
    n_pre_mix, n_post_mix, n_pre_ffn, n_post_ffn = map(
        r3, (norm_pre_mix, norm_post_mix, norm_pre_ffn, norm_post_ffn))
    hn, gn, bgg = r3(hgrn_norm), r3(gla_norm), r3(b_gla_gate)
    mall, lvl = _level_tables()
    ep_main = functools.partial(_ep_main, scale=GK ** -0.5)
    tm_mg = _pick(D, (1024, 512, 256, 128))
    tf = _pick(F, (1024, 512, 256, 128))
    tn2 = _pick(D, (512, 256, 128))

    def in_projections(a):
        zb, = _proj(a, w_in, i, lambda n: n + jnp.minimum(n, 1), 6, HW, ep_main, (), (),
                    (HW,), (bf16,), "in_proj")
        ldhi, ldlo, kh = _proj(a, w_in, i, lambda n: 1, 1, HW, _ep_forget, (lbs,),
                               (pl.BlockSpec((None, 1, HW), lambda n, m: (i, 0, 0)),),
                               (HW, HW, HW), (bf16, bf16, f32), "in_proj_forget",
                               tm_cands=(512, 256, 128))
        zm, = _proj(a, w_mg, i, lambda n: n, 2 * D // tm_mg, tm_mg, _ep_sigmoid, (), (),
                    (tm_mg,), (bf16,), "in_proj_merge")
        glr, = _proj(a, w_glr, i, lambda n: 0, 1, LANES, _ep_copy, (), (), (LANES,), (f32,),
                     "in_proj_lowrank")
        return zb, ldhi, ldlo, kh, zm, glr

    def dense_tail(h, oh, og, zm, p):
        h1, c = _postmix(oh, og, zm, h, whu, wgu, wout, n_post_mix, n_pre_ffn, i)
        u, = _proj(c, w_ff1, i, lambda n: n, F // tf, tf, _ep_relu2, (), (), (tf,), (bf16,),
                   "ffn_up")
        t2, = _proj(u, w_ff2, i, lambda n: n, D // tn2, tn2, _ep_copy, (), (), (tn2,), (f32,),
                    "ffn_down", tm_cands=(512, 256, 128), w_single=True)
        return _ple(t2, h1, p, wpg, wp, n_post_ffn, n_pre_mix, i, (i + 1) % depth)

    hp = x_prompt.reshape(B * L, D)
    hs = x_sample.reshape(Bs, D)
    pp = p_prompt.reshape(depth, B * L, -1)
    ps = p_sample.reshape(depth, Bs, -1)
    ap = _norm(hp, n_pre_mix, 0)
    as_ = _norm(hs, n_pre_mix, 0)

    hgrn_p, gla_p = [], []
    hgrn_s = gla_s = None
    for i in range(depth):
        zb, ldhi, ldlo, kh, zm, glr = in_projections(ap)
        oh, sh = _hgrn_prompt(zb, ldhi, ldlo, kh, hn, mall, lvl, i, B, L, HH, HK, HV)
        og, sg = _gla_prompt(zb, glr, wgg, bgg, gn, mall, lvl, i, B, L, GH, GK, GV)
        hp, ap = dense_tail(hp, oh, og, zm, pp)
        hgrn_p.append(sh)
        gla_p.append(sg)

        zb, ldhi, ldlo, kh, zm, glr = in_projections(as_)
        oh, hgrn_s = _hgrn_step(zb, ldhi, ldlo, kh, hn, state_hgrn, hgrn_s, i, HH, HK, HV)
        og, gla_s = _gla_step(zb, glr, wgg, bgg, gn, state_gla, gla_s, i, GH, GK, GV)
        hs, as_ = dense_tail(hs, oh, og, zm, ps)

    return (hp.reshape(B, L, D), hs.reshape(Bs, 1, D),
            jnp.stack(hgrn_p).astype(state_hgrn.dtype), jnp.stack(gla_p).astype(state_gla.dtype),
            hgrn_s.astype(state_hgrn.dtype), gla_s.astype(state_gla.dtype))
```

```python
import functools

import numpy as np
import jax
import jax.numpy as jnp
from jax import lax
from jax.experimental import pallas as pl
from jax.experimental.pallas import tpu as pltpu

EPS = 1e-6
GLA_GATE_NORM = 16.0
LOG2E = 1.4426950408889634
LANES = 128
CHUNK = 64
N_LEVELS = 6
VMEM_LIMIT = 56 * 1024 * 1024

f32 = jnp.float32
bf16 = jnp.bfloat16


def _sigmoid(x):
    return 1.0 / (1.0 + jnp.exp(-x))


def _silu(x):
    return x * _sigmoid(x)


def _log_sigmoid(x):
    return jnp.minimum(x, 0.0) - jnp.log(1.0 + jnp.exp(-jnp.abs(x)))


def _rms(x, g):
    return x * lax.rsqrt(jnp.mean(x * x, axis=-1, keepdims=True) + EPS) * g


def _dot(a, b):
    return jnp.dot(a, b, preferred_element_type=f32)


def _dot_nt(a, b):
    return lax.dot_general(a, b, (((1,), (1,)), ((), ())), preferred_element_type=f32)


def _dot_tn(a, b):
    return lax.dot_general(a, b, (((0,), (0,)), ((), ())), preferred_element_type=f32)


def _split2(x):
    hi = x.astype(bf16)
    lo = (x - hi.astype(f32)).astype(bf16)
    return hi, lo


def _params(sem):
    return pltpu.CompilerParams(dimension_semantics=sem, vmem_limit_bytes=VMEM_LIMIT)


def _const_spec(shape, index_map):
    return pl.BlockSpec(shape, index_map, pipeline_mode=pl.Buffered(1))


def _pick(n, cands):
    for c in cands:
        if n % c == 0:
            return c
    return n


def _level_tables():
    C = CHUNK
    t = np.arange(C)[:, None]
    j = np.arange(C)[None, :]
    mats = [(j <= t), (j > t)]
    lvl = np.full((C, C), -1, np.int32)
    lvl[t == j] = 0
    w = C // 2
    level = 1
    while w >= 1:
        start = (t // (2 * w)) * (2 * w)
        m = start + w - 1
        second = (t - start) >= w
        mats.append(np.where(second, (j > m) & (j <= t), (j > t) & (j <= m)))
        same = (t // (2 * w)) == (j // (2 * w))
        lvl[same & second & ((j - (j // (2 * w)) * (2 * w)) < w)] = level
        w //= 2
        level += 1
    mall = np.concatenate(mats, axis=0).astype(np.float32)
    return jnp.asarray(np.concatenate([mall, mall], axis=1), bf16), jnp.asarray(lvl)


def _norm_kernel(x_ref, g_ref, o_ref):
    o_ref[...] = _rms(x_ref[...], g_ref[...]).astype(o_ref.dtype)


def _norm(x, gains, layer):
    M, D = x.shape
    tm = _pick(M, (512, 256, 128))
    return pl.pallas_call(
        _norm_kernel,
        grid=(M // tm,),
        in_specs=[pl.BlockSpec((tm, D), lambda m: (m, 0)),
                  pl.BlockSpec((None, 1, D), lambda m: (layer, 0, 0))],
        out_specs=pl.BlockSpec((tm, D), lambda m: (m, 0)),
        out_shape=jax.ShapeDtypeStruct((M, D), bf16),
        compiler_params=_params(("parallel",)),
        name="rmsnorm",
    )(x, gains)


def _proj_kernel(*refs, n_extra, epilogue, transposed, tn, slab):
    a_ref, w_ref = refs[0], refs[1]
    extra = refs[2:2 + n_extra]
    outs = refs[2 + n_extra:-1]
    wbf_ref = refs[-1]

    @pl.when(pl.program_id(1) == 0)
    def _():
        w = w_ref[0] if transposed else w_ref[...]
        wbf_ref[...] = w.astype(bf16)

    a = a_ref[...]
    for j in range(tn // slab):
        cols = slice(j * slab, (j + 1) * slab)
        if transposed:
            acc = _dot_nt(a, wbf_ref[cols, :])
        else:
            acc = _dot(a, wbf_ref[:, cols])
        epilogue(acc, cols, extra, outs)


def _proj(a, w, layer, w_off, n_tiles, tn, epilogue, extra, extra_specs, out_dtypes, name,
          transposed, tm_cands=(1024, 512, 256, 128), w_single=False):
    M, K = a.shape
    tm = _pick(M, tm_cands)
    if transposed:
        w_shape = (pl.Element(1), pl.Element(tn), pl.Element(K))
        def w_map(n, m):
            off = w_off(n)
            return (layer, off if isinstance(off, int) else pl.multiple_of(off, 8), 0)
        scratch = pltpu.VMEM((tn, K), bf16)
    else:
        w_shape = (None, K, tn)
        w_map = lambda n, m: (layer, 0, w_off(n))
        scratch = pltpu.VMEM((K, tn), bf16)
    w_spec = _const_spec(w_shape, w_map) if w_single else pl.BlockSpec(w_shape, w_map)
    kern = functools.partial(_proj_kernel, n_extra=len(extra), epilogue=epilogue,
                             transposed=transposed, tn=tn, slab=min(tn, 2 * LANES))
    return pl.pallas_call(
        kern,
        grid=(n_tiles, M // tm),
        in_specs=[pl.BlockSpec((tm, K), lambda n, m: (m, 0)), w_spec] + list(extra_specs),
        out_specs=[pl.BlockSpec((tm, tn), lambda n, m: (m, n)) for _ in out_dtypes],
        out_shape=[jax.ShapeDtypeStruct((M, n_tiles * tn), dt) for dt in out_dtypes],
        scratch_shapes=[scratch],
        compiler_params=_params(("parallel", "arbitrary")),
        name=name,
    )(a, w, *extra)


def _hgrn_gates(fl, lb):
    fl2 = fl * LOG2E
    e = jnp.exp2(-jnp.abs(fl2))
    x2 = jnp.log2(1.0 - lb) + (jnp.minimum(fl2, 0.0) - jnp.log2(1.0 + e))
    x1 = jnp.log2(lb)
    log2_f = jnp.maximum(x1, x2) + jnp.log2(1.0 + jnp.exp2(-jnp.abs(x1 - x2)))
    r = 1.0 / (1.0 + e)
    k = (1.0 - lb) * jnp.where(fl >= 0.0, e * r, r)
    return log2_f, k


def _ep_main(acc, cols, extra, outs):
    silu_flag = extra[0][:, cols]
    scale = extra[1][:, cols]
    gate = jnp.where(silu_flag > 0.0, _sigmoid(acc), 1.0)
    outs[0][:, cols] = (acc * scale * gate).astype(outs[0].dtype)


def _ep_forget(acc, cols, extra, outs):
    ld_ref, k_ref = outs
    log2_f, k = _hgrn_gates(acc, extra[0][:, cols])
    ld_ref[:, cols] = log2_f
    k_ref[:, cols] = k


def _ep_sigmoid(acc, cols, extra, outs):
    outs[0][:, cols] = _sigmoid(acc).astype(outs[0].dtype)


def _ep_copy(acc, cols, extra, outs):
    outs[0][:, cols] = acc.astype(outs[0].dtype)


def _ep_relu2(acc, cols, extra, outs):
    outs[0][:, cols] = jnp.square(jnp.maximum(acc, 0.0)).astype(outs[0].dtype)


def _recurrence_chunks(T, H, K, V, q_ref, load_k, ld_ref, v_ref, g_ref, gain,
                       mall_ref, lvl_ref, st_ref, e_ref, o_ref):
    C = CHUNK

    def chunk(c, carry):
        r0 = pl.multiple_of(c * C, C)
        rows = pl.ds(r0, C)
        lvl = lvl_ref[...]
        hi, lo = _split2(ld_ref[rows, :])
        e_ref[...] = jnp.exp2(_dot(mall_ref[...], jnp.concatenate([hi, lo], axis=0)))
        for h in range(H):
            cs = slice(h * K, (h + 1) * K)
            vs = slice(h * V, (h + 1) * V)
            qc = q_ref[rows, cs].astype(f32)
            kc = load_k(rows, cs)
            vb = v_ref[rows, vs]
            st = st_ref[h]

            o = _dot_nt((qc * e_ref[0:C, cs]).astype(bf16), st.astype(bf16))

            a = jnp.where(lvl == 0, _dot_nt(qc.astype(bf16), kc.astype(bf16)), 0.0)
            for l in range(1, N_LEVELS + 1):
                el = e_ref[(l + 1) * C:(l + 2) * C, cs]
                al = _dot_nt((qc * el).astype(bf16), (kc * el).astype(bf16))
                a = jnp.where(lvl == l, al, a)
            o = o + _dot(a.astype(bf16), vb)

            kd = (kc * e_ref[C:2 * C, cs]).astype(bf16)
            st_ref[h] = st * e_ref[C - 1:C, cs] + _dot_tn(vb, kd)

            gate = g_ref[rows, vs].astype(f32)
            o_ref[rows, vs] = (_rms(o, gain) * gate).astype(o_ref.dtype)
        return carry

    lax.fori_loop(0, T // C, chunk, 0)


def _write_state(tb, st_ref, s_ref, H):
    @pl.when(tb == pl.num_programs(1) - 1)
    def _():
        for h in range(H):
            s_ref[0, h] = st_ref[h].T


def _hgrn_prompt_kernel(zq_ref, zi_ref, zg_ref, ld_ref, k_ref, gain_ref, mall_ref,
                        lvl_ref, o_ref, s_ref, st_ref, e_ref, *, T, H, K, V):
    tb = pl.program_id(1)

    @pl.when(tb == 0)
    def _():
        st_ref[...] = jnp.zeros_like(st_ref)

    _recurrence_chunks(T, H, K, V, zq_ref, lambda rows, cs: k_ref[rows, cs], ld_ref,
                       zi_ref, zg_ref, gain_ref[...], mall_ref, lvl_ref, st_ref, e_ref, o_ref)
    _write_state(tb, st_ref, s_ref, H)


def _gla_log2_decay(glr, wg, bg):
    g_hi, g_lo = _split2(glr)
    w_hi, w_lo = _split2(wg)
    pre = _dot(g_hi, w_hi) + _dot(g_hi, w_lo) + _dot(g_lo, w_hi) + bg
    return _log_sigmoid(pre) * (LOG2E / GLA_GATE_NORM)


def _gla_prompt_kernel(gqk_ref, gv_ref, gr_ref, glr_ref, wg_ref, bg_ref, gain_ref, mall_ref,
                       lvl_ref, o_ref, s_ref, st_ref, e_ref, ld_ref, *, T, H, K, V):
    tb = pl.program_id(1)

    @pl.when(tb == 0)
    def _():
        st_ref[...] = jnp.zeros_like(st_ref)

    ld_ref[...] = _gla_log2_decay(glr_ref[...], wg_ref[...], bg_ref[...])
    KW = H * K
    k_ref = gqk_ref.at[:, KW:2 * KW]
    _recurrence_chunks(T, H, K, V, gqk_ref, lambda rows, cs: k_ref[rows, cs].astype(f32),
                       ld_ref, gv_ref, gr_ref, gain_ref[...], mall_ref, lvl_ref,
                       st_ref, e_ref, o_ref)
    _write_state(tb, st_ref, s_ref, H)


def _hgrn_prompt(zb, ld, kh, hgrn_norm, mall, lvl, layer, B, L, H, K, V):
    HW = H * K
    T = min(256, L)
    nT = L // T
    tok = lambda col: (lambda b, t: (b * nT + t, col))
    cst = lambda b, t: (0, 0)
    kern = functools.partial(_hgrn_prompt_kernel, T=T, H=H, K=K, V=V)
    return pl.pallas_call(
        kern,
        grid=(B, nT),
        in_specs=[pl.BlockSpec((T, HW), tok(0)),
                  pl.BlockSpec((T, HW), tok(1)),
                  pl.BlockSpec((T, HW), tok(2)),
                  pl.BlockSpec((T, HW), tok(0)),
                  pl.BlockSpec((T, HW), tok(0)),
                  pl.BlockSpec((None, 1, V), lambda b, t: (layer, 0, 0)),
                  pl.BlockSpec(mall.shape, cst),
                  pl.BlockSpec(lvl.shape, cst)],
        out_specs=[pl.BlockSpec((T, HW), tok(0)),
                   pl.BlockSpec((1, H, K, V), lambda b, t: (b, 0, 0, 0))],
        out_shape=[jax.ShapeDtypeStruct((B * L, H * V), bf16),
                   jax.ShapeDtypeStruct((B, H, K, V), f32)],
        scratch_shapes=[pltpu.VMEM((H, V, K), f32),
                        pltpu.VMEM((mall.shape[0], HW), f32)],
        compiler_params=_params(("parallel", "arbitrary")),
        name="hgrn_prompt",
    )(zb, zb, zb, ld, kh, hgrn_norm, mall, lvl)


def _gla_prompt(zb, glr, wgg, bgg, gla_norm, mall, lvl, layer, B, L, H, K, V):
    KW, VW = H * K, H * V
    T = min(256, L)
    nT = L // T
    tok = lambda col: (lambda b, t: (b * nT + t, col))
    cst = lambda b, t: (0, 0)
    kern = functools.partial(_gla_prompt_kernel, T=T, H=H, K=K, V=V)
    return pl.pallas_call(
        kern,
        grid=(B, nT),
        in_specs=[pl.BlockSpec((T, 2 * KW), tok(3)),
                  pl.BlockSpec((T, VW), tok(4)),
                  pl.BlockSpec((T, VW), tok(5)),
                  pl.BlockSpec((T, LANES), tok(0)),
                  pl.BlockSpec((None, LANES, KW), lambda b, t: (layer, 0, 0)),
                  pl.BlockSpec((None, 1, KW), lambda b, t: (layer, 0, 0)),
                  pl.BlockSpec((None, 1, V), lambda b, t: (layer, 0, 0)),
                  pl.BlockSpec(mall.shape, cst),
                  pl.BlockSpec(lvl.shape, cst)],
        out_specs=[pl.BlockSpec((T, VW), tok(0)),
                   pl.BlockSpec((1, H, K, V), lambda b, t: (b, 0, 0, 0))],
        out_shape=[jax.ShapeDtypeStruct((B * L, VW), bf16),
                   jax.ShapeDtypeStruct((B, H, K, V), f32)],
        scratch_shapes=[pltpu.VMEM((H, V, K), f32),
                        pltpu.VMEM((mall.shape[0], KW), f32),
                        pltpu.VMEM((T, KW), f32)],
        compiler_params=_params(("parallel", "arbitrary")),
        name="gla_prompt",
    )(zb, zb, zb, glr, wgg, bgg, gla_norm, mall, lvl)


def _step_rows(Bt, H, K, V, a_ref, k_ref, qa_ref, v_ref, s_in_ref, s_out_ref, oi_ref):
    r = lax.broadcasted_iota(jnp.int32, (K, K), 0)
    c = lax.broadcasted_iota(jnp.int32, (K, K), 1)
    eye = r == c
    ones = jnp.ones((2 * K, LANES), bf16)
    r2 = lax.broadcasted_iota(jnp.int32, (2 * K, 2 * LANES), 0)
    c2 = lax.broadcasted_iota(jnp.int32, (2 * K, 2 * LANES), 1)
    ones_pair = jnp.where((r2 < K) == (c2 < LANES), 1.0, 0.0).astype(bf16)
    nv = V // LANES

    def diag(x):
        return jnp.where(eye, jnp.broadcast_to(x, (K, K)), 0.0)

    def body(b, carry):
        for h in range(H):
            cs = slice(h * K, (h + 1) * K)
            a = a_ref[b, :, cs]
            a_hi = a.astype(bf16).astype(f32)
            lhs_a = jnp.concatenate([diag(a_hi), diag(a - a_hi)], axis=1).astype(bf16)
            lhs_kq = jnp.concatenate([diag(k_ref[b, :, cs]), diag(qa_ref[b, :, cs])],
                                     axis=1).astype(bf16)
            a_col = _dot(lhs_a, ones)
            kq = _dot(lhs_kq, ones_pair)
            k_col, q_col = kq[:, :LANES], kq[:, LANES:]
            for j in range(nv):
                ls = slice(h * V + j * LANES, h * V + (j + 1) * LANES)
                vj = slice(j * LANES, (j + 1) * LANES)
                s = s_in_ref[b, h, :, vj]
                s_out_ref[b, h, :, vj] = a_col * s + k_col * v_ref[b, :, ls]
                oi_ref[b, :, ls] = jnp.sum(q_col * s, axis=0, keepdims=True)
        return carry

    lax.fori_loop(0, Bt, body, 0)


def _store_rows(ref, x):
    for b in range(x.shape[0]):
        ref[b] = x[b:b + 1, :]


def _step_finish(H, K, V, q, k, v, g, gain, oi_ref, o_ref):
    ones = jnp.ones((K, LANES), bf16)
    nv = V // LANES
    for h in range(H):
        cs = slice(h * K, (h + 1) * K)
        vs = slice(h * V, (h + 1) * V)
        qk = _dot((q[:, cs] * k[:, cs]).astype(bf16), ones)
        if nv > 1:
            qk = jnp.concatenate([qk] * nv, axis=1)
        oi = jnp.concatenate([oi_ref[b, :, vs] for b in range(q.shape[0])], axis=0)
        o = qk * v[:, vs] + oi
        o_ref[:, vs] = (_rms(o, gain) * g[:, vs]).astype(o_ref.dtype)


def _step_common(Bt, H, K, V, a, q, k, v, g, gain, a_ref, k_ref, qa_ref, v_ref, oi_ref,
                 s_in_ref, s_out_ref, o_ref):
    _store_rows(a_ref, a)
    _store_rows(k_ref, k)
    _store_rows(qa_ref, q * a)
    _store_rows(v_ref, v)
    _step_rows(Bt, H, K, V, a_ref, k_ref, qa_ref, v_ref, s_in_ref, s_out_ref, oi_ref)
    _step_finish(H, K, V, q, k, v, g, gain, oi_ref, o_ref)


def _hgrn_step_kernel(*refs, Bt, H, K, V, aliased):
    (zq_ref, zi_ref, zg_ref, ld_ref, kh_ref, gain_ref, s_in_ref) = refs[:7]
    o_ref, s_out_ref, a_ref, k_ref, qa_ref, v_ref, oi_ref = refs[7 + aliased:]
    a = jnp.exp2(ld_ref[...])
    _step_common(Bt, H, K, V, a, zq_ref[...].astype(f32), kh_ref[...], zi_ref[...].astype(f32),
                 zg_ref[...].astype(f32), gain_ref[...], a_ref, k_ref, qa_ref, v_ref, oi_ref,
                 s_in_ref, s_out_ref, o_ref)


def _gla_step_kernel(*refs, Bt, H, K, V, aliased):
    (gqk_ref, gv_ref, gr_ref, glr_ref, wg_ref, bg_ref, gain_ref, s_in_ref) = refs[:8]
    o_ref, s_out_ref, a_ref, k_ref, qa_ref, v_ref, oi_ref = refs[8 + aliased:]
    KW = H * K
    a = jnp.exp2(_gla_log2_decay(glr_ref[...], wg_ref[...], bg_ref[...]))
    _step_common(Bt, H, K, V, a, gqk_ref[:, :KW].astype(f32), gqk_ref[:, KW:].astype(f32),
                 gv_ref[...].astype(f32), gr_ref[...].astype(f32), gain_ref[...],
                 a_ref, k_ref, qa_ref, v_ref, oi_ref, s_in_ref, s_out_ref, o_ref)


def _step_call(kern, name, ins, in_specs, state, prev, layer, Bs, Bt, H, K, V, KW):
    depth = state.shape[0]
    VW = H * V
    aliased = prev is not None
    in_specs = list(in_specs) + [pl.BlockSpec((None, Bt, H, K, V), lambda i: (layer, i, 0, 0, 0))]
    args = list(ins) + [state]
    aliases = {}
    if aliased:
        in_specs.append(pl.BlockSpec(memory_space=pl.ANY))
        args.append(prev)
        aliases = {len(args) - 1: 1}
    return pl.pallas_call(
        functools.partial(kern, Bt=Bt, H=H, K=K, V=V, aliased=int(aliased)),
        grid=(Bs // Bt,),
        in_specs=in_specs,
        out_specs=[pl.BlockSpec((Bt, VW), lambda i: (i, 0)),
                   pl.BlockSpec((None, Bt, H, K, V), lambda i: (layer, i, 0, 0, 0))],
        out_shape=[jax.ShapeDtypeStruct((Bs, VW), bf16),
                   jax.ShapeDtypeStruct((depth, Bs, H, K, V), f32)],
        scratch_shapes=[pltpu.VMEM((Bt, 1, KW), f32)] * 3 + [pltpu.VMEM((Bt, 1, VW), f32)] * 2,
        input_output_aliases=aliases,
        compiler_params=_params(("arbitrary",)),
        name=name,
    )(*args)


def _hgrn_step(zb, ld, kh, hgrn_norm, state, prev, layer, H, K, V):
    Bs = zb.shape[0]
    HW = H * K
    Bt = min(16, Bs)
    row = lambda col: (lambda i: (i, col))
    in_specs = [pl.BlockSpec((Bt, HW), row(0)),
                pl.BlockSpec((Bt, HW), row(1)),
                pl.BlockSpec((Bt, HW), row(2)),
                pl.BlockSpec((Bt, HW), row(0)),
                pl.BlockSpec((Bt, HW), row(0)),
                pl.BlockSpec((None, 1, V), lambda i: (layer, 0, 0))]
    return _step_call(_hgrn_step_kernel, "hgrn_step", (zb, zb, zb, ld, kh, hgrn_norm),
                      in_specs, state, prev, layer, Bs, Bt, H, K, V, HW)


def _gla_step(zb, glr, wgg, bgg, gla_norm, state, prev, layer, H, K, V):
    Bs = zb.shape[0]
    KW, VW = H * K, H * V
    Bt = min(16, Bs)
    row = lambda col: (lambda i: (i, col))
    in_specs = [pl.BlockSpec((Bt, 2 * KW), row(3)),
                pl.BlockSpec((Bt, VW), row(4)),
                pl.BlockSpec((Bt, VW), row(5)),
                pl.BlockSpec((Bt, LANES), row(0)),
                pl.BlockSpec((None, LANES, KW), lambda i: (layer, 0, 0)),
                pl.BlockSpec((None, 1, KW), lambda i: (layer, 0, 0)),
                pl.BlockSpec((None, 1, V), lambda i: (layer, 0, 0))]
    return _step_call(_gla_step_kernel, "gla_step", (zb, zb, zb, glr, wgg, bgg, gla_norm),
                      in_specs, state, prev, layer, Bs, Bt, H, K, V, KW)


def _postmix_kernel(oh_ref, og_ref, mh_ref, mg_ref, h_ref, whu_ref, wgu_ref, wout_ref,
                    npost_ref, npre_ref, h1_ref, c_ref):
    yh = _dot(oh_ref[...], whu_ref[...])
    yg = _dot(og_ref[...], wgu_ref[...])
    merged = mh_ref[...].astype(f32) * yh + mg_ref[...].astype(f32) * yg
    t = _dot(merged.astype(bf16), wout_ref[...])
    h1 = h_ref[...] + _rms(t, npost_ref[...])
    h1_ref[...] = h1
    c_ref[...] = _rms(h1, npre_ref[...]).astype(c_ref.dtype)


def _postmix(oh, og, zm, h, whu, wgu, wout, npost, npre, layer):
    M, D = h.shape
    HW, VW = oh.shape[1], og.shape[1]
    tm = _pick(M, (256, 128))
    lay = lambda m: (layer, 0, 0)
    return pl.pallas_call(
        _postmix_kernel,
        grid=(M // tm,),
        in_specs=[pl.BlockSpec((tm, HW), lambda m: (m, 0)),
                  pl.BlockSpec((tm, VW), lambda m: (m, 0)),
                  pl.BlockSpec((tm, D), lambda m: (m, 0)),
                  pl.BlockSpec((tm, D), lambda m: (m, 1)),
                  pl.BlockSpec((tm, D), lambda m: (m, 0)),
                  _const_spec((None, HW, D), lay),
                  _const_spec((None, VW, D), lay),
                  _const_spec((None, D, D), lay),
                  pl.BlockSpec((None, 1, D), lay),
                  pl.BlockSpec((None, 1, D), lay)],
        out_specs=[pl.BlockSpec((tm, D), lambda m: (m, 0)),
                   pl.BlockSpec((tm, D), lambda m: (m, 0))],
        out_shape=[jax.ShapeDtypeStruct((M, D), f32),
                   jax.ShapeDtypeStruct((M, D), bf16)],
        compiler_params=_params(("parallel",)),
        name="postmix",
    )(oh, og, zm, zm, h, whu, wgu, wout, npost, npre)


def _ple_kernel(t_ref, h_ref, p_ref, wg_ref, wp_ref, npost_ref, nnext_ref, h3_ref, a_ref):
    h = h_ref[...] + _rms(t_ref[...], npost_ref[...])
    gate = _sigmoid(_dot(h.astype(bf16), wg_ref[...]))
    pe = _dot(p_ref[...].astype(bf16), wp_ref[...])
    h3 = h + gate * pe
    h3_ref[...] = h3
    a_ref[...] = _rms(h3, nnext_ref[...]).astype(a_ref.dtype)


def _ple(t2, h1, p, wg, wp, npost, nnext, layer, next_layer):
    M, D = h1.shape
    P = p.shape[-1]
    tm = _pick(M, (512, 256, 128))
    lay = lambda m: (layer, 0, 0)
    return pl.pallas_call(
        _ple_kernel,
        grid=(M // tm,),
        in_specs=[pl.BlockSpec((tm, D), lambda m: (m, 0)),
                  pl.BlockSpec((tm, D), lambda m: (m, 0)),
                  pl.BlockSpec((None, tm, P), lambda m: (layer, m, 0)),
                  _const_spec((None, D, D), lay),
                  _const_spec((None, P, D), lay),
                  pl.BlockSpec((None, 1, D), lay),
                  pl.BlockSpec((None, 1, D), lambda m: (next_layer, 0, 0))],
        out_specs=[pl.BlockSpec((tm, D), lambda m: (m, 0)),
                   pl.BlockSpec((tm, D), lambda m: (m, 0))],
        out_shape=[jax.ShapeDtypeStruct((M, D), f32),
                   jax.ShapeDtypeStruct((M, D), bf16)],
        compiler_params=_params(("parallel",)),
        name="ple",
    )(t2, h1, p, wg, wp, npost, nnext)


def kernel(x_prompt, x_sample, p_prompt, p_sample, state_hgrn, state_gla, norm_pre_mix,
           norm_post_mix, norm_pre_ffn, norm_post_ffn, w_in, lb_param, hgrn_norm, w_hgrn_up,
           w_gla_gate, b_gla_gate, gla_norm, w_gla_up, w_out, w_ff1, w_ff2, w_ple, w_ple_gate):
    B, L, D = x_prompt.shape
    Bs = x_sample.shape[0]
    depth = w_in.shape[0]
    _, _, HH, HK, HV = state_hgrn.shape
    _, _, GH, GK, GV = state_gla.shape
    HW, GKW, GVW = HH * HK, GH * GK, GH * GV
    R = w_gla_gate.shape[1]
    F = w_ff1.shape[2]
    assert HK == LANES and GK == LANES and HV % LANES == 0 and GV % LANES == 0
    assert HH * HV == HW and 2 * GKW == HW and GVW == HW and R <= LANES
    assert x_sample.shape[1] == 1 and L % CHUNK == 0 and D % HW == 0
    mixer_cols = 4 * HW + 2 * GKW + 2 * GVW

    assert w_in.shape[2] - mixer_cols >= LANES and (mixer_cols + R) % 8 == 0
    w_in_t = jnp.swapaxes(w_in, 1, 2)
    whu = w_hgrn_up.astype(bf16)
    wgu = w_gla_up.astype(bf16)
    wout = w_out.astype(bf16)
    wpg = w_ple_gate.astype(bf16)
    wp = w_ple.astype(bf16)
    wgg = jnp.concatenate([w_gla_gate.astype(f32), jnp.zeros((depth, LANES - R, GKW), f32)], axis=1)

    lbs = jnp.cumsum(jax.nn.softmax(lb_param.astype(f32), axis=0), axis=0)
    lbs = (lbs - lbs[0:1]).reshape(depth, 1, HW)
    r3 = lambda t: t.astype(f32).reshape(depth, 1, t.shape[-1])
    n_pre_mix, n_post_mix, n_pre_ffn, n_post_ffn = map(
        r3, (norm_pre_mix, norm_post_mix, norm_pre_ffn, norm_post_ffn))
    hn, gn, bgg = r3(hgrn_norm), r3(gla_norm), r3(b_gla_gate)
    mall, lvl = _level_tables()
    seg_w = (HW, HW, HW, GKW, GKW, GVW, GVW)
    silu_flag = jnp.asarray(np.concatenate(
        [np.full((1, w), v, np.float32) for w, v in zip(seg_w, (1, 0, 1, 0, 0, 0, 1))], axis=1))
    col_scale = jnp.asarray(np.concatenate(
        [np.full((1, w), v, np.float32) for w, v in zip(seg_w, (1, 1, 1, GK ** -0.5, 1, 1, 1))],
        axis=1))
    tm_mg = _pick(D, (1024, 512, 256, 128))
    tf = _pick(F, (1024, 512, 256, 128))
    tn2 = _pick(D, (512, 256, 128))

    def in_projections(a):
        col_spec = pl.BlockSpec((1, HW), lambda n, m: (0, n))
        zb, = _proj(a, w_in_t, i, lambda n: (n + jnp.minimum(n, 1)) * HW, 6, HW, _ep_main,
                    (silu_flag, col_scale), (col_spec, col_spec), (bf16,), "in_proj", True)
        ld, kh = _proj(a, w_in_t, i, lambda n: HW, 1, HW, _ep_forget, (lbs,),
                       (pl.BlockSpec((None, 1, HW), lambda n, m: (i, 0, 0)),),
                       (f32, f32), "in_proj_forget", True, tm_cands=(512, 256, 128))
        zm, = _proj(a, w_in_t, i, lambda n: mixer_cols + R + n * tm_mg, 2 * D // tm_mg, tm_mg,
                    _ep_sigmoid, (), (), (bf16,), "in_proj_merge", True)
        glr, = _proj(a, w_in_t, i, lambda n: mixer_cols, 1, LANES, _ep_copy, (), (), (f32,),
                     "in_proj_lowrank", True)
        return zb, ld, kh, zm, glr

    def dense_tail(h, oh, og, zm, p):
        h1, c = _postmix(oh, og, zm, h, whu, wgu, wout, n_post_mix, n_pre_ffn, i)
        u, = _proj(c, w_ff1, i, lambda n: n, F // tf, tf, _ep_relu2, (), (), (bf16,),
                   "ffn_up", False)
        t2, = _proj(u, w_ff2, i, lambda n: n, D // tn2, tn2, _ep_copy, (), (), (f32,),
                    "ffn_down", False, tm_cands=(512, 256, 128), w_single=True)
        return _ple(t2, h1, p, wpg, wp, n_post_ffn, n_pre_mix, i, (i + 1) % depth)

    hp = x_prompt.reshape(B * L, D)
    hs = x_sample.reshape(Bs, D)
    pp = p_prompt.reshape(depth, B * L, -1)
    ps = p_sample.reshape(depth, Bs, -1)
    ap = _norm(hp, n_pre_mix, 0)
    as_ = _norm(hs, n_pre_mix, 0)

    hgrn_p, gla_p = [], []
    hgrn_s = gla_s = None
    for i in range(depth):
        zb, ld, kh, zm, glr = in_projections(ap)
        oh, sh = _hgrn_prompt(zb, ld, kh, hn, mall, lvl, i, B, L, HH, HK, HV)
        og, sg = _gla_prompt(zb, glr, wgg, bgg, gn, mall, lvl, i, B, L, GH, GK, GV)
        hp, ap = dense_tail(hp, oh, og, zm, pp)
        hgrn_p.append(sh)
        gla_p.append(sg)

        zb, ld, kh, zm, glr = in_projections(as_)
        oh, hgrn_s = _hgrn_step(zb, ld, kh, hn, state_hgrn, hgrn_s, i, HH, HK, HV)
        og, gla_s = _gla_step(zb, glr, wgg, bgg, gn, state_gla, gla_s, i, GH, GK, GV)
        hs, as_ = dense_tail(hs, oh, og, zm, ps)

    return (hp.reshape(B, L, D), hs.reshape(Bs, 1, D),
            jnp.stack(hgrn_p).astype(state_hgrn.dtype), jnp.stack(gla_p).astype(state_gla.dtype),
            hgrn_s.astype(state_hgrn.dtype), gla_s.astype(state_gla.dtype))
```

```python
import functools

import numpy as np
import jax
import jax.numpy as jnp
from jax import lax
from jax.experimental import pallas as pl
from jax.experimental.pallas import tpu as pltpu

EPS = 1e-6
GLA_GATE_NORM = 16.0
LOG2E = 1.4426950408889634
LANES = 128
CHUNK = 64
N_LEVELS = 6
VMEM_LIMIT = 56 * 1024 * 1024

f32 = jnp.float32
bf16 = jnp.bfloat16


def _sigmoid(x):
    return 1.0 / (1.0 + jnp.exp(-x))


def _silu(x):
    return x * _sigmoid(x)


def _log_sigmoid(x):
    return jnp.minimum(x, 0.0) - jnp.log(1.0 + jnp.exp(-jnp.abs(x)))


def _rms(x, g):
    return x * lax.rsqrt(jnp.mean(x * x, axis=-1, keepdims=True) + EPS) * g


def _dot(a, b):
    return jnp.dot(a, b, preferred_element_type=f32)


def _dot_nt(a, b):
    return lax.dot_general(a, b, (((1,), (1,)), ((), ())), preferred_element_type=f32)


def _dot_tn(a, b):
    return lax.dot_general(a, b, (((0,), (0,)), ((), ())), preferred_element_type=f32)


def _split2(x):
    hi = x.astype(bf16)
    lo = (x - hi.astype(f32)).astype(bf16)
    return hi, lo


def _params(sem):
    return pltpu.CompilerParams(dimension_semantics=sem, vmem_limit_bytes=VMEM_LIMIT)


def _const_spec(shape, index_map):
    return pl.BlockSpec(shape, index_map, pipeline_mode=pl.Buffered(1))


def _pick(n, cands):
    for c in cands:
        if n % c == 0:
            return c
    return n


def _row_tile(m, cap):
    for t in range(min(cap, m), 0, -1):
        if m % t == 0 and t % 16 == 0:
            return t
    return m


def _level_tables():
    C = CHUNK
    t = np.arange(C)[:, None]
    j = np.arange(C)[None, :]
    mats = [(j <= t), (j > t)]
    lvl = np.full((C, C), -1, np.int32)
    lvl[t == j] = 0
    w = C // 2
    level = 1
    while w >= 1:
        start = (t // (2 * w)) * (2 * w)
        m = start + w - 1
        second = (t - start) >= w
        mats.append(np.where(second, (j > m) & (j <= t), (j > t) & (j <= m)))
        same = (t // (2 * w)) == (j // (2 * w))
        lvl[same & second & ((j - (j // (2 * w)) * (2 * w)) < w)] = level
        w //= 2
        level += 1
    mall = np.concatenate(mats, axis=0).astype(np.float32)
    return jnp.asarray(np.concatenate([mall, mall], axis=1), bf16), jnp.asarray(lvl)


def _norm_kernel(x_ref, g_ref, o_ref):
    o_ref[...] = _rms(x_ref[...], g_ref[...]).astype(o_ref.dtype)


def _norm(x, gains, layer):
    M, D = x.shape
    tm = _row_tile(M, 512)
    return pl.pallas_call(
        _norm_kernel,
        grid=(M // tm,),
        in_specs=[pl.BlockSpec((tm, D), lambda m: (m, 0)),
                  pl.BlockSpec((None, 1, D), lambda m: (layer, 0, 0))],
        out_specs=pl.BlockSpec((tm, D), lambda m: (m, 0)),
        out_shape=jax.ShapeDtypeStruct((M, D), bf16),
        compiler_params=_params(("parallel",)),
        name="rmsnorm",
    )(x, gains)


def _proj_kernel(*refs, n_extra, epilogue, transposed, tn, slab):
    a_ref, w_ref = refs[0], refs[1]
    extra = refs[2:2 + n_extra]
    outs = refs[2 + n_extra:-1]
    wbf_ref = refs[-1]

    @pl.when(pl.program_id(1) == 0)
    def _():
        w = w_ref[0] if transposed else w_ref[...]
        wbf_ref[...] = w.astype(bf16)

    a = a_ref[...]
    for j in range(tn // slab):
        cols = slice(j * slab, (j + 1) * slab)
        if transposed:
            acc = _dot_nt(a, wbf_ref[cols, :])
        else:
            acc = _dot(a, wbf_ref[:, cols])
        epilogue(acc, cols, extra, outs)


def _proj(a, w, layer, w_off, n_tiles, tn, epilogue, extra, extra_specs, out_dtypes, name,
          transposed, tm_cap=1024, w_single=False):
    M, K = a.shape
    tm = _row_tile(M, tm_cap)
    if transposed:
        w_shape = (pl.Element(1), pl.Element(tn), pl.Element(K))
        def w_map(n, m):
            off = w_off(n)
            return (layer, off if isinstance(off, int) else pl.multiple_of(off, 8), 0)
        scratch = pltpu.VMEM((tn, K), bf16)
    else:
        w_shape = (None, K, tn)
        w_map = lambda n, m: (layer, 0, w_off(n))
        scratch = pltpu.VMEM((K, tn), bf16)
    w_spec = _const_spec(w_shape, w_map) if w_single else pl.BlockSpec(w_shape, w_map)
    kern = functools.partial(_proj_kernel, n_extra=len(extra), epilogue=epilogue,
                             transposed=transposed, tn=tn, slab=min(tn, 2 * LANES))
    return pl.pallas_call(
        kern,
        grid=(n_tiles, M // tm),
        in_specs=[pl.BlockSpec((tm, K), lambda n, m: (m, 0)), w_spec] + list(extra_specs),
        out_specs=[pl.BlockSpec((tm, tn), lambda n, m: (m, n)) for _ in out_dtypes],
        out_shape=[jax.ShapeDtypeStruct((M, n_tiles * tn), dt) for dt in out_dtypes],
        scratch_shapes=[scratch],
        compiler_params=_params(("parallel", "arbitrary")),
        name=name,
    )(a, w, *extra)


def _hgrn_gates(fl, lb):
    fl2 = fl * LOG2E
    e = jnp.exp2(-jnp.abs(fl2))
    x2 = jnp.log2(1.0 - lb) + (jnp.minimum(fl2, 0.0) - jnp.log2(1.0 + e))
    x1 = jnp.log2(lb)
    log2_f = jnp.maximum(x1, x2) + jnp.log2(1.0 + jnp.exp2(-jnp.abs(x1 - x2)))
    r = 1.0 / (1.0 + e)
    k = (1.0 - lb) * jnp.where(fl >= 0.0, e * r, r)
    return log2_f, k


def _ep_main(acc, cols, extra, outs):
    silu_flag = extra[0][:, cols]
    scale = extra[1][:, cols]
    gate = jnp.where(silu_flag > 0.0, _sigmoid(acc), 1.0)
    outs[0][:, cols] = (acc * scale * gate).astype(outs[0].dtype)


def _ep_forget(acc, cols, extra, outs):
    ld_ref, k_ref = outs
    log2_f, k = _hgrn_gates(acc, extra[0][:, cols])
    ld_ref[:, cols] = log2_f
    k_ref[:, cols] = k


def _ep_sigmoid(acc, cols, extra, outs):
    outs[0][:, cols] = _sigmoid(acc).astype(outs[0].dtype)


def _ep_copy(acc, cols, extra, outs):
    outs[0][:, cols] = acc.astype(outs[0].dtype)


def _ep_relu2(acc, cols, extra, outs):
    outs[0][:, cols] = jnp.square(jnp.maximum(acc, 0.0)).astype(outs[0].dtype)


def _recurrence_chunks(T, H, K, V, q_ref, load_k, ld_ref, v_ref, g_ref, gain,
                       mall_ref, lvl_ref, st_ref, e_ref, o_ref):
    C = CHUNK

    def chunk(c, carry):
        r0 = pl.multiple_of(c * C, C)
        rows = pl.ds(r0, C)
        lvl = lvl_ref[...]
        hi, lo = _split2(ld_ref[rows, :])
        e_ref[...] = jnp.exp2(_dot(mall_ref[...], jnp.concatenate([hi, lo], axis=0)))
        for h in range(H):
            cs = slice(h * K, (h + 1) * K)
            vs = slice(h * V, (h + 1) * V)
            qc = q_ref[rows, cs].astype(f32)
            kc = load_k(rows, cs)
            vb = v_ref[rows, vs]
            st = st_ref[h]

            o = _dot_nt((qc * e_ref[0:C, cs]).astype(bf16), st.astype(bf16))

            a = jnp.where(lvl == 0, _dot_nt(qc.astype(bf16), kc.astype(bf16)), 0.0)
            for l in range(1, N_LEVELS + 1):
                el = e_ref[(l + 1) * C:(l + 2) * C, cs]
                al = _dot_nt((qc * el).astype(bf16), (kc * el).astype(bf16))
                a = jnp.where(lvl == l, al, a)
            o = o + _dot(a.astype(bf16), vb)

            kd = (kc * e_ref[C:2 * C, cs]).astype(bf16)
            st_ref[h] = st * e_ref[C - 1:C, cs] + _dot_tn(vb, kd)

            gate = g_ref[rows, vs].astype(f32)
            o_ref[rows, vs] = (_rms(o, gain) * gate).astype(o_ref.dtype)
        return carry

    lax.fori_loop(0, T // C, chunk, 0)


def _write_state(tb, st_ref, s_ref, H):
    @pl.when(tb == pl.num_programs(1) - 1)
    def _():
        for h in range(H):
            s_ref[0, h] = st_ref[h].T


def _hgrn_prompt_kernel(zq_ref, zi_ref, zg_ref, ld_ref, k_ref, gain_ref, mall_ref,
                        lvl_ref, o_ref, s_ref, st_ref, e_ref, *, T, H, K, V):
    tb = pl.program_id(1)

    @pl.when(tb == 0)
    def _():
        st_ref[...] = jnp.zeros_like(st_ref)

    _recurrence_chunks(T, H, K, V, zq_ref, lambda rows, cs: k_ref[rows, cs], ld_ref,
                       zi_ref, zg_ref, gain_ref[...], mall_ref, lvl_ref, st_ref, e_ref, o_ref)
    _write_state(tb, st_ref, s_ref, H)


def _gla_log2_decay(glr, wg, bg):
    g_hi, g_lo = _split2(glr)
    w_hi, w_lo = _split2(wg)
    pre = _dot(g_hi, w_hi) + _dot(g_hi, w_lo) + _dot(g_lo, w_hi) + bg
    return _log_sigmoid(pre) * (LOG2E / GLA_GATE_NORM)


def _gla_prompt_kernel(gqk_ref, gv_ref, gr_ref, glr_ref, wg_ref, bg_ref, gain_ref, mall_ref,
                       lvl_ref, o_ref, s_ref, st_ref, e_ref, ld_ref, *, T, H, K, V):
    tb = pl.program_id(1)

    @pl.when(tb == 0)
    def _():
        st_ref[...] = jnp.zeros_like(st_ref)

    ld_ref[...] = _gla_log2_decay(glr_ref[...], wg_ref[...], bg_ref[...])
    KW = H * K
    k_ref = gqk_ref.at[:, KW:2 * KW]
    _recurrence_chunks(T, H, K, V, gqk_ref, lambda rows, cs: k_ref[rows, cs].astype(f32),
                       ld_ref, gv_ref, gr_ref, gain_ref[...], mall_ref, lvl_ref,
                       st_ref, e_ref, o_ref)
    _write_state(tb, st_ref, s_ref, H)


def _hgrn_prompt(zb, ld, kh, hgrn_norm, mall, lvl, layer, B, L, H, K, V):
    HW = H * K
    T = min(256, L)
    nT = L // T
    tok = lambda col: (lambda b, t: (b * nT + t, col))
    cst = lambda b, t: (0, 0)
    kern = functools.partial(_hgrn_prompt_kernel, T=T, H=H, K=K, V=V)
    return pl.pallas_call(
        kern,
        grid=(B, nT),
        in_specs=[pl.BlockSpec((T, HW), tok(0)),
                  pl.BlockSpec((T, HW), tok(1)),
                  pl.BlockSpec((T, HW), tok(2)),
                  pl.BlockSpec((T, HW), tok(0)),
                  pl.BlockSpec((T, HW), tok(0)),
                  pl.BlockSpec((None, 1, V), lambda b, t: (layer, 0, 0)),
                  pl.BlockSpec(mall.shape, cst),
                  pl.BlockSpec(lvl.shape, cst)],
        out_specs=[pl.BlockSpec((T, HW), tok(0)),
                   pl.BlockSpec((1, H, K, V), lambda b, t: (b, 0, 0, 0))],
        out_shape=[jax.ShapeDtypeStruct((zb.shape[0], H * V), bf16),
                   jax.ShapeDtypeStruct((B, H, K, V), f32)],
        scratch_shapes=[pltpu.VMEM((H, V, K), f32),
                        pltpu.VMEM((mall.shape[0], HW), f32)],
        compiler_params=_params(("parallel", "arbitrary")),
        name="hgrn_prompt",
    )(zb, zb, zb, ld, kh, hgrn_norm, mall, lvl)


def _gla_prompt(zb, glr, wgg, bgg, gla_norm, mall, lvl, layer, B, L, H, K, V):
    KW, VW = H * K, H * V
    T = min(256, L)
    nT = L // T
    tok = lambda col: (lambda b, t: (b * nT + t, col))
    cst = lambda b, t: (0, 0)
    kern = functools.partial(_gla_prompt_kernel, T=T, H=H, K=K, V=V)
    return pl.pallas_call(
        kern,
        grid=(B, nT),
        in_specs=[pl.BlockSpec((T, 2 * KW), tok(3)),
                  pl.BlockSpec((T, VW), tok(4)),
                  pl.BlockSpec((T, VW), tok(5)),
                  pl.BlockSpec((T, LANES), tok(0)),
                  pl.BlockSpec((None, LANES, KW), lambda b, t: (layer, 0, 0)),
                  pl.BlockSpec((None, 1, KW), lambda b, t: (layer, 0, 0)),
                  pl.BlockSpec((None, 1, V), lambda b, t: (layer, 0, 0)),
                  pl.BlockSpec(mall.shape, cst),
                  pl.BlockSpec(lvl.shape, cst)],
        out_specs=[pl.BlockSpec((T, VW), tok(0)),
                   pl.BlockSpec((1, H, K, V), lambda b, t: (b, 0, 0, 0))],
        out_shape=[jax.ShapeDtypeStruct((zb.shape[0], VW), bf16),
                   jax.ShapeDtypeStruct((B, H, K, V), f32)],
        scratch_shapes=[pltpu.VMEM((H, V, K), f32),
                        pltpu.VMEM((mall.shape[0], KW), f32),
                        pltpu.VMEM((T, KW), f32)],
        compiler_params=_params(("parallel", "arbitrary")),
        name="gla_prompt",
    )(zb, zb, zb, glr, wgg, bgg, gla_norm, mall, lvl)


def _step_rows(Bt, H, K, V, a_ref, k_ref, qa_ref, v_ref, s_in_ref, s_out_ref, oi_ref):
    r = lax.broadcasted_iota(jnp.int32, (K, K), 0)
    c = lax.broadcasted_iota(jnp.int32, (K, K), 1)
    eye = r == c
    ones = jnp.ones((2 * K, LANES), bf16)
    r2 = lax.broadcasted_iota(jnp.int32, (2 * K, 2 * LANES), 0)
    c2 = lax.broadcasted_iota(jnp.int32, (2 * K, 2 * LANES), 1)
    ones_pair = jnp.where((r2 < K) == (c2 < LANES), 1.0, 0.0).astype(bf16)
    nv = V // LANES

    def diag(x):
        return jnp.where(eye, jnp.broadcast_to(x, (K, K)), 0.0)

    def body(b, carry):
        for h in range(H):
            cs = slice(h * K, (h + 1) * K)
            a = a_ref[b, :, cs]
            a_hi = a.astype(bf16).astype(f32)
            lhs_a = jnp.concatenate([diag(a_hi), diag(a - a_hi)], axis=1).astype(bf16)
            lhs_kq = jnp.concatenate([diag(k_ref[b, :, cs]), diag(qa_ref[b, :, cs])],
                                     axis=1).astype(bf16)
            a_col = _dot(lhs_a, ones)
            kq = _dot(lhs_kq, ones_pair)
            k_col, q_col = kq[:, :LANES], kq[:, LANES:]
            for j in range(nv):
                ls = slice(h * V + j * LANES, h * V + (j + 1) * LANES)
                vj = slice(j * LANES, (j + 1) * LANES)
                s = s_in_ref[b, h, :, vj]
                s_out_ref[b, h, :, vj] = a_col * s + k_col * v_ref[b, :, ls]
                oi_ref[b, :, ls] = jnp.sum(q_col * s, axis=0, keepdims=True)
        return carry

    lax.fori_loop(0, Bt, body, 0)


def _store_rows(ref, x):
    for b in range(x.shape[0]):
        ref[b] = x[b:b + 1, :]


def _step_finish(H, K, V, q, k, v, g, gain, oi_ref, o_ref):
    ones = jnp.ones((K, LANES), bf16)
    nv = V // LANES
    for h in range(H):
        cs = slice(h * K, (h + 1) * K)
        vs = slice(h * V, (h + 1) * V)
        qk = _dot((q[:, cs] * k[:, cs]).astype(bf16), ones)
        if nv > 1:
            qk = jnp.concatenate([qk] * nv, axis=1)
        oi = jnp.concatenate([oi_ref[b, :, vs] for b in range(q.shape[0])], axis=0)
        o = qk * v[:, vs] + oi
        o_ref[:, vs] = (_rms(o, gain) * g[:, vs]).astype(o_ref.dtype)


def _step_common(Bt, H, K, V, a, q, k, v, g, gain, a_ref, k_ref, qa_ref, v_ref, oi_ref,
                 s_in_ref, s_out_ref, o_ref):
    _store_rows(a_ref, a)
    _store_rows(k_ref, k)
    _store_rows(qa_ref, q * a)
    _store_rows(v_ref, v)
    _step_rows(Bt, H, K, V, a_ref, k_ref, qa_ref, v_ref, s_in_ref, s_out_ref, oi_ref)
    _step_finish(H, K, V, q, k, v, g, gain, oi_ref, o_ref)


def _hgrn_step_kernel(*refs, Bt, H, K, V, aliased):
    (zq_ref, zi_ref, zg_ref, ld_ref, kh_ref, gain_ref, s_in_ref) = refs[:7]
    o_ref, s_out_ref, a_ref, k_ref, qa_ref, v_ref, oi_ref = refs[7 + aliased:]
    a = jnp.exp2(ld_ref[...])
    _step_common(Bt, H, K, V, a, zq_ref[...].astype(f32), kh_ref[...], zi_ref[...].astype(f32),
                 zg_ref[...].astype(f32), gain_ref[...], a_ref, k_ref, qa_ref, v_ref, oi_ref,
                 s_in_ref, s_out_ref, o_ref)


def _gla_step_kernel(*refs, Bt, H, K, V, aliased):
    (gqk_ref, gv_ref, gr_ref, glr_ref, wg_ref, bg_ref, gain_ref, s_in_ref) = refs[:8]
    o_ref, s_out_ref, a_ref, k_ref, qa_ref, v_ref, oi_ref = refs[8 + aliased:]
    KW = H * K
    a = jnp.exp2(_gla_log2_decay(glr_ref[...], wg_ref[...], bg_ref[...]))
    _step_common(Bt, H, K, V, a, gqk_ref[:, :KW].astype(f32), gqk_ref[:, KW:].astype(f32),
                 gv_ref[...].astype(f32), gr_ref[...].astype(f32), gain_ref[...],
                 a_ref, k_ref, qa_ref, v_ref, oi_ref, s_in_ref, s_out_ref, o_ref)


def _step_call(kern, name, ins, in_specs, state, o_prev, s_prev, layer, row0, Bt, H, K, V, KW):
    depth, Bs = state.shape[:2]
    VW = H * V
    blk0 = row0 // Bt
    in_specs = list(in_specs) + [pl.BlockSpec((None, Bt, H, K, V), lambda i: (layer, i, 0, 0, 0)),
                                 pl.BlockSpec(memory_space=pl.ANY)]
    args = list(ins) + [state, o_prev]
    aliases = {len(args) - 1: 0}
    if s_prev is not None:
        in_specs.append(pl.BlockSpec(memory_space=pl.ANY))
        args.append(s_prev)
        aliases[len(args) - 1] = 1
    return pl.pallas_call(
        functools.partial(kern, Bt=Bt, H=H, K=K, V=V, aliased=len(aliases)),
        grid=(Bs // Bt,),
        in_specs=in_specs,
        out_specs=[pl.BlockSpec((Bt, VW), lambda i: (blk0 + i, 0)),
                   pl.BlockSpec((None, Bt, H, K, V), lambda i: (layer, i, 0, 0, 0))],
        out_shape=[jax.ShapeDtypeStruct(o_prev.shape, bf16),
                   jax.ShapeDtypeStruct((depth, Bs, H, K, V), f32)],
        scratch_shapes=[pltpu.VMEM((Bt, 1, KW), f32)] * 3 + [pltpu.VMEM((Bt, 1, VW), f32)] * 2,
        input_output_aliases=aliases,
        compiler_params=_params(("arbitrary",)),
        name=name,
    )(*args)


def _hgrn_step(zb, ld, kh, hgrn_norm, state, o_prev, s_prev, layer, row0, H, K, V):
    HW = H * K
    Bt = min(16, state.shape[1])
    blk0 = row0 // Bt
    row = lambda col: (lambda i: (blk0 + i, col))
    in_specs = [pl.BlockSpec((Bt, HW), row(0)),
                pl.BlockSpec((Bt, HW), row(1)),
                pl.BlockSpec((Bt, HW), row(2)),
                pl.BlockSpec((Bt, HW), row(0)),
                pl.BlockSpec((Bt, HW), row(0)),
                pl.BlockSpec((None, 1, V), lambda i: (layer, 0, 0))]
    return _step_call(_hgrn_step_kernel, "hgrn_step", (zb, zb, zb, ld, kh, hgrn_norm),
                      in_specs, state, o_prev, s_prev, layer, row0, Bt, H, K, V, HW)


def _gla_step(zb, glr, wgg, bgg, gla_norm, state, o_prev, s_prev, layer, row0, H, K, V):
    KW, VW = H * K, H * V
    Bt = min(16, state.shape[1])
    blk0 = row0 // Bt
    row = lambda col: (lambda i: (blk0 + i, col))
    in_specs = [pl.BlockSpec((Bt, 2 * KW), row(3)),
                pl.BlockSpec((Bt, VW), row(4)),
                pl.BlockSpec((Bt, VW), row(5)),
                pl.BlockSpec((Bt, LANES), row(0)),
                pl.BlockSpec((None, LANES, KW), lambda i: (layer, 0, 0)),
                pl.BlockSpec((None, 1, KW), lambda i: (layer, 0, 0)),
                pl.BlockSpec((None, 1, V), lambda i: (layer, 0, 0))]
    return _step_call(_gla_step_kernel, "gla_step", (zb, zb, zb, glr, wgg, bgg, gla_norm),
                      in_specs, state, o_prev, s_prev, layer, row0, Bt, H, K, V, KW)


def _postmix_kernel(oh_ref, og_ref, mh_ref, mg_ref, h_ref, whu_ref, wgu_ref, wout_ref,
                    npost_ref, npre_ref, h1_ref, c_ref):
    yh = _dot(oh_ref[...], whu_ref[...])
    yg = _dot(og_ref[...], wgu_ref[...])
    merged = mh_ref[...].astype(f32) * yh + mg_ref[...].astype(f32) * yg
    t = _dot(merged.astype(bf16), wout_ref[...])
    h1 = h_ref[...] + _rms(t, npost_ref[...])
    h1_ref[...] = h1
    c_ref[...] = _rms(h1, npre_ref[...]).astype(c_ref.dtype)


def _postmix(oh, og, zm, h, whu, wgu, wout, npost, npre, layer):
    M, D = h.shape
    HW, VW = oh.shape[1], og.shape[1]
    tm = _row_tile(M, 320)
    lay = lambda m: (layer, 0, 0)
    return pl.pallas_call(
        _postmix_kernel,
        grid=(M // tm,),
        in_specs=[pl.BlockSpec((tm, HW), lambda m: (m, 0)),
                  pl.BlockSpec((tm, VW), lambda m: (m, 0)),
                  pl.BlockSpec((tm, D), lambda m: (m, 0)),
                  pl.BlockSpec((tm, D), lambda m: (m, 1)),
                  pl.BlockSpec((tm, D), lambda m: (m, 0)),
                  _const_spec((None, HW, D), lay),
                  _const_spec((None, VW, D), lay),
                  _const_spec((None, D, D), lay),
                  pl.BlockSpec((None, 1, D), lay),
                  pl.BlockSpec((None, 1, D), lay)],
        out_specs=[pl.BlockSpec((tm, D), lambda m: (m, 0)),
                   pl.BlockSpec((tm, D), lambda m: (m, 0))],
        out_shape=[jax.ShapeDtypeStruct((M, D), f32),
                   jax.ShapeDtypeStruct((M, D), bf16)],
        compiler_params=_params(("parallel",)),
        name="postmix",
    )(oh, og, zm, zm, h, whu, wgu, wout, npost, npre)


def _ple_kernel(t_ref, h_ref, p_ref, wg_ref, wp_ref, npost_ref, nnext_ref, h3_ref, a_ref):
    h = h_ref[...] + _rms(t_ref[...], npost_ref[...])
    gate = _sigmoid(_dot(h.astype(bf16), wg_ref[...]))
    pe = _dot(p_ref[...].astype(bf16), wp_ref[...])
    h3 = h + gate * pe
    h3_ref[...] = h3
    a_ref[...] = _rms(h3, nnext_ref[...]).astype(a_ref.dtype)


def _ple(t2, h1, p, wg, wp, npost, nnext, layer, next_layer):
    M, D = h1.shape
    P = p.shape[-1]
    tm = _row_tile(M, 512)
    lay = lambda m: (layer, 0, 0)
    return pl.pallas_call(
        _ple_kernel,
        grid=(M // tm,),
        in_specs=[pl.BlockSpec((tm, D), lambda m: (m, 0)),
                  pl.BlockSpec((tm, D), lambda m: (m, 0)),
                  pl.BlockSpec((None, tm, P), lambda m: (layer, m, 0)),
                  _const_spec((None, D, D), lay),
                  _const_spec((None, P, D), lay),
                  pl.BlockSpec((None, 1, D), lay),
                  pl.BlockSpec((None, 1, D), lambda m: (next_layer, 0, 0))],
        out_specs=[pl.BlockSpec((tm, D), lambda m: (m, 0)),
                   pl.BlockSpec((tm, D), lambda m: (m, 0))],
        out_shape=[jax.ShapeDtypeStruct((M, D), f32),
                   jax.ShapeDtypeStruct((M, D), bf16)],
        compiler_params=_params(("parallel",)),
        name="ple",
    )(t2, h1, p, wg, wp, npost, nnext)


def kernel(x_prompt, x_sample, p_prompt, p_sample, state_hgrn, state_gla, norm_pre_mix,
           norm_post_mix, norm_pre_ffn, norm_post_ffn, w_in, lb_param, hgrn_norm, w_hgrn_up,
           w_gla_gate, b_gla_gate, gla_norm, w_gla_up, w_out, w_ff1, w_ff2, w_ple, w_ple_gate):
    B, L, D = x_prompt.shape
    Bs = x_sample.shape[0]
    depth = w_in.shape[0]
    _, _, HH, HK, HV = state_hgrn.shape
    _, _, GH, GK, GV = state_gla.shape
    HW, GKW, GVW = HH * HK, GH * GK, GH * GV
    R = w_gla_gate.shape[1]
    F = w_ff1.shape[2]
    assert HK == LANES and GK == LANES and HV % LANES == 0 and GV % LANES == 0
    assert HH * HV == HW and 2 * GKW == HW and GVW == HW and R <= LANES
    assert x_sample.shape[1] == 1 and L % CHUNK == 0 and D % HW == 0
    mixer_cols = 4 * HW + 2 * GKW + 2 * GVW

    assert w_in.shape[2] - mixer_cols >= LANES and (mixer_cols + R) % 8 == 0
    w_in_t = jnp.swapaxes(w_in, 1, 2)
    whu = w_hgrn_up.astype(bf16)
    wgu = w_gla_up.astype(bf16)
    wout = w_out.astype(bf16)
    wpg = w_ple_gate.astype(bf16)
    wp = w_ple.astype(bf16)
    wgg = jnp.concatenate([w_gla_gate.astype(f32), jnp.zeros((depth, LANES - R, GKW), f32)], axis=1)

    lbs = jnp.cumsum(jax.nn.softmax(lb_param.astype(f32), axis=0), axis=0)
    lbs = (lbs - lbs[0:1]).reshape(depth, 1, HW)
    r3 = lambda t: t.astype(f32).reshape(depth, 1, t.shape[-1])
    n_pre_mix, n_post_mix, n_pre_ffn, n_post_ffn = map(
        r3, (norm_pre_mix, norm_post_mix, norm_pre_ffn, norm_post_ffn))
    hn, gn, bgg = r3(hgrn_norm), r3(gla_norm), r3(b_gla_gate)
    mall, lvl = _level_tables()
    seg_w = (HW, HW, HW, GKW, GKW, GVW, GVW)
    silu_flag = jnp.asarray(np.concatenate(
        [np.full((1, w), v, np.float32) for w, v in zip(seg_w, (1, 0, 1, 0, 0, 0, 1))], axis=1))
    col_scale = jnp.asarray(np.concatenate(
        [np.full((1, w), v, np.float32) for w, v in zip(seg_w, (1, 1, 1, GK ** -0.5, 1, 1, 1))],
        axis=1))
    tm_mg = _pick(D, (1024, 512, 256, 128))
    tf = _pick(F, (1024, 512, 256, 128))
    tn2 = _pick(D, (512, 256, 128))

    def in_projections(a):
        col_spec = pl.BlockSpec((1, HW), lambda n, m: (0, n))
        zb, = _proj(a, w_in_t, i, lambda n: (n + jnp.minimum(n, 1)) * HW, 6, HW, _ep_main,
                    (silu_flag, col_scale), (col_spec, col_spec), (bf16,), "in_proj", True)
        ld, kh = _proj(a, w_in_t, i, lambda n: HW, 1, HW, _ep_forget, (lbs,),
                       (pl.BlockSpec((None, 1, HW), lambda n, m: (i, 0, 0)),),
                       (f32, f32), "in_proj_forget", True, tm_cap=512)
        zm, = _proj(a, w_in_t, i, lambda n: mixer_cols + R + n * tm_mg, 2 * D // tm_mg, tm_mg,
                    _ep_sigmoid, (), (), (bf16,), "in_proj_merge", True)
        glr, = _proj(a, w_in_t, i, lambda n: mixer_cols, 1, LANES, _ep_copy, (), (), (f32,),
                     "in_proj_lowrank", True)
        return zb, ld, kh, zm, glr

    def dense_tail(h, oh, og, zm, p):
        h1, c = _postmix(oh, og, zm, h, whu, wgu, wout, n_post_mix, n_pre_ffn, i)
        u, = _proj(c, w_ff1, i, lambda n: n, F // tf, tf, _ep_relu2, (), (), (bf16,),
                   "ffn_up", False)
        t2, = _proj(u, w_ff2, i, lambda n: n, D // tn2, tn2, _ep_copy, (), (), (f32,),
                    "ffn_down", False, tm_cap=640, w_single=True)
        return _ple(t2, h1, p, wpg, wp, n_post_ffn, n_pre_mix, i, (i + 1) % depth)

    BL = B * L
    assert BL % 16 == 0 and Bs % 16 == 0
    h = jnp.concatenate([x_prompt.reshape(BL, D), x_sample.reshape(Bs, D)], axis=0)
    p = jnp.concatenate([p_prompt.reshape(depth, BL, -1), p_sample.reshape(depth, Bs, -1)], axis=1)
    a = _norm(h, n_pre_mix, 0)

    hgrn_p, gla_p = [], []
    hgrn_s = gla_s = None
    for i in range(depth):
        zb, ld, kh, zm, glr = in_projections(a)
        oh, sh = _hgrn_prompt(zb, ld, kh, hn, mall, lvl, i, B, L, HH, HK, HV)
        oh, hgrn_s = _hgrn_step(zb, ld, kh, hn, state_hgrn, oh, hgrn_s, i, BL, HH, HK, HV)
        og, sg = _gla_prompt(zb, glr, wgg, bgg, gn, mall, lvl, i, B, L, GH, GK, GV)
        og, gla_s = _gla_step(zb, glr, wgg, bgg, gn, state_gla, og, gla_s, i, BL, GH, GK, GV)
        h, a = dense_tail(h, oh, og, zm, p)
        hgrn_p.append(sh)
        gla_p.append(sg)

    return (h[:BL].reshape(B, L, D), h[BL:].reshape(Bs, 1, D),
            jnp.stack(hgrn_p).astype(state_hgrn.dtype), jnp.stack(gla_p).astype(state_gla.dtype),
            hgrn_s.astype(state_hgrn.dtype), gla_s.astype(state_gla.dtype))
```

```python
import functools

import numpy as np
import jax
import jax.numpy as jnp
from jax import lax
from jax.experimental import pallas as pl
from jax.experimental.pallas import tpu as pltpu

EPS = 1e-6
GLA_GATE_NORM = 16.0
LOG2E = 1.4426950408889634
LANES = 128
CHUNK = 64
N_LEVELS = 6
VMEM_LIMIT = 56 * 1024 * 1024

f32 = jnp.float32
bf16 = jnp.bfloat16


def _sigmoid(x):
    return 1.0 / (1.0 + jnp.exp(-x))


def _silu(x):
    return x * _sigmoid(x)


def _log_sigmoid(x):
    return jnp.minimum(x, 0.0) - jnp.log(1.0 + jnp.exp(-jnp.abs(x)))


def _rms(x, g):
    return x * lax.rsqrt(jnp.mean(x * x, axis=-1, keepdims=True) + EPS) * g


def _dot(a, b):
    return jnp.dot(a, b, preferred_element_type=f32)


def _dot_nt(a, b):
    return lax.dot_general(a, b, (((1,), (1,)), ((), ())), preferred_element_type=f32)


def _dot_tn(a, b):
    return lax.dot_general(a, b, (((0,), (0,)), ((), ())), preferred_element_type=f32)


def _split2(x):
    hi = x.astype(bf16)
    lo = (x - hi.astype(f32)).astype(bf16)
    return hi, lo


def _params(sem):
    return pltpu.CompilerParams(dimension_semantics=sem, vmem_limit_bytes=VMEM_LIMIT)


def _const_spec(shape, index_map):
    return pl.BlockSpec(shape, index_map, pipeline_mode=pl.Buffered(1))


def _pick(n, cands):
    for c in cands:
        if n % c == 0:
            return c
    return n


def _row_tile(m, cap):
    for t in range(min(cap, m), 0, -1):
        if m % t == 0 and t % 16 == 0:
            return t
    return m


def _level_tables():
    C = CHUNK
    t = np.arange(C)[:, None]
    j = np.arange(C)[None, :]
    mats = [(j <= t), (j > t)]
    lvl = np.full((C, C), -1, np.int32)
    lvl[t == j] = 0
    w = C // 2
    level = 1
    while w >= 1:
        start = (t // (2 * w)) * (2 * w)
        m = start + w - 1
        second = (t - start) >= w
        mats.append(np.where(second, (j > m) & (j <= t), (j > t) & (j <= m)))
        same = (t // (2 * w)) == (j // (2 * w))
        lvl[same & second & ((j - (j // (2 * w)) * (2 * w)) < w)] = level
        w //= 2
        level += 1
    mall = np.concatenate(mats, axis=0).astype(np.float32)
    return jnp.asarray(np.concatenate([mall, mall], axis=1), bf16), jnp.asarray(lvl)


def _norm_kernel(x_ref, g_ref, o_ref):
    o_ref[...] = _rms(x_ref[...], g_ref[...]).astype(o_ref.dtype)


def _norm(x, gains, layer):
    M, D = x.shape
    tm = _row_tile(M, 512)
    return pl.pallas_call(
        _norm_kernel,
        grid=(M // tm,),
        in_specs=[pl.BlockSpec((tm, D), lambda m: (m, 0)),
                  pl.BlockSpec((None, 1, D), lambda m: (layer, 0, 0))],
        out_specs=pl.BlockSpec((tm, D), lambda m: (m, 0)),
        out_shape=jax.ShapeDtypeStruct((M, D), bf16),
        compiler_params=_params(("parallel",)),
        name="rmsnorm",
    )(x, gains)


def _proj_kernel(*refs, n_extra, epilogue, transposed, tn, slab):
    a_ref, w_ref = refs[0], refs[1]
    extra = refs[2:2 + n_extra]
    outs = refs[2 + n_extra:-1]
    wbf_ref = refs[-1]

    @pl.when(pl.program_id(1) == 0)
    def _():
        w = w_ref[0] if transposed else w_ref[...]
        wbf_ref[...] = w.astype(bf16)

    a = a_ref[...]
    for j in range(tn // slab):
        cols = slice(j * slab, (j + 1) * slab)
        if transposed:
            acc = _dot_nt(a, wbf_ref[cols, :])
        else:
            acc = _dot(a, wbf_ref[:, cols])
        epilogue(acc, cols, extra, outs)


def _proj(a, w, layer, w_off, n_tiles, tn, epilogue, extra, extra_specs, out_dtypes, name,
          transposed, tm_cap=1024, w_single=False):
    M, K = a.shape
    tm = _row_tile(M, tm_cap)
    if transposed:
        w_shape = (pl.Element(1), pl.Element(tn), pl.Element(K))
        def w_map(n, m):
            off = w_off(n)
            return (layer, off if isinstance(off, int) else pl.multiple_of(off, 8), 0)
        scratch = pltpu.VMEM((tn, K), bf16)
    else:
        w_shape = (None, K, tn)
        w_map = lambda n, m: (layer, 0, w_off(n))
        scratch = pltpu.VMEM((K, tn), bf16)
    w_spec = _const_spec(w_shape, w_map) if w_single else pl.BlockSpec(w_shape, w_map)
    kern = functools.partial(_proj_kernel, n_extra=len(extra), epilogue=epilogue,
                             transposed=transposed, tn=tn, slab=min(tn, 2 * LANES))
    return pl.pallas_call(
        kern,
        grid=(n_tiles, M // tm),
        in_specs=[pl.BlockSpec((tm, K), lambda n, m: (m, 0)), w_spec] + list(extra_specs),
        out_specs=[pl.BlockSpec((tm, tn), lambda n, m: (m, n)) for _ in out_dtypes],
        out_shape=[jax.ShapeDtypeStruct((M, n_tiles * tn), dt) for dt in out_dtypes],
        scratch_shapes=[scratch],
        compiler_params=_params(("parallel", "arbitrary")),
        name=name,
    )(a, w, *extra)


def _hgrn_gates(fl, lb):
    fl2 = fl * LOG2E
    e = jnp.exp2(-jnp.abs(fl2))
    x2 = jnp.log2(1.0 - lb) + (jnp.minimum(fl2, 0.0) - jnp.log2(1.0 + e))
    x1 = jnp.log2(lb)
    log2_f = jnp.maximum(x1, x2) + jnp.log2(1.0 + jnp.exp2(-jnp.abs(x1 - x2)))
    r = 1.0 / (1.0 + e)
    k = (1.0 - lb) * jnp.where(fl >= 0.0, e * r, r)
    return log2_f, k


def _ep_main(acc, cols, extra, outs):
    silu_flag = extra[0][:, cols]
    scale = extra[1][:, cols]
    gate = jnp.where(silu_flag > 0.0, _sigmoid(acc), 1.0)
    outs[0][:, cols] = (acc * scale * gate).astype(outs[0].dtype)


def _ep_forget(acc, cols, extra, outs):
    ld_ref, k_ref = outs
    log2_f, k = _hgrn_gates(acc, extra[0][:, cols])
    ld_ref[:, cols] = log2_f
    k_ref[:, cols] = k


def _ep_sigmoid(acc, cols, extra, outs):
    outs[0][:, cols] = _sigmoid(acc).astype(outs[0].dtype)


def _ep_copy(acc, cols, extra, outs):
    outs[0][:, cols] = acc.astype(outs[0].dtype)


def _ep_relu2(acc, cols, extra, outs):
    outs[0][:, cols] = jnp.square(jnp.maximum(acc, 0.0)).astype(outs[0].dtype)


def _recurrence_chunks(T, H, K, V, q_ref, load_k, ld_ref, v_ref, g_ref, gain,
                       mall_ref, lvl_ref, st_ref, e_ref, o_ref):
    C = CHUNK

    def chunk(c, carry):
        r0 = pl.multiple_of(c * C, C)
        rows = pl.ds(r0, C)
        lvl = lvl_ref[...]
        hi, lo = _split2(ld_ref[rows, :])
        e_ref[...] = jnp.exp2(_dot(mall_ref[...], jnp.concatenate([hi, lo], axis=0)))
        for h in range(H):
            cs = slice(h * K, (h + 1) * K)
            vs = slice(h * V, (h + 1) * V)
            qc = q_ref[rows, cs].astype(f32)
            kc = load_k(rows, cs)
            vb = v_ref[rows, vs]
            st = st_ref[h]

            o = _dot_nt((qc * e_ref[0:C, cs]).astype(bf16), st.astype(bf16))

            e1 = e_ref[(N_LEVELS + 1) * C:(N_LEVELS + 2) * C, cs]
            diag = jnp.sum(qc * kc, axis=-1, keepdims=True)
            adj = jnp.sum(qc * pltpu.roll(kc, 1, axis=0) * e1, axis=-1, keepdims=True)
            a = jnp.where(lvl == 0, diag, jnp.where(lvl == N_LEVELS, adj, 0.0))
            for l in range(1, N_LEVELS):
                el = e_ref[(l + 1) * C:(l + 2) * C, cs]
                al = _dot_nt((qc * el).astype(bf16), (kc * el).astype(bf16))
                a = jnp.where(lvl == l, al, a)
            o = o + _dot(a.astype(bf16), vb)

            kd = (kc * e_ref[C:2 * C, cs]).astype(bf16)
            st_ref[h] = st * e_ref[C - 1:C, cs] + _dot_tn(vb, kd)

            gate = g_ref[rows, vs].astype(f32)
            o_ref[rows, vs] = (_rms(o, gain) * gate).astype(o_ref.dtype)
        return carry

    lax.fori_loop(0, T // C, chunk, 0)


def _write_state(tb, st_ref, s_ref, H):
    @pl.when(tb == pl.num_programs(1) - 1)
    def _():
        for h in range(H):
            s_ref[0, h] = st_ref[h].T


def _hgrn_prompt_kernel(zq_ref, zi_ref, zg_ref, ld_ref, k_ref, gain_ref, mall_ref,
                        lvl_ref, o_ref, s_ref, st_ref, e_ref, *, T, H, K, V):
    tb = pl.program_id(1)

    @pl.when(tb == 0)
    def _():
        st_ref[...] = jnp.zeros_like(st_ref)

    _recurrence_chunks(T, H, K, V, zq_ref, lambda rows, cs: k_ref[rows, cs], ld_ref,
                       zi_ref, zg_ref, gain_ref[...], mall_ref, lvl_ref, st_ref, e_ref, o_ref)
    _write_state(tb, st_ref, s_ref, H)


def _gla_log2_decay(glr, wg, bg):
    g_hi, g_lo = _split2(glr)
    w_hi, w_lo = _split2(wg)
    pre = _dot(g_hi, w_hi) + _dot(g_hi, w_lo) + _dot(g_lo, w_hi) + bg
    return _log_sigmoid(pre) * (LOG2E / GLA_GATE_NORM)


def _gla_prompt_kernel(gqk_ref, gv_ref, gr_ref, glr_ref, wg_ref, bg_ref, gain_ref, mall_ref,
                       lvl_ref, o_ref, s_ref, st_ref, e_ref, ld_ref, *, T, H, K, V):
    tb = pl.program_id(1)

    @pl.when(tb == 0)
    def _():
        st_ref[...] = jnp.zeros_like(st_ref)

    ld_ref[...] = _gla_log2_decay(glr_ref[...], wg_ref[...], bg_ref[...])
    KW = H * K
    k_ref = gqk_ref.at[:, KW:2 * KW]
    _recurrence_chunks(T, H, K, V, gqk_ref, lambda rows, cs: k_ref[rows, cs].astype(f32),
                       ld_ref, gv_ref, gr_ref, gain_ref[...], mall_ref, lvl_ref,
                       st_ref, e_ref, o_ref)
    _write_state(tb, st_ref, s_ref, H)


def _hgrn_prompt(zb, ld, kh, hgrn_norm, mall, lvl, layer, B, L, H, K, V):
    HW = H * K
    T = min(512, L)
    nT = L // T
    tok = lambda col: (lambda b, t: (b * nT + t, col))
    cst = lambda b, t: (0, 0)
    kern = functools.partial(_hgrn_prompt_kernel, T=T, H=H, K=K, V=V)
    return pl.pallas_call(
        kern,
        grid=(B, nT),
        in_specs=[pl.BlockSpec((T, HW), tok(0)),
                  pl.BlockSpec((T, HW), tok(1)),
                  pl.BlockSpec((T, HW), tok(2)),
                  pl.BlockSpec((T, HW), tok(0)),
                  pl.BlockSpec((T, HW), tok(0)),
                  pl.BlockSpec((None, 1, V), lambda b, t: (layer, 0, 0)),
                  pl.BlockSpec(mall.shape, cst),
                  pl.BlockSpec(lvl.shape, cst)],
        out_specs=[pl.BlockSpec((T, HW), tok(0)),
                   pl.BlockSpec((1, H, K, V), lambda b, t: (b, 0, 0, 0))],
        out_shape=[jax.ShapeDtypeStruct((zb.shape[0], H * V), bf16),
                   jax.ShapeDtypeStruct((B, H, K, V), f32)],
        scratch_shapes=[pltpu.VMEM((H, V, K), f32),
                        pltpu.VMEM((mall.shape[0], HW), f32)],
        compiler_params=_params(("parallel", "arbitrary")),
        name="hgrn_prompt",
    )(zb, zb, zb, ld, kh, hgrn_norm, mall, lvl)


def _gla_prompt(zb, glr, wgg, bgg, gla_norm, mall, lvl, layer, B, L, H, K, V):
    KW, VW = H * K, H * V
    T = min(512, L)
    nT = L // T
    tok = lambda col: (lambda b, t: (b * nT + t, col))
    cst = lambda b, t: (0, 0)
    kern = functools.partial(_gla_prompt_kernel, T=T, H=H, K=K, V=V)
    return pl.pallas_call(
        kern,
        grid=(B, nT),
        in_specs=[pl.BlockSpec((T, 2 * KW), tok(3)),
                  pl.BlockSpec((T, VW), tok(4)),
                  pl.BlockSpec((T, VW), tok(5)),
                  pl.BlockSpec((T, LANES), tok(0)),
                  pl.BlockSpec((None, LANES, KW), lambda b, t: (layer, 0, 0)),
                  pl.BlockSpec((None, 1, KW), lambda b, t: (layer, 0, 0)),
                  pl.BlockSpec((None, 1, V), lambda b, t: (layer, 0, 0)),
                  pl.BlockSpec(mall.shape, cst),
                  pl.BlockSpec(lvl.shape, cst)],
        out_specs=[pl.BlockSpec((T, VW), tok(0)),
                   pl.BlockSpec((1, H, K, V), lambda b, t: (b, 0, 0, 0))],
        out_shape=[jax.ShapeDtypeStruct((zb.shape[0], VW), bf16),
                   jax.ShapeDtypeStruct((B, H, K, V), f32)],
        scratch_shapes=[pltpu.VMEM((H, V, K), f32),
                        pltpu.VMEM((mall.shape[0], KW), f32),
                        pltpu.VMEM((T, KW), f32)],
        compiler_params=_params(("parallel", "arbitrary")),
        name="gla_prompt",
    )(zb, zb, zb, glr, wgg, bgg, gla_norm, mall, lvl)


def _step_rows(Bt, H, K, V, a_ref, k_ref, qa_ref, v_ref, s_in_ref, s_out_ref, oi_ref):
    r = lax.broadcasted_iota(jnp.int32, (K, K), 0)
    c = lax.broadcasted_iota(jnp.int32, (K, K), 1)
    eye = r == c
    ones = jnp.ones((2 * K, LANES), bf16)
    r2 = lax.broadcasted_iota(jnp.int32, (2 * K, 2 * LANES), 0)
    c2 = lax.broadcasted_iota(jnp.int32, (2 * K, 2 * LANES), 1)
    ones_pair = jnp.where((r2 < K) == (c2 < LANES), 1.0, 0.0).astype(bf16)
    nv = V // LANES

    def diag(x):
        return jnp.where(eye, jnp.broadcast_to(x, (K, K)), 0.0)

    def body(b, carry):
        for h in range(H):
            cs = slice(h * K, (h + 1) * K)
            a = a_ref[b, :, cs]
            a_hi = a.astype(bf16).astype(f32)
            lhs_a = jnp.concatenate([diag(a_hi), diag(a - a_hi)], axis=1).astype(bf16)
            lhs_kq = jnp.concatenate([diag(k_ref[b, :, cs]), diag(qa_ref[b, :, cs])],
                                     axis=1).astype(bf16)
            a_col = _dot(lhs_a, ones)
            kq = _dot(lhs_kq, ones_pair)
            k_col, q_col = kq[:, :LANES], kq[:, LANES:]
            for j in range(nv):
                ls = slice(h * V + j * LANES, h * V + (j + 1) * LANES)
                vj = slice(j * LANES, (j + 1) * LANES)
                s = s_in_ref[b, h, :, vj]
                s_out_ref[b, h, :, vj] = a_col * s + k_col * v_ref[b, :, ls]
                oi_ref[b, :, ls] = jnp.sum(q_col * s, axis=0, keepdims=True)
        return carry

    lax.fori_loop(0, Bt, body, 0)


def _store_rows(ref, x):
    for b in range(x.shape[0]):
        ref[b] = x[b:b + 1, :]


def _step_finish(H, K, V, q, k, v, g, gain, oi_ref, o_ref):
    ones = jnp.ones((K, LANES), bf16)
    nv = V // LANES
    for h in range(H):
        cs = slice(h * K, (h + 1) * K)
        vs = slice(h * V, (h + 1) * V)
        qk = _dot((q[:, cs] * k[:, cs]).astype(bf16), ones)
        if nv > 1:
            qk = jnp.concatenate([qk] * nv, axis=1)
        oi = jnp.concatenate([oi_ref[b, :, vs] for b in range(q.shape[0])], axis=0)
        o = qk * v[:, vs] + oi
        o_ref[:, vs] = (_rms(o, gain) * g[:, vs]).astype(o_ref.dtype)


def _step_common(Bt, H, K, V, a, q, k, v, g, gain, a_ref, k_ref, qa_ref, v_ref, oi_ref,
                 s_in_ref, s_out_ref, o_ref):
    _store_rows(a_ref, a)
    _store_rows(k_ref, k)
    _store_rows(qa_ref, q * a)
    _store_rows(v_ref, v)
    _step_rows(Bt, H, K, V, a_ref, k_ref, qa_ref, v_ref, s_in_ref, s_out_ref, oi_ref)
    _step_finish(H, K, V, q, k, v, g, gain, oi_ref, o_ref)


def _hgrn_step_kernel(*refs, Bt, H, K, V, aliased):
    (zq_ref, zi_ref, zg_ref, ld_ref, kh_ref, gain_ref, s_in_ref) = refs[:7]
    o_ref, s_out_ref, a_ref, k_ref, qa_ref, v_ref, oi_ref = refs[7 + aliased:]
    a = jnp.exp2(ld_ref[...])
    _step_common(Bt, H, K, V, a, zq_ref[...].astype(f32), kh_ref[...], zi_ref[...].astype(f32),
                 zg_ref[...].astype(f32), gain_ref[...], a_ref, k_ref, qa_ref, v_ref, oi_ref,
                 s_in_ref, s_out_ref, o_ref)


def _gla_step_kernel(*refs, Bt, H, K, V, aliased):
    (gqk_ref, gv_ref, gr_ref, glr_ref, wg_ref, bg_ref, gain_ref, s_in_ref) = refs[:8]
    o_ref, s_out_ref, a_ref, k_ref, qa_ref, v_ref, oi_ref = refs[8 + aliased:]
    KW = H * K
    a = jnp.exp2(_gla_log2_decay(glr_ref[...], wg_ref[...], bg_ref[...]))
    _step_common(Bt, H, K, V, a, gqk_ref[:, :KW].astype(f32), gqk_ref[:, KW:].astype(f32),
                 gv_ref[...].astype(f32), gr_ref[...].astype(f32), gain_ref[...],
                 a_ref, k_ref, qa_ref, v_ref, oi_ref, s_in_ref, s_out_ref, o_ref)


def _step_call(kern, name, ins, in_specs, state, o_prev, s_prev, layer, row0, Bt, H, K, V, KW):
    depth, Bs = state.shape[:2]
    VW = H * V
    blk0 = row0 // Bt
    in_specs = list(in_specs) + [pl.BlockSpec((None, Bt, H, K, V), lambda i: (layer, i, 0, 0, 0)),
                                 pl.BlockSpec(memory_space=pl.ANY)]
    args = list(ins) + [state, o_prev]
    aliases = {len(args) - 1: 0}
    if s_prev is not None:
        in_specs.append(pl.BlockSpec(memory_space=pl.ANY))
        args.append(s_prev)
        aliases[len(args) - 1] = 1
    return pl.pallas_call(
        functools.partial(kern, Bt=Bt, H=H, K=K, V=V, aliased=len(aliases)),
        grid=(Bs // Bt,),
        in_specs=in_specs,
        out_specs=[pl.BlockSpec((Bt, VW), lambda i: (blk0 + i, 0)),
                   pl.BlockSpec((None, Bt, H, K, V), lambda i: (layer, i, 0, 0, 0))],
        out_shape=[jax.ShapeDtypeStruct(o_prev.shape, bf16),
                   jax.ShapeDtypeStruct((depth, Bs, H, K, V), f32)],
        scratch_shapes=[pltpu.VMEM((Bt, 1, KW), f32)] * 3 + [pltpu.VMEM((Bt, 1, VW), f32)] * 2,
        input_output_aliases=aliases,
        compiler_params=_params(("arbitrary",)),
        name=name,
    )(*args)


def _hgrn_step(zb, ld, kh, hgrn_norm, state, o_prev, s_prev, layer, row0, H, K, V):
    HW = H * K
    Bt = min(16, state.shape[1])
    blk0 = row0 // Bt
    row = lambda col: (lambda i: (blk0 + i, col))
    in_specs = [pl.BlockSpec((Bt, HW), row(0)),
                pl.BlockSpec((Bt, HW), row(1)),
                pl.BlockSpec((Bt, HW), row(2)),
                pl.BlockSpec((Bt, HW), row(0)),
                pl.BlockSpec((Bt, HW), row(0)),
                pl.BlockSpec((None, 1, V), lambda i: (layer, 0, 0))]
    return _step_call(_hgrn_step_kernel, "hgrn_step", (zb, zb, zb, ld, kh, hgrn_norm),
                      in_specs, state, o_prev, s_prev, layer, row0, Bt, H, K, V, HW)


def _gla_step(zb, glr, wgg, bgg, gla_norm, state, o_prev, s_prev, layer, row0, H, K, V):
    KW, VW = H * K, H * V
    Bt = min(16, state.shape[1])
    blk0 = row0 // Bt
    row = lambda col: (lambda i: (blk0 + i, col))
    in_specs = [pl.BlockSpec((Bt, 2 * KW), row(3)),
                pl.BlockSpec((Bt, VW), row(4)),
                pl.BlockSpec((Bt, VW), row(5)),
                pl.BlockSpec((Bt, LANES), row(0)),
                pl.BlockSpec((None, LANES, KW), lambda i: (layer, 0, 0)),
                pl.BlockSpec((None, 1, KW), lambda i: (layer, 0, 0)),
                pl.BlockSpec((None, 1, V), lambda i: (layer, 0, 0))]
    return _step_call(_gla_step_kernel, "gla_step", (zb, zb, zb, glr, wgg, bgg, gla_norm),
                      in_specs, state, o_prev, s_prev, layer, row0, Bt, H, K, V, KW)


def _postmix_kernel(oh_ref, og_ref, mh_ref, mg_ref, h_ref, whu_ref, wgu_ref, wout_ref,
                    npost_ref, npre_ref, h1_ref, c_ref):
    yh = _dot(oh_ref[...], whu_ref[...])
    yg = _dot(og_ref[...], wgu_ref[...])
    merged = mh_ref[...].astype(f32) * yh + mg_ref[...].astype(f32) * yg
    t = _dot(merged.astype(bf16), wout_ref[...])
    h1 = h_ref[...] + _rms(t, npost_ref[...])
    h1_ref[...] = h1
    c_ref[...] = _rms(h1, npre_ref[...]).astype(c_ref.dtype)


def _postmix(oh, og, zm, h, whu, wgu, wout, npost, npre, layer):
    M, D = h.shape
    HW, VW = oh.shape[1], og.shape[1]
    tm = _row_tile(M, 320)
    lay = lambda m: (layer, 0, 0)
    return pl.pallas_call(
        _postmix_kernel,
        grid=(M // tm,),
        in_specs=[pl.BlockSpec((tm, HW), lambda m: (m, 0)),
                  pl.BlockSpec((tm, VW), lambda m: (m, 0)),
                  pl.BlockSpec((tm, D), lambda m: (m, 0)),
                  pl.BlockSpec((tm, D), lambda m: (m, 1)),
                  pl.BlockSpec((tm, D), lambda m: (m, 0)),
                  _const_spec((None, HW, D), lay),
                  _const_spec((None, VW, D), lay),
                  _const_spec((None, D, D), lay),
                  pl.BlockSpec((None, 1, D), lay),
                  pl.BlockSpec((None, 1, D), lay)],
        out_specs=[pl.BlockSpec((tm, D), lambda m: (m, 0)),
                   pl.BlockSpec((tm, D), lambda m: (m, 0))],
        out_shape=[jax.ShapeDtypeStruct((M, D), f32),
                   jax.ShapeDtypeStruct((M, D), bf16)],
        compiler_params=_params(("parallel",)),
        name="postmix",
    )(oh, og, zm, zm, h, whu, wgu, wout, npost, npre)


def _ple_kernel(t_ref, h_ref, p_ref, wg_ref, wp_ref, npost_ref, nnext_ref, h3_ref, a_ref):
    h = h_ref[...] + _rms(t_ref[...], npost_ref[...])
    gate = _sigmoid(_dot(h.astype(bf16), wg_ref[...]))
    pe = _dot(p_ref[...].astype(bf16), wp_ref[...])
    h3 = h + gate * pe
    h3_ref[...] = h3
    a_ref[...] = _rms(h3, nnext_ref[...]).astype(a_ref.dtype)


def _ple(t2, h1, p, wg, wp, npost, nnext, layer, next_layer):
    M, D = h1.shape
    P = p.shape[-1]
    tm = _row_tile(M, 512)
    lay = lambda m: (layer, 0, 0)
    return pl.pallas_call(
        _ple_kernel,
        grid=(M // tm,),
        in_specs=[pl.BlockSpec((tm, D), lambda m: (m, 0)),
                  pl.BlockSpec((tm, D), lambda m: (m, 0)),
                  pl.BlockSpec((None, tm, P), lambda m: (layer, m, 0)),
                  _const_spec((None, D, D), lay),
                  _const_spec((None, P, D), lay),
                  pl.BlockSpec((None, 1, D), lay),
                  pl.BlockSpec((None, 1, D), lambda m: (next_layer, 0, 0))],
        out_specs=[pl.BlockSpec((tm, D), lambda m: (m, 0)),
                   pl.BlockSpec((tm, D), lambda m: (m, 0))],
        out_shape=[jax.ShapeDtypeStruct((M, D), f32),
                   jax.ShapeDtypeStruct((M, D), bf16)],
        compiler_params=_params(("parallel",)),
        name="ple",
    )(t2, h1, p, wg, wp, npost, nnext)


def kernel(x_prompt, x_sample, p_prompt, p_sample, state_hgrn, state_gla, norm_pre_mix,
           norm_post_mix, norm_pre_ffn, norm_post_ffn, w_in, lb_param, hgrn_norm, w_hgrn_up,
           w_gla_gate, b_gla_gate, gla_norm, w_gla_up, w_out, w_ff1, w_ff2, w_ple, w_ple_gate):
    B, L, D = x_prompt.shape
    Bs = x_sample.shape[0]
    depth = w_in.shape[0]
    _, _, HH, HK, HV = state_hgrn.shape
    _, _, GH, GK, GV = state_gla.shape
    HW, GKW, GVW = HH * HK, GH * GK, GH * GV
    R = w_gla_gate.shape[1]
    F = w_ff1.shape[2]
    assert HK == LANES and GK == LANES and HV % LANES == 0 and GV % LANES == 0
    assert HH * HV == HW and 2 * GKW == HW and GVW == HW and R <= LANES
    assert x_sample.shape[1] == 1 and L % CHUNK == 0 and D % HW == 0
    mixer_cols = 4 * HW + 2 * GKW + 2 * GVW

    assert w_in.shape[2] - mixer_cols >= LANES and (mixer_cols + R) % 8 == 0
    w_in_t = jnp.swapaxes(w_in, 1, 2)
    whu = w_hgrn_up.astype(bf16)
    wgu = w_gla_up.astype(bf16)
    wout = w_out.astype(bf16)
    wpg = w_ple_gate.astype(bf16)
    wp = w_ple.astype(bf16)
    wgg = jnp.concatenate([w_gla_gate.astype(f32), jnp.zeros((depth, LANES - R, GKW), f32)], axis=1)

    lbs = jnp.cumsum(jax.nn.softmax(lb_param.astype(f32), axis=0), axis=0)
    lbs = (lbs - lbs[0:1]).reshape(depth, 1, HW)
    r3 = lambda t: t.astype(f32).reshape(depth, 1, t.shape[-1])
    n_pre_mix, n_post_mix, n_pre_ffn, n_post_ffn = map(
        r3, (norm_pre_mix, norm_post_mix, norm_pre_ffn, norm_post_ffn))
    hn, gn, bgg = r3(hgrn_norm), r3(gla_norm), r3(b_gla_gate)
    mall, lvl = _level_tables()
    seg_w = (HW, HW, HW, GKW, GKW, GVW, GVW)
    silu_flag = jnp.asarray(np.concatenate(
        [np.full((1, w), v, np.float32) for w, v in zip(seg_w, (1, 0, 1, 0, 0, 0, 1))], axis=1))
    col_scale = jnp.asarray(np.concatenate(
        [np.full((1, w), v, np.float32) for w, v in zip(seg_w, (1, 1, 1, GK ** -0.5, 1, 1, 1))],
        axis=1))
    tm_mg = _pick(D, (1024, 512, 256, 128))
    tf = _pick(F, (1024, 512, 256, 128))
    tn2 = _pick(D, (512, 256, 128))

    def in_projections(a):
        col_spec = pl.BlockSpec((1, HW), lambda n, m: (0, n))
        zb, = _proj(a, w_in_t, i, lambda n: (n + jnp.minimum(n, 1)) * HW, 6, HW, _ep_main,
                    (silu_flag, col_scale), (col_spec, col_spec), (bf16,), "in_proj", True,
                    tm_cap=2048)
        ld, kh = _proj(a, w_in_t, i, lambda n: HW, 1, HW, _ep_forget, (lbs,),
                       (pl.BlockSpec((None, 1, HW), lambda n, m: (i, 0, 0)),),
                       (f32, f32), "in_proj_forget", True, tm_cap=512)
        zm, = _proj(a, w_in_t, i, lambda n: mixer_cols + R + n * tm_mg, 2 * D // tm_mg, tm_mg,
                    _ep_sigmoid, (), (), (bf16,), "in_proj_merge", True, tm_cap=2048)
        glr, = _proj(a, w_in_t, i, lambda n: mixer_cols, 1, LANES, _ep_copy, (), (), (f32,),
                     "in_proj_lowrank", True)
        return zb, ld, kh, zm, glr

    def dense_tail(h, oh, og, zm, p):
        h1, c = _postmix(oh, og, zm, h, whu, wgu, wout, n_post_mix, n_pre_ffn, i)
        u, = _proj(c, w_ff1, i, lambda n: n, F // tf, tf, _ep_relu2, (), (), (bf16,),
                   "ffn_up", False, tm_cap=2048)
        t2, = _proj(u, w_ff2, i, lambda n: n, D // tn2, tn2, _ep_copy, (), (), (f32,),
                    "ffn_down", False, tm_cap=640, w_single=True)
        return _ple(t2, h1, p, wpg, wp, n_post_ffn, n_pre_mix, i, (i + 1) % depth)

    BL = B * L
    assert BL % 16 == 0 and Bs % 16 == 0
    h = jnp.concatenate([x_prompt.reshape(BL, D), x_sample.reshape(Bs, D)], axis=0)
    p = jnp.concatenate([p_prompt.reshape(depth, BL, -1), p_sample.reshape(depth, Bs, -1)], axis=1)
    a = _norm(h, n_pre_mix, 0)

    hgrn_p, gla_p = [], []
    hgrn_s = gla_s = None
    for i in range(depth):
        zb, ld, kh, zm, glr = in_projections(a)
        oh, sh = _hgrn_prompt(zb, ld, kh, hn, mall, lvl, i, B, L, HH, HK, HV)
        oh, hgrn_s = _hgrn_step(zb, ld, kh, hn, state_hgrn, oh, hgrn_s, i, BL, HH, HK, HV)
        og, sg = _gla_prompt(zb, glr, wgg, bgg, gn, mall, lvl, i, B, L, GH, GK, GV)
        og, gla_s = _gla_step(zb, glr, wgg, bgg, gn, state_gla, og, gla_s, i, BL, GH, GK, GV)
        h, a = dense_tail(h, oh, og, zm, p)
        hgrn_p.append(sh)
        gla_p.append(sg)

    return (h[:BL].reshape(B, L, D), h[BL:].reshape(Bs, 1, D),
            jnp.stack(hgrn_p).astype(state_hgrn.dtype), jnp.stack(gla_p).astype(state_gla.dtype),
            hgrn_s.astype(state_hgrn.dtype), gla_s.astype(state_gla.dtype))
```

```python
import functools

import numpy as np
import jax
import jax.numpy as jnp
from jax import lax
from jax.experimental import pallas as pl
from jax.experimental.pallas import tpu as pltpu

EPS = 1e-6
GLA_GATE_NORM = 16.0
LOG2E = 1.4426950408889634
LANES = 128
CHUNK = 64
N_LEVELS = 6
VMEM_LIMIT = 56 * 1024 * 1024

f32 = jnp.float32
bf16 = jnp.bfloat16


def _sigmoid(x):
    return 1.0 / (1.0 + jnp.exp(-x))


def _silu(x):
    return x * _sigmoid(x)


def _log_sigmoid(x):
    return jnp.minimum(x, 0.0) - jnp.log(1.0 + jnp.exp(-jnp.abs(x)))


def _rms(x, g):
    return x * lax.rsqrt(jnp.mean(x * x, axis=-1, keepdims=True) + EPS) * g


def _dot(a, b):
    return jnp.dot(a, b, preferred_element_type=f32)


def _dot_nt(a, b):
    return lax.dot_general(a, b, (((1,), (1,)), ((), ())), preferred_element_type=f32)


def _dot_tn(a, b):
    return lax.dot_general(a, b, (((0,), (0,)), ((), ())), preferred_element_type=f32)


def _split2(x):
    hi = x.astype(bf16)
    lo = (x - hi.astype(f32)).astype(bf16)
    return hi, lo


def _params(sem):
    return pltpu.CompilerParams(dimension_semantics=sem, vmem_limit_bytes=VMEM_LIMIT)


def _const_spec(shape, index_map):
    return pl.BlockSpec(shape, index_map, pipeline_mode=pl.Buffered(1))


def _pick(n, cands):
    for c in cands:
        if n % c == 0:
            return c
    return n


def _row_tile(m, cap):
    for t in range(min(cap, m), 0, -1):
        if m % t == 0 and t % 16 == 0:
            return t
    return m


def _level_tables():
    C = CHUNK
    t = np.arange(C)[:, None]
    j = np.arange(C)[None, :]
    mats = [(j <= t), (j > t)]
    lvl = np.full((C, C), -1, np.int32)
    lvl[t == j] = 0
    w = C // 2
    level = 1
    while w >= 1:
        start = (t // (2 * w)) * (2 * w)
        m = start + w - 1
        second = (t - start) >= w
        mats.append(np.where(second, (j > m) & (j <= t), (j > t) & (j <= m)))
        same = (t // (2 * w)) == (j // (2 * w))
        lvl[same & second & ((j - (j // (2 * w)) * (2 * w)) < w)] = level
        w //= 2
        level += 1
    mall = np.concatenate(mats, axis=0).astype(np.float32)
    return jnp.asarray(np.concatenate([mall, mall], axis=1), bf16), jnp.asarray(lvl)


def _norm_kernel(x_ref, g_ref, o_ref):
    o_ref[...] = _rms(x_ref[...], g_ref[...]).astype(o_ref.dtype)


def _norm(x, gains, layer):
    M, D = x.shape
    tm = _row_tile(M, 512)
    return pl.pallas_call(
        _norm_kernel,
        grid=(M // tm,),
        in_specs=[pl.BlockSpec((tm, D), lambda m: (m, 0)),
                  pl.BlockSpec((None, 1, D), lambda m: (layer, 0, 0))],
        out_specs=pl.BlockSpec((tm, D), lambda m: (m, 0)),
        out_shape=jax.ShapeDtypeStruct((M, D), bf16),
        compiler_params=_params(("parallel",)),
        name="rmsnorm",
    )(x, gains)


def _proj_kernel(*refs, n_extra, epilogue, transposed, tn, slab):
    a_ref, w_ref = refs[0], refs[1]
    extra = refs[2:2 + n_extra]
    outs = refs[2 + n_extra:-1]
    wbf_ref = refs[-1]

    @pl.when(pl.program_id(1) == 0)
    def _():
        w = w_ref[0] if transposed else w_ref[...]
        wbf_ref[...] = w.astype(bf16)

    a = a_ref[...]
    for j in range(tn // slab):
        cols = slice(j * slab, (j + 1) * slab)
        if transposed:
            acc = _dot_nt(a, wbf_ref[cols, :])
        else:
            acc = _dot(a, wbf_ref[:, cols])
        epilogue(acc, cols, extra, outs)


def _proj(a, w, layer, w_off, n_tiles, tn, epilogue, extra, extra_specs, out_dtypes, name,
          transposed, tm_cap=1024, w_single=False):
    M, K = a.shape
    tm = _row_tile(M, tm_cap)
    if transposed:
        w_shape = (pl.Element(1), pl.Element(tn), pl.Element(K))
        def w_map(n, m):
            off = w_off(n)
            return (layer, off if isinstance(off, int) else pl.multiple_of(off, 8), 0)
        scratch = pltpu.VMEM((tn, K), bf16)
    else:
        w_shape = (None, K, tn)
        w_map = lambda n, m: (layer, 0, w_off(n))
        scratch = pltpu.VMEM((K, tn), bf16)
    w_spec = _const_spec(w_shape, w_map) if w_single else pl.BlockSpec(w_shape, w_map)
    kern = functools.partial(_proj_kernel, n_extra=len(extra), epilogue=epilogue,
                             transposed=transposed, tn=tn, slab=min(tn, 2 * LANES))
    return pl.pallas_call(
        kern,
        grid=(n_tiles, M // tm),
        in_specs=[pl.BlockSpec((tm, K), lambda n, m: (m, 0)), w_spec] + list(extra_specs),
        out_specs=[pl.BlockSpec((tm, tn), lambda n, m: (m, n)) for _ in out_dtypes],
        out_shape=[jax.ShapeDtypeStruct((M, n_tiles * tn), dt) for dt in out_dtypes],
        scratch_shapes=[scratch],
        compiler_params=_params(("parallel", "arbitrary")),
        name=name,
    )(a, w, *extra)


def _hgrn_gates(fl, lb):
    fl2 = fl * LOG2E
    e = jnp.exp2(-jnp.abs(fl2))
    x2 = jnp.log2(1.0 - lb) + (jnp.minimum(fl2, 0.0) - jnp.log2(1.0 + e))
    x1 = jnp.log2(lb)
    log2_f = jnp.maximum(x1, x2) + jnp.log2(1.0 + jnp.exp2(-jnp.abs(x1 - x2)))
    r = 1.0 / (1.0 + e)
    k = (1.0 - lb) * jnp.where(fl >= 0.0, e * r, r)
    return log2_f, k


def _ep_main(acc, cols, extra, outs):
    silu_flag = extra[0][:, cols]
    scale = extra[1][:, cols]
    gate = jnp.where(silu_flag > 0.0, _sigmoid(acc), 1.0)
    outs[0][:, cols] = (acc * scale * gate).astype(outs[0].dtype)


def _ep_forget(acc, cols, extra, outs):
    ld_ref, k_ref = outs
    log2_f, k = _hgrn_gates(acc, extra[0][:, cols])
    ld_ref[:, cols] = log2_f
    k_ref[:, cols] = k


def _ep_sigmoid(acc, cols, extra, outs):
    outs[0][:, cols] = _sigmoid(acc).astype(outs[0].dtype)


def _ep_copy(acc, cols, extra, outs):
    outs[0][:, cols] = acc.astype(outs[0].dtype)


def _ep_relu2(acc, cols, extra, outs):
    outs[0][:, cols] = jnp.square(jnp.maximum(acc, 0.0)).astype(outs[0].dtype)


def _recurrence_chunks(T, H, K, V, q_ref, load_k, ld_ref, v_ref, g_ref, gain,
                       mall_ref, lvl_ref, st_ref, e_ref, o_ref):
    C = CHUNK

    def chunk(c, carry):
        r0 = pl.multiple_of(c * C, C)
        rows = pl.ds(r0, C)
        lvl = lvl_ref[...]
        hi, lo = _split2(ld_ref[rows, :])
        e_ref[...] = jnp.exp2(_dot(mall_ref[...], jnp.concatenate([hi, lo], axis=0)))
        heads = []
        for h in range(H):
            cs = slice(h * K, (h + 1) * K)
            vs = slice(h * V, (h + 1) * V)
            qc = q_ref[rows, cs].astype(f32)
            kc = load_k(rows, cs)
            vb = v_ref[rows, vs]
            st = st_ref[h]

            o = _dot_nt((qc * e_ref[0:C, cs]).astype(bf16), st.astype(bf16))

            e1 = e_ref[(N_LEVELS + 1) * C:(N_LEVELS + 2) * C, cs]
            diag = jnp.sum(qc * kc, axis=-1, keepdims=True)
            adj = jnp.sum(qc * pltpu.roll(kc, 1, axis=0) * e1, axis=-1, keepdims=True)
            levels = []
            for l in range(1, N_LEVELS):
                el = e_ref[(l + 1) * C:(l + 2) * C, cs]
                levels.append(_dot_nt((qc * el).astype(bf16), (kc * el).astype(bf16)))

            kd = (kc * e_ref[C:2 * C, cs]).astype(bf16)
            heads.append((h, cs, vs, vb, st, kd, o, diag, adj, levels))

        outs = []
        for h, cs, vs, vb, st, kd, o, diag, adj, levels in heads:
            a = jnp.where(lvl == 0, diag, jnp.where(lvl == N_LEVELS, adj, 0.0))
            for l, al in enumerate(levels, start=1):
                a = jnp.where(lvl == l, al, a)
            outs.append(o + _dot(a.astype(bf16), vb))

        for h, cs, vs, vb, st, kd, o, diag, adj, levels in heads:
            st_ref[h] = st * e_ref[C - 1:C, cs] + _dot_tn(vb, kd)

        for (h, cs, vs, *_), o in zip(heads, outs):
            gate = g_ref[rows, vs].astype(f32)
            o_ref[rows, vs] = (_rms(o, gain) * gate).astype(o_ref.dtype)
        return carry

    lax.fori_loop(0, T // C, chunk, 0)


def _write_state(tb, st_ref, s_ref, H):
    @pl.when(tb == pl.num_programs(1) - 1)
    def _():
        for h in range(H):
            s_ref[0, h] = st_ref[h].T


def _hgrn_prompt_kernel(zq_ref, zi_ref, zg_ref, ld_ref, k_ref, gain_ref, mall_ref,
                        lvl_ref, o_ref, s_ref, st_ref, e_ref, *, T, H, K, V):
    tb = pl.program_id(1)

    @pl.when(tb == 0)
    def _():
        st_ref[...] = jnp.zeros_like(st_ref)

    _recurrence_chunks(T, H, K, V, zq_ref, lambda rows, cs: k_ref[rows, cs], ld_ref,
                       zi_ref, zg_ref, gain_ref[...], mall_ref, lvl_ref, st_ref, e_ref, o_ref)
    _write_state(tb, st_ref, s_ref, H)


def _gla_log2_decay(glr, wg, bg):
    g_hi, g_lo = _split2(glr)
    w_hi, w_lo = _split2(wg)
    pre = _dot(g_hi, w_hi) + _dot(g_hi, w_lo) + _dot(g_lo, w_hi) + bg
    return _log_sigmoid(pre) * (LOG2E / GLA_GATE_NORM)


def _gla_prompt_kernel(gqk_ref, gv_ref, gr_ref, glr_ref, wg_ref, bg_ref, gain_ref, mall_ref,
                       lvl_ref, o_ref, s_ref, st_ref, e_ref, ld_ref, *, T, H, K, V):
    tb = pl.program_id(1)

    @pl.when(tb == 0)
    def _():
        st_ref[...] = jnp.zeros_like(st_ref)

    ld_ref[...] = _gla_log2_decay(glr_ref[...], wg_ref[...], bg_ref[...])
    KW = H * K
    k_ref = gqk_ref.at[:, KW:2 * KW]
    _recurrence_chunks(T, H, K, V, gqk_ref, lambda rows, cs: k_ref[rows, cs].astype(f32),
                       ld_ref, gv_ref, gr_ref, gain_ref[...], mall_ref, lvl_ref,
                       st_ref, e_ref, o_ref)
    _write_state(tb, st_ref, s_ref, H)


def _hgrn_prompt(zb, ld, kh, hgrn_norm, mall, lvl, layer, B, L, H, K, V):
    HW = H * K
    T = min(512, L)
    nT = L // T
    tok = lambda col: (lambda b, t: (b * nT + t, col))
    cst = lambda b, t: (0, 0)
    kern = functools.partial(_hgrn_prompt_kernel, T=T, H=H, K=K, V=V)
    return pl.pallas_call(
        kern,
        grid=(B, nT),
        in_specs=[pl.BlockSpec((T, HW), tok(0)),
                  pl.BlockSpec((T, HW), tok(1)),
                  pl.BlockSpec((T, HW), tok(2)),
                  pl.BlockSpec((T, HW), tok(0)),
                  pl.BlockSpec((T, HW), tok(0)),
                  pl.BlockSpec((None, 1, V), lambda b, t: (layer, 0, 0)),
                  pl.BlockSpec(mall.shape, cst),
                  pl.BlockSpec(lvl.shape, cst)],
        out_specs=[pl.BlockSpec((T, HW), tok(0)),
                   pl.BlockSpec((1, H, K, V), lambda b, t: (b, 0, 0, 0))],
        out_shape=[jax.ShapeDtypeStruct((zb.shape[0], H * V), bf16),
                   jax.ShapeDtypeStruct((B, H, K, V), f32)],
        scratch_shapes=[pltpu.VMEM((H, V, K), f32),
                        pltpu.VMEM((mall.shape[0], HW), f32)],
        compiler_params=_params(("parallel", "arbitrary")),
        name="hgrn_prompt",
    )(zb, zb, zb, ld, kh, hgrn_norm, mall, lvl)


def _gla_prompt(zb, glr, wgg, bgg, gla_norm, mall, lvl, layer, B, L, H, K, V):
    KW, VW = H * K, H * V
    T = min(512, L)
    nT = L // T
    tok = lambda col: (lambda b, t: (b * nT + t, col))
    cst = lambda b, t: (0, 0)
    kern = functools.partial(_gla_prompt_kernel, T=T, H=H, K=K, V=V)
    return pl.pallas_call(
        kern,
        grid=(B, nT),
        in_specs=[pl.BlockSpec((T, 2 * KW), tok(3)),
                  pl.BlockSpec((T, VW), tok(4)),
                  pl.BlockSpec((T, VW), tok(5)),
                  pl.BlockSpec((T, LANES), tok(0)),
                  pl.BlockSpec((None, LANES, KW), lambda b, t: (layer, 0, 0)),
                  pl.BlockSpec((None, 1, KW), lambda b, t: (layer, 0, 0)),
                  pl.BlockSpec((None, 1, V), lambda b, t: (layer, 0, 0)),
                  pl.BlockSpec(mall.shape, cst),
                  pl.BlockSpec(lvl.shape, cst)],
        out_specs=[pl.BlockSpec((T, VW), tok(0)),
                   pl.BlockSpec((1, H, K, V), lambda b, t: (b, 0, 0, 0))],
        out_shape=[jax.ShapeDtypeStruct((zb.shape[0], VW), bf16),
                   jax.ShapeDtypeStruct((B, H, K, V), f32)],
        scratch_shapes=[pltpu.VMEM((H, V, K), f32),
                        pltpu.VMEM((mall.shape[0], KW), f32),
                        pltpu.VMEM((T, KW), f32)],
        compiler_params=_params(("parallel", "arbitrary")),
        name="gla_prompt",
    )(zb, zb, zb, glr, wgg, bgg, gla_norm, mall, lvl)


def _step_rows(Bt, H, K, V, a_ref, k_ref, qa_ref, v_ref, s_in_ref, s_out_ref, oi_ref):
    r = lax.broadcasted_iota(jnp.int32, (K, K), 0)
    c = lax.broadcasted_iota(jnp.int32, (K, K), 1)
    eye = r == c
    ones = jnp.ones((2 * K, LANES), bf16)
    r2 = lax.broadcasted_iota(jnp.int32, (2 * K, 2 * LANES), 0)
    c2 = lax.broadcasted_iota(jnp.int32, (2 * K, 2 * LANES), 1)
    ones_pair = jnp.where((r2 < K) == (c2 < LANES), 1.0, 0.0).astype(bf16)
    nv = V // LANES

    def diag(x):
        return jnp.where(eye, jnp.broadcast_to(x, (K, K)), 0.0)

    def body(b, carry):
        cols = []
        for h in range(H):
            cs = slice(h * K, (h + 1) * K)
            a = a_ref[b, :, cs]
            a_hi = a.astype(bf16).astype(f32)
            lhs_a = jnp.concatenate([diag(a_hi), diag(a - a_hi)], axis=1).astype(bf16)
            lhs_kq = jnp.concatenate([diag(k_ref[b, :, cs]), diag(qa_ref[b, :, cs])],
                                     axis=1).astype(bf16)
            cols.append((_dot(lhs_a, ones), _dot(lhs_kq, ones_pair)))
        for h, (a_col, kq) in enumerate(cols):
            k_col, q_col = kq[:, :LANES], kq[:, LANES:]
            for j in range(nv):
                ls = slice(h * V + j * LANES, h * V + (j + 1) * LANES)
                vj = slice(j * LANES, (j + 1) * LANES)
                s = s_in_ref[b, h, :, vj]
                s_out_ref[b, h, :, vj] = a_col * s + k_col * v_ref[b, :, ls]
                oi_ref[b, :, ls] = jnp.sum(q_col * s, axis=0, keepdims=True)
        return carry

    lax.fori_loop(0, Bt, body, 0)


def _store_rows(ref, x):
    for b in range(x.shape[0]):
        ref[b] = x[b:b + 1, :]


def _step_finish(H, K, V, q, k, v, g, gain, oi_ref, o_ref):
    ones = jnp.ones((K, LANES), bf16)
    nv = V // LANES
    for h in range(H):
        cs = slice(h * K, (h + 1) * K)
        vs = slice(h * V, (h + 1) * V)
        qk = _dot((q[:, cs] * k[:, cs]).astype(bf16), ones)
        if nv > 1:
            qk = jnp.concatenate([qk] * nv, axis=1)
        oi = jnp.concatenate([oi_ref[b, :, vs] for b in range(q.shape[0])], axis=0)
        o = qk * v[:, vs] + oi
        o_ref[:, vs] = (_rms(o, gain) * g[:, vs]).astype(o_ref.dtype)


def _step_common(Bt, H, K, V, a, q, k, v, g, gain, a_ref, k_ref, qa_ref, v_ref, oi_ref,
                 s_in_ref, s_out_ref, o_ref):
    _store_rows(a_ref, a)
    _store_rows(k_ref, k)
    _store_rows(qa_ref, q * a)
    _store_rows(v_ref, v)
    _step_rows(Bt, H, K, V, a_ref, k_ref, qa_ref, v_ref, s_in_ref, s_out_ref, oi_ref)
    _step_finish(H, K, V, q, k, v, g, gain, oi_ref, o_ref)


def _hgrn_step_kernel(*refs, Bt, H, K, V, aliased):
    (zq_ref, zi_ref, zg_ref, ld_ref, kh_ref, gain_ref, s_in_ref) = refs[:7]
    o_ref, s_out_ref, a_ref, k_ref, qa_ref, v_ref, oi_ref = refs[7 + aliased:]
    a = jnp.exp2(ld_ref[...])
    _step_common(Bt, H, K, V, a, zq_ref[...].astype(f32), kh_ref[...], zi_ref[...].astype(f32),
                 zg_ref[...].astype(f32), gain_ref[...], a_ref, k_ref, qa_ref, v_ref, oi_ref,
                 s_in_ref, s_out_ref, o_ref)


def _gla_step_kernel(*refs, Bt, H, K, V, aliased):
    (gqk_ref, gv_ref, gr_ref, glr_ref, wg_ref, bg_ref, gain_ref, s_in_ref) = refs[:8]
    o_ref, s_out_ref, a_ref, k_ref, qa_ref, v_ref, oi_ref = refs[8 + aliased:]
    KW = H * K
    a = jnp.exp2(_gla_log2_decay(glr_ref[...], wg_ref[...], bg_ref[...]))
    _step_common(Bt, H, K, V, a, gqk_ref[:, :KW].astype(f32), gqk_ref[:, KW:].astype(f32),
                 gv_ref[...].astype(f32), gr_ref[...].astype(f32), gain_ref[...],
                 a_ref, k_ref, qa_ref, v_ref, oi_ref, s_in_ref, s_out_ref, o_ref)


def _step_call(kern, name, ins, in_specs, state, o_prev, s_prev, layer, row0, Bt, H, K, V, KW):
    depth, Bs = state.shape[:2]
    VW = H * V
    blk0 = row0 // Bt
    in_specs = list(in_specs) + [pl.BlockSpec((None, Bt, H, K, V), lambda i: (layer, i, 0, 0, 0)),
                                 pl.BlockSpec(memory_space=pl.ANY)]
    args = list(ins) + [state, o_prev]
    aliases = {len(args) - 1: 0}
    if s_prev is not None:
        in_specs.append(pl.BlockSpec(memory_space=pl.ANY))
        args.append(s_prev)
        aliases[len(args) - 1] = 1
    return pl.pallas_call(
        functools.partial(kern, Bt=Bt, H=H, K=K, V=V, aliased=len(aliases)),
        grid=(Bs // Bt,),
        in_specs=in_specs,
        out_specs=[pl.BlockSpec((Bt, VW), lambda i: (blk0 + i, 0)),
                   pl.BlockSpec((None, Bt, H, K, V), lambda i: (layer, i, 0, 0, 0))],
        out_shape=[jax.ShapeDtypeStruct(o_prev.shape, bf16),
                   jax.ShapeDtypeStruct((depth, Bs, H, K, V), f32)],
        scratch_shapes=[pltpu.VMEM((Bt, 1, KW), f32)] * 3 + [pltpu.VMEM((Bt, 1, VW), f32)] * 2,
        input_output_aliases=aliases,
        compiler_params=_params(("arbitrary",)),
        name=name,
    )(*args)


def _hgrn_step(zb, ld, kh, hgrn_norm, state, o_prev, s_prev, layer, row0, H, K, V):
    HW = H * K
    Bt = min(16, state.shape[1])
    blk0 = row0 // Bt
    row = lambda col: (lambda i: (blk0 + i, col))
    in_specs = [pl.BlockSpec((Bt, HW), row(0)),
                pl.BlockSpec((Bt, HW), row(1)),
                pl.BlockSpec((Bt, HW), row(2)),
                pl.BlockSpec((Bt, HW), row(0)),
                pl.BlockSpec((Bt, HW), row(0)),
                pl.BlockSpec((None, 1, V), lambda i: (layer, 0, 0))]
    return _step_call(_hgrn_step_kernel, "hgrn_step", (zb, zb, zb, ld, kh, hgrn_norm),
                      in_specs, state, o_prev, s_prev, layer, row0, Bt, H, K, V, HW)


def _gla_step(zb, glr, wgg, bgg, gla_norm, state, o_prev, s_prev, layer, row0, H, K, V):
    KW, VW = H * K, H * V
    Bt = min(16, state.shape[1])
    blk0 = row0 // Bt
    row = lambda col: (lambda i: (blk0 + i, col))
    in_specs = [pl.BlockSpec((Bt, 2 * KW), row(3)),
                pl.BlockSpec((Bt, VW), row(4)),
                pl.BlockSpec((Bt, VW), row(5)),
                pl.BlockSpec((Bt, LANES), row(0)),
                pl.BlockSpec((None, LANES, KW), lambda i: (layer, 0, 0)),
                pl.BlockSpec((None, 1, KW), lambda i: (layer, 0, 0)),
                pl.BlockSpec((None, 1, V), lambda i: (layer, 0, 0))]
    return _step_call(_gla_step_kernel, "gla_step", (zb, zb, zb, glr, wgg, bgg, gla_norm),
                      in_specs, state, o_prev, s_prev, layer, row0, Bt, H, K, V, KW)


def _postmix_kernel(oh_ref, og_ref, mh_ref, mg_ref, h_ref, whu_ref, wgu_ref, wout_ref,
                    npost_ref, npre_ref, h1_ref, c_ref):
    yh = _dot(oh_ref[...], whu_ref[...])
    yg = _dot(og_ref[...], wgu_ref[...])
    merged = mh_ref[...].astype(f32) * yh + mg_ref[...].astype(f32) * yg
    t = _dot(merged.astype(bf16), wout_ref[...])
    h1 = h_ref[...] + _rms(t, npost_ref[...])
    h1_ref[...] = h1
    c_ref[...] = _rms(h1, npre_ref[...]).astype(c_ref.dtype)


def _postmix(oh, og, zm, h, whu, wgu, wout, npost, npre, layer):
    M, D = h.shape
    HW, VW = oh.shape[1], og.shape[1]
    tm = _row_tile(M, 320)
    lay = lambda m: (layer, 0, 0)
    return pl.pallas_call(
        _postmix_kernel,
        grid=(M // tm,),
        in_specs=[pl.BlockSpec((tm, HW), lambda m: (m, 0)),
                  pl.BlockSpec((tm, VW), lambda m: (m, 0)),
                  pl.BlockSpec((tm, D), lambda m: (m, 0)),
                  pl.BlockSpec((tm, D), lambda m: (m, 1)),
                  pl.BlockSpec((tm, D), lambda m: (m, 0)),
                  _const_spec((None, HW, D), lay),
                  _const_spec((None, VW, D), lay),
                  _const_spec((None, D, D), lay),
                  pl.BlockSpec((None, 1, D), lay),
                  pl.BlockSpec((None, 1, D), lay)],
        out_specs=[pl.BlockSpec((tm, D), lambda m: (m, 0)),
                   pl.BlockSpec((tm, D), lambda m: (m, 0))],
        out_shape=[jax.ShapeDtypeStruct((M, D), f32),
                   jax.ShapeDtypeStruct((M, D), bf16)],
        compiler_params=_params(("parallel",)),
        name="postmix",
    )(oh, og, zm, zm, h, whu, wgu, wout, npost, npre)


def _ple_kernel(t_ref, h_ref, p_ref, wg_ref, wp_ref, npost_ref, nnext_ref, h3_ref, a_ref):
    h = h_ref[...] + _rms(t_ref[...], npost_ref[...])
    gate = _sigmoid(_dot(h.astype(bf16), wg_ref[...]))
    pe = _dot(p_ref[...].astype(bf16), wp_ref[...])
    h3 = h + gate * pe
    h3_ref[...] = h3
    a_ref[...] = _rms(h3, nnext_ref[...]).astype(a_ref.dtype)


def _ple(t2, h1, p, wg, wp, npost, nnext, layer, next_layer):
    M, D = h1.shape
    P = p.shape[-1]
    tm = _row_tile(M, 512)
    lay = lambda m: (layer, 0, 0)
    return pl.pallas_call(
        _ple_kernel,
        grid=(M // tm,),
        in_specs=[pl.BlockSpec((tm, D), lambda m: (m, 0)),
                  pl.BlockSpec((tm, D), lambda m: (m, 0)),
                  pl.BlockSpec((None, tm, P), lambda m: (layer, m, 0)),
                  _const_spec((None, D, D), lay),
                  _const_spec((None, P, D), lay),
                  pl.BlockSpec((None, 1, D), lay),
                  pl.BlockSpec((None, 1, D), lambda m: (next_layer, 0, 0))],
        out_specs=[pl.BlockSpec((tm, D), lambda m: (m, 0)),
                   pl.BlockSpec((tm, D), lambda m: (m, 0))],
        out_shape=[jax.ShapeDtypeStruct((M, D), f32),
                   jax.ShapeDtypeStruct((M, D), bf16)],
        compiler_params=_params(("parallel",)),
        name="ple",
    )(t2, h1, p, wg, wp, npost, nnext)


def kernel(x_prompt, x_sample, p_prompt, p_sample, state_hgrn, state_gla, norm_pre_mix,
           norm_post_mix, norm_pre_ffn, norm_post_ffn, w_in, lb_param, hgrn_norm, w_hgrn_up,
           w_gla_gate, b_gla_gate, gla_norm, w_gla_up, w_out, w_ff1, w_ff2, w_ple, w_ple_gate):
    B, L, D = x_prompt.shape
    Bs = x_sample.shape[0]
    depth = w_in.shape[0]
    _, _, HH, HK, HV = state_hgrn.shape
    _, _, GH, GK, GV = state_gla.shape
    HW, GKW, GVW = HH * HK, GH * GK, GH * GV
    R = w_gla_gate.shape[1]
    F = w_ff1.shape[2]
    assert HK == LANES and GK == LANES and HV % LANES == 0 and GV % LANES == 0
    assert HH * HV == HW and 2 * GKW == HW and GVW == HW and R <= LANES
    assert x_sample.shape[1] == 1 and L % CHUNK == 0 and D % HW == 0
    mixer_cols = 4 * HW + 2 * GKW + 2 * GVW

    assert w_in.shape[2] - mixer_cols >= LANES and (mixer_cols + R) % 8 == 0
    w_in_t = jnp.swapaxes(w_in, 1, 2)
    whu = w_hgrn_up.astype(bf16)
    wgu = w_gla_up.astype(bf16)
    wout = w_out.astype(bf16)
    wpg = w_ple_gate.astype(bf16)
    wp = w_ple.astype(bf16)
    wgg = jnp.concatenate([w_gla_gate.astype(f32), jnp.zeros((depth, LANES - R, GKW), f32)], axis=1)

    lbs = jnp.cumsum(jax.nn.softmax(lb_param.astype(f32), axis=0), axis=0)
    lbs = (lbs - lbs[0:1]).reshape(depth, 1, HW)
    r3 = lambda t: t.astype(f32).reshape(depth, 1, t.shape[-1])
    n_pre_mix, n_post_mix, n_pre_ffn, n_post_ffn = map(
        r3, (norm_pre_mix, norm_post_mix, norm_pre_ffn, norm_post_ffn))
    hn, gn, bgg = r3(hgrn_norm), r3(gla_norm), r3(b_gla_gate)
    mall, lvl = _level_tables()
    seg_w = (HW, HW, HW, GKW, GKW, GVW, GVW)
    silu_flag = jnp.asarray(np.concatenate(
        [np.full((1, w), v, np.float32) for w, v in zip(seg_w, (1, 0, 1, 0, 0, 0, 1))], axis=1))
    col_scale = jnp.asarray(np.concatenate(
        [np.full((1, w), v, np.float32) for w, v in zip(seg_w, (1, 1, 1, GK ** -0.5, 1, 1, 1))],
        axis=1))
    tm_mg = _pick(D, (1024, 512, 256, 128))
    tf = _pick(F, (1024, 512, 256, 128))
    tn2 = _pick(D, (512, 256, 128))

    def in_projections(a):
        col_spec = pl.BlockSpec((1, HW), lambda n, m: (0, n))
        zb, = _proj(a, w_in_t, i, lambda n: (n + jnp.minimum(n, 1)) * HW, 6, HW, _ep_main,
                    (silu_flag, col_scale), (col_spec, col_spec), (bf16,), "in_proj", True,
                    tm_cap=2048)
        ld, kh = _proj(a, w_in_t, i, lambda n: HW, 1, HW, _ep_forget, (lbs,),
                       (pl.BlockSpec((None, 1, HW), lambda n, m: (i, 0, 0)),),
                       (f32, f32), "in_proj_forget", True, tm_cap=512)
        zm, = _proj(a, w_in_t, i, lambda n: mixer_cols + R + n * tm_mg, 2 * D // tm_mg, tm_mg,
                    _ep_sigmoid, (), (), (bf16,), "in_proj_merge", True, tm_cap=2048)
        glr, = _proj(a, w_in_t, i, lambda n: mixer_cols, 1, LANES, _ep_copy, (), (), (f32,),
                     "in_proj_lowrank", True)
        return zb, ld, kh, zm, glr

    def dense_tail(h, oh, og, zm, p):
        h1, c = _postmix(oh, og, zm, h, whu, wgu, wout, n_post_mix, n_pre_ffn, i)
        u, = _proj(c, w_ff1, i, lambda n: n, F // tf, tf, _ep_relu2, (), (), (bf16,),
                   "ffn_up", False, tm_cap=2048)
        t2, = _proj(u, w_ff2, i, lambda n: n, D // tn2, tn2, _ep_copy, (), (), (f32,),
                    "ffn_down", False, tm_cap=640, w_single=True)
        return _ple(t2, h1, p, wpg, wp, n_post_ffn, n_pre_mix, i, (i + 1) % depth)

    BL = B * L
    assert BL % 16 == 0 and Bs % 16 == 0
    h = jnp.concatenate([x_prompt.reshape(BL, D), x_sample.reshape(Bs, D)], axis=0)
    p = jnp.concatenate([p_prompt.reshape(depth, BL, -1), p_sample.reshape(depth, Bs, -1)], axis=1)
    a = _norm(h, n_pre_mix, 0)

    hgrn_p, gla_p = [], []
    hgrn_s = gla_s = None
    for i in range(depth):
        zb, ld, kh, zm, glr = in_projections(a)
        oh, sh = _hgrn_prompt(zb, ld, kh, hn, mall, lvl, i, B, L, HH, HK, HV)
        oh, hgrn_s = _hgrn_step(zb, ld, kh, hn, state_hgrn, oh, hgrn_s, i, BL, HH, HK, HV)
        og, sg = _gla_prompt(zb, glr, wgg, bgg, gn, mall, lvl, i, B, L, GH, GK, GV)
        og, gla_s = _gla_step(zb, glr, wgg, bgg, gn, state_gla, og, gla_s, i, BL, GH, GK, GV)
        h, a = dense_tail(h, oh, og, zm, p)
        hgrn_p.append(sh)
        gla_p.append(sg)

    return (h[:BL].reshape(B, L, D), h[BL:].reshape(Bs, 1, D),
            jnp.stack(hgrn_p).astype(state_hgrn.dtype), jnp.stack(gla_p).astype(state_gla.dtype),
            hgrn_s.astype(state_hgrn.dtype), gla_s.astype(state_gla.dtype))
```

```python
import functools

import numpy as np
import jax
import jax.numpy as jnp
from jax import lax
from jax.experimental import pallas as pl
from jax.experimental.pallas import tpu as pltpu

EPS = 1e-6
GLA_GATE_NORM = 16.0
LOG2E = 1.4426950408889634
LANES = 128
CHUNK = 64
N_LEVELS = 6
VMEM_LIMIT = 56 * 1024 * 1024

f32 = jnp.float32
bf16 = jnp.bfloat16


def _sigmoid(x):
    return 0.5 * jnp.tanh(0.5 * x) + 0.5


def _silu(x):
    return x * _sigmoid(x)


def _log_sigmoid(x):
    return jnp.minimum(x, 0.0) - jnp.log(1.0 + jnp.exp(-jnp.abs(x)))


def _rms(x, g):
    return x * lax.rsqrt(jnp.mean(x * x, axis=-1, keepdims=True) + EPS) * g


def _dot(a, b):
    return jnp.dot(a, b, preferred_element_type=f32)


def _dot_nt(a, b):
    return lax.dot_general(a, b, (((1,), (1,)), ((), ())), preferred_element_type=f32)


def _dot_tn(a, b):
    return lax.dot_general(a, b, (((0,), (0,)), ((), ())), preferred_element_type=f32)


def _split2(x):
    hi = x.astype(bf16)
    lo = (x - hi.astype(f32)).astype(bf16)
    return hi, lo


def _params(sem):
    return pltpu.CompilerParams(dimension_semantics=sem, vmem_limit_bytes=VMEM_LIMIT)


def _const_spec(shape, index_map):
    return pl.BlockSpec(shape, index_map, pipeline_mode=pl.Buffered(1))


def _pick(n, cands):
    for c in cands:
        if n % c == 0:
            return c
    return n


def _row_tile(m, cap):
    for t in range(min(cap, m), 0, -1):
        if m % t == 0 and t % 16 == 0:
            return t
    return m


def _level_tables():
    C = CHUNK
    t = np.arange(C)[:, None]
    j = np.arange(C)[None, :]
    mats = [(j <= t), (j > t)]
    lvl = np.full((C, C), -1, np.int32)
    lvl[t == j] = 0
    w = C // 2
    level = 1
    while w >= 1:
        start = (t // (2 * w)) * (2 * w)
        m = start + w - 1
        second = (t - start) >= w
        mats.append(np.where(second, (j > m) & (j <= t), (j > t) & (j <= m)))
        same = (t // (2 * w)) == (j // (2 * w))
        lvl[same & second & ((j - (j // (2 * w)) * (2 * w)) < w)] = level
        w //= 2
        level += 1
    mall = np.concatenate(mats, axis=0).astype(np.float32)
    return jnp.asarray(np.concatenate([mall, mall], axis=1), bf16), jnp.asarray(lvl)


def _norm_kernel(x_ref, g_ref, o_ref):
    o_ref[...] = _rms(x_ref[...], g_ref[...]).astype(o_ref.dtype)


def _norm(x, gains, layer):
    M, D = x.shape
    tm = _row_tile(M, 512)
    return pl.pallas_call(
        _norm_kernel,
        grid=(M // tm,),
        in_specs=[pl.BlockSpec((tm, D), lambda m: (m, 0)),
                  pl.BlockSpec((None, 1, D), lambda m: (layer, 0, 0))],
        out_specs=pl.BlockSpec((tm, D), lambda m: (m, 0)),
        out_shape=jax.ShapeDtypeStruct((M, D), bf16),
        compiler_params=_params(("parallel",)),
        name="rmsnorm",
    )(x, gains)


def _proj_kernel(*refs, n_extra, epilogue, transposed, tn, slab):
    a_ref, w_ref = refs[0], refs[1]
    extra = refs[2:2 + n_extra]
    outs = refs[2 + n_extra:-1]
    wbf_ref = refs[-1]

    @pl.when(pl.program_id(1) == 0)
    def _():
        w = w_ref[0] if transposed else w_ref[...]
        wbf_ref[...] = w.astype(bf16)

    a = a_ref[...]
    for j in range(tn // slab):
        cols = slice(j * slab, (j + 1) * slab)
        if transposed:
            acc = _dot_nt(a, wbf_ref[cols, :])
        else:
            acc = _dot(a, wbf_ref[:, cols])
        epilogue(acc, cols, extra, outs)


def _proj(a, w, layer, w_off, n_tiles, tn, epilogue, extra, extra_specs, out_dtypes, name,
          transposed, tm_cap=1024, w_single=False):
    M, K = a.shape
    tm = _row_tile(M, tm_cap)
    if transposed:
        w_shape = (pl.Element(1), pl.Element(tn), pl.Element(K))
        def w_map(n, m):
            off = w_off(n)
            return (layer, off if isinstance(off, int) else pl.multiple_of(off, 8), 0)
        scratch = pltpu.VMEM((tn, K), bf16)
    else:
        w_shape = (None, K, tn)
        w_map = lambda n, m: (layer, 0, w_off(n))
        scratch = pltpu.VMEM((K, tn), bf16)
    w_spec = _const_spec(w_shape, w_map) if w_single else pl.BlockSpec(w_shape, w_map)
    kern = functools.partial(_proj_kernel, n_extra=len(extra), epilogue=epilogue,
                             transposed=transposed, tn=tn, slab=min(tn, 2 * LANES))
    return pl.pallas_call(
        kern,
        grid=(n_tiles, M // tm),
        in_specs=[pl.BlockSpec((tm, K), lambda n, m: (m, 0)), w_spec] + list(extra_specs),
        out_specs=[pl.BlockSpec((tm, tn), lambda n, m: (m, n)) for _ in out_dtypes],
        out_shape=[jax.ShapeDtypeStruct((M, n_tiles * tn), dt) for dt in out_dtypes],
        scratch_shapes=[scratch],
        compiler_params=_params(("parallel", "arbitrary")),
        name=name,
    )(a, w, *extra)


def _hgrn_gates(fl, lb):
    fl2 = fl * LOG2E
    e = jnp.exp2(-jnp.abs(fl2))
    x2 = jnp.log2(1.0 - lb) + (jnp.minimum(fl2, 0.0) - jnp.log2(1.0 + e))
    x1 = jnp.log2(lb)
    log2_f = jnp.maximum(x1, x2) + jnp.log2(1.0 + jnp.exp2(-jnp.abs(x1 - x2)))
    r = 1.0 / (1.0 + e)
    k = (1.0 - lb) * jnp.where(fl >= 0.0, e * r, r)
    return log2_f, k


def _ep_main(acc, cols, extra, outs):
    silu_flag = extra[0][:, cols]
    scale = extra[1][:, cols]
    gate = jnp.where(silu_flag > 0.0, _sigmoid(acc), 1.0)
    outs[0][:, cols] = (acc * scale * gate).astype(outs[0].dtype)


def _ep_forget(acc, cols, extra, outs):
    ld_ref, k_ref = outs
    log2_f, k = _hgrn_gates(acc, extra[0][:, cols])
    ld_ref[:, cols] = log2_f
    k_ref[:, cols] = k


def _ep_sigmoid(acc, cols, extra, outs):
    outs[0][:, cols] = _sigmoid(acc).astype(outs[0].dtype)


def _ep_copy(acc, cols, extra, outs):
    outs[0][:, cols] = acc.astype(outs[0].dtype)


def _ep_relu2(acc, cols, extra, outs):
    outs[0][:, cols] = jnp.square(jnp.maximum(acc, 0.0)).astype(outs[0].dtype)


def _recurrence_chunks(T, H, K, V, q_ref, load_k, ld_ref, v_ref, g_ref, gain,
                       mall_ref, lvl_ref, st_ref, e_ref, o_ref):
    C = CHUNK

    def chunk(c, carry):
        r0 = pl.multiple_of(c * C, C)
        rows = pl.ds(r0, C)
        lvl = lvl_ref[...]
        hi, lo = _split2(ld_ref[rows, :])
        e_ref[...] = jnp.exp2(_dot(mall_ref[...], jnp.concatenate([hi, lo], axis=0)))
        heads = []
        for h in range(H):
            cs = slice(h * K, (h + 1) * K)
            vs = slice(h * V, (h + 1) * V)
            qc = q_ref[rows, cs].astype(f32)
            kc = load_k(rows, cs)
            vb = v_ref[rows, vs]
            st = st_ref[h]

            o = _dot_nt((qc * e_ref[0:C, cs]).astype(bf16), st.astype(bf16))

            e1 = e_ref[(N_LEVELS + 1) * C:(N_LEVELS + 2) * C, cs]
            diag = jnp.sum(qc * kc, axis=-1, keepdims=True)
            adj = jnp.sum(qc * pltpu.roll(kc, 1, axis=0) * e1, axis=-1, keepdims=True)
            levels = []
            for l in range(1, N_LEVELS):
                el = e_ref[(l + 1) * C:(l + 2) * C, cs]
                levels.append(_dot_nt((qc * el).astype(bf16), (kc * el).astype(bf16)))

            kd = (kc * e_ref[C:2 * C, cs]).astype(bf16)
            alpha = e_ref[C - 1:C, cs]
            heads.append((h, vs, vb, st, kd, alpha, o, diag, adj, levels))

        outs = []
        for h, vs, vb, st, kd, alpha, o, diag, adj, levels in heads:
            a = jnp.where(lvl == 0, diag, jnp.where(lvl == N_LEVELS, adj, 0.0))
            for l, al in enumerate(levels, start=1):
                a = jnp.where(lvl == l, al, a)
            outs.append(o + _dot(a.astype(bf16), vb))

        for h, vs, vb, st, kd, alpha, o, diag, adj, levels in heads:
            st_ref[h] = st * alpha + _dot_tn(vb, kd)

        for (h, vs, *_), o in zip(heads, outs):
            gate = g_ref[rows, vs].astype(f32)
            o_ref[rows, vs] = (_rms(o, gain) * gate).astype(o_ref.dtype)
        return carry

    lax.fori_loop(0, T // C, chunk, 0)


def _write_state(tb, st_ref, s_ref, H):
    @pl.when(tb == pl.num_programs(1) - 1)
    def _():
        for h in range(H):
            s_ref[0, h] = st_ref[h].T


def _hgrn_prompt_kernel(zq_ref, zi_ref, zg_ref, ld_ref, k_ref, gain_ref, mall_ref,
                        lvl_ref, o_ref, s_ref, st_ref, e_ref, *, T, H, K, V):
    tb = pl.program_id(1)

    @pl.when(tb == 0)
    def _():
        st_ref[...] = jnp.zeros_like(st_ref)

    _recurrence_chunks(T, H, K, V, zq_ref, lambda rows, cs: k_ref[rows, cs], ld_ref,
                       zi_ref, zg_ref, gain_ref[...], mall_ref, lvl_ref, st_ref, e_ref, o_ref)
    _write_state(tb, st_ref, s_ref, H)


def _gla_log2_decay(glr, wg, bg):
    g_hi, g_lo = _split2(glr)
    w_hi, w_lo = _split2(wg)
    pre = _dot(g_hi, w_hi) + _dot(g_hi, w_lo) + _dot(g_lo, w_hi) + bg
    return _log_sigmoid(pre) * (LOG2E / GLA_GATE_NORM)


def _gla_prompt_kernel(gqk_ref, gv_ref, gr_ref, glr_ref, wg_ref, bg_ref, gain_ref, mall_ref,
                       lvl_ref, o_ref, s_ref, st_ref, ld_ref, e_ref, *, T, H, K, V):
    tb = pl.program_id(1)

    @pl.when(tb == 0)
    def _():
        st_ref[...] = jnp.zeros_like(st_ref)

    ld_ref[...] = _gla_log2_decay(glr_ref[...], wg_ref[...], bg_ref[...])
    KW = H * K
    k_ref = gqk_ref.at[:, KW:2 * KW]
    _recurrence_chunks(T, H, K, V, gqk_ref, lambda rows, cs: k_ref[rows, cs].astype(f32),
                       ld_ref, gv_ref, gr_ref, gain_ref[...], mall_ref, lvl_ref,
                       st_ref, e_ref, o_ref)
    _write_state(tb, st_ref, s_ref, H)


def _hgrn_prompt(zb, ld, kh, hgrn_norm, mall, lvl, layer, B, L, H, K, V):
    HW = H * K
    T = min(512, L)
    nT = L // T
    tok = lambda col: (lambda b, t: (b * nT + t, col))
    cst = lambda b, t: (0, 0)
    kern = functools.partial(_hgrn_prompt_kernel, T=T, H=H, K=K, V=V)
    return pl.pallas_call(
        kern,
        grid=(B, nT),
        in_specs=[pl.BlockSpec((T, HW), tok(0)),
                  pl.BlockSpec((T, HW), tok(1)),
                  pl.BlockSpec((T, HW), tok(2)),
                  pl.BlockSpec((T, HW), tok(0)),
                  pl.BlockSpec((T, HW), tok(0)),
                  pl.BlockSpec((None, 1, V), lambda b, t: (layer, 0, 0)),
                  pl.BlockSpec(mall.shape, cst),
                  pl.BlockSpec(lvl.shape, cst)],
        out_specs=[pl.BlockSpec((T, HW), tok(0)),
                   pl.BlockSpec((1, H, K, V), lambda b, t: (b, 0, 0, 0))],
        out_shape=[jax.ShapeDtypeStruct((zb.shape[0], H * V), bf16),
                   jax.ShapeDtypeStruct((B, H, K, V), f32)],
        scratch_shapes=[pltpu.VMEM((H, V, K), f32),
                        pltpu.VMEM((mall.shape[0], HW), f32)],
        compiler_params=_params(("parallel", "arbitrary")),
        name="hgrn_prompt",
    )(zb, zb, zb, ld, kh, hgrn_norm, mall, lvl)


def _gla_prompt(zb, glr, wgg, bgg, gla_norm, mall, lvl, layer, B, L, H, K, V):
    KW, VW = H * K, H * V
    T = min(512, L)
    nT = L // T
    tok = lambda col: (lambda b, t: (b * nT + t, col))
    cst = lambda b, t: (0, 0)
    kern = functools.partial(_gla_prompt_kernel, T=T, H=H, K=K, V=V)
    return pl.pallas_call(
        kern,
        grid=(B, nT),
        in_specs=[pl.BlockSpec((T, 2 * KW), tok(3)),
                  pl.BlockSpec((T, VW), tok(4)),
                  pl.BlockSpec((T, VW), tok(5)),
                  pl.BlockSpec((T, LANES), tok(0)),
                  pl.BlockSpec((None, LANES, KW), lambda b, t: (layer, 0, 0)),
                  pl.BlockSpec((None, 1, KW), lambda b, t: (layer, 0, 0)),
                  pl.BlockSpec((None, 1, V), lambda b, t: (layer, 0, 0)),
                  pl.BlockSpec(mall.shape, cst),
                  pl.BlockSpec(lvl.shape, cst)],
        out_specs=[pl.BlockSpec((T, VW), tok(0)),
                   pl.BlockSpec((1, H, K, V), lambda b, t: (b, 0, 0, 0))],
        out_shape=[jax.ShapeDtypeStruct((zb.shape[0], VW), bf16),
                   jax.ShapeDtypeStruct((B, H, K, V), f32)],
        scratch_shapes=[pltpu.VMEM((H, V, K), f32),
                        pltpu.VMEM((T, KW), f32),
                        pltpu.VMEM((mall.shape[0], KW), f32)],
        compiler_params=_params(("parallel", "arbitrary")),
        name="gla_prompt",
    )(zb, zb, zb, glr, wgg, bgg, gla_norm, mall, lvl)


def _step_rows(Bt, H, K, V, a_ref, k_ref, qa_ref, v_ref, s_in_ref, s_out_ref, oi_ref):
    r = lax.broadcasted_iota(jnp.int32, (K, K), 0)
    c = lax.broadcasted_iota(jnp.int32, (K, K), 1)
    eye = r == c
    ones = jnp.ones((K, LANES), bf16)
    nv = V // LANES

    def diag(x):
        return jnp.where(eye, jnp.broadcast_to(x, (K, K)), 0.0)

    def body(b, carry):
        pieces = []
        for h in range(H):
            cs = slice(h * K, (h + 1) * K)
            a = a_ref[b, :, cs]
            a_hi = a.astype(bf16).astype(f32)
            pieces += [diag(a_hi), diag(a - a_hi), diag(k_ref[b, :, cs]), diag(qa_ref[b, :, cs])]
        cb = _dot(jnp.concatenate(pieces, axis=0).astype(bf16), ones)
        for h in range(H):
            r0 = 4 * h * K
            a_col = cb[r0:r0 + K] + cb[r0 + K:r0 + 2 * K]
            k_col, q_col = cb[r0 + 2 * K:r0 + 3 * K], cb[r0 + 3 * K:r0 + 4 * K]
            for j in range(nv):
                ls = slice(h * V + j * LANES, h * V + (j + 1) * LANES)
                vj = slice(j * LANES, (j + 1) * LANES)
                s = s_in_ref[b, h, :, vj]
                s_out_ref[b, h, :, vj] = a_col * s + k_col * v_ref[b, :, ls]
                oi_ref[b, :, ls] = jnp.sum(q_col * s, axis=0, keepdims=True)
        return carry

    lax.fori_loop(0, Bt, body, 0)


def _store_rows(ref, x):
    for b in range(x.shape[0]):
        ref[b] = x[b:b + 1, :]


def _step_finish(H, K, V, q, k, v, g, gain, oi_ref, o_ref):
    ones = jnp.ones((K, LANES), bf16)
    nv = V // LANES
    for h in range(H):
        cs = slice(h * K, (h + 1) * K)
        vs = slice(h * V, (h + 1) * V)
        qk = _dot((q[:, cs] * k[:, cs]).astype(bf16), ones)
        if nv > 1:
            qk = jnp.concatenate([qk] * nv, axis=1)
        oi = jnp.concatenate([oi_ref[b, :, vs] for b in range(q.shape[0])], axis=0)
        o = qk * v[:, vs] + oi
        o_ref[:, vs] = (_rms(o, gain) * g[:, vs]).astype(o_ref.dtype)


def _step_common(Bt, H, K, V, a, q, k, v, g, gain, a_ref, k_ref, qa_ref, v_ref, oi_ref,
                 s_in_ref, s_out_ref, o_ref):
    _store_rows(a_ref, a)
    _store_rows(k_ref, k)
    _store_rows(qa_ref, q * a)
    _store_rows(v_ref, v)
    _step_rows(Bt, H, K, V, a_ref, k_ref, qa_ref, v_ref, s_in_ref, s_out_ref, oi_ref)
    _step_finish(H, K, V, q, k, v, g, gain, oi_ref, o_ref)


def _hgrn_step_kernel(*refs, Bt, H, K, V, aliased):
    (zq_ref, zi_ref, zg_ref, ld_ref, kh_ref, gain_ref, s_in_ref) = refs[:7]
    o_ref, s_out_ref, a_ref, k_ref, qa_ref, v_ref, oi_ref = refs[7 + aliased:]
    a = jnp.exp2(ld_ref[...])
    _step_common(Bt, H, K, V, a, zq_ref[...].astype(f32), kh_ref[...], zi_ref[...].astype(f32),
                 zg_ref[...].astype(f32), gain_ref[...], a_ref, k_ref, qa_ref, v_ref, oi_ref,
                 s_in_ref, s_out_ref, o_ref)


def _gla_step_kernel(*refs, Bt, H, K, V, aliased):
    (gqk_ref, gv_ref, gr_ref, glr_ref, wg_ref, bg_ref, gain_ref, s_in_ref) = refs[:8]
    o_ref, s_out_ref, a_ref, k_ref, qa_ref, v_ref, oi_ref = refs[8 + aliased:]
    KW = H * K
    a = jnp.exp2(_gla_log2_decay(glr_ref[...], wg_ref[...], bg_ref[...]))
    _step_common(Bt, H, K, V, a, gqk_ref[:, :KW].astype(f32), gqk_ref[:, KW:].astype(f32),
                 gv_ref[...].astype(f32), gr_ref[...].astype(f32), gain_ref[...],
                 a_ref, k_ref, qa_ref, v_ref, oi_ref, s_in_ref, s_out_ref, o_ref)


def _step_call(kern, name, ins, in_specs, state, o_prev, s_prev, layer, row0, Bt, H, K, V, KW):
    depth, Bs = state.shape[:2]
    VW = H * V
    blk0 = row0 // Bt
    in_specs = list(in_specs) + [pl.BlockSpec((None, Bt, H, K, V), lambda i: (layer, i, 0, 0, 0)),
                                 pl.BlockSpec(memory_space=pl.ANY)]
    args = list(ins) + [state, o_prev]
    aliases = {len(args) - 1: 0}
    if s_prev is not None:
        in_specs.append(pl.BlockSpec(memory_space=pl.ANY))
        args.append(s_prev)
        aliases[len(args) - 1] = 1
    return pl.pallas_call(
        functools.partial(kern, Bt=Bt, H=H, K=K, V=V, aliased=len(aliases)),
        grid=(Bs // Bt,),
        in_specs=in_specs,
        out_specs=[pl.BlockSpec((Bt, VW), lambda i: (blk0 + i, 0)),
                   pl.BlockSpec((None, Bt, H, K, V), lambda i: (layer, i, 0, 0, 0))],
        out_shape=[jax.ShapeDtypeStruct(o_prev.shape, bf16),
                   jax.ShapeDtypeStruct((depth, Bs, H, K, V), f32)],
        scratch_shapes=[pltpu.VMEM((Bt, 1, KW), f32)] * 3 + [pltpu.VMEM((Bt, 1, VW), f32)] * 2,
        input_output_aliases=aliases,
        compiler_params=_params(("arbitrary",)),
        name=name,
    )(*args)


def _hgrn_step(zb, ld, kh, hgrn_norm, state, o_prev, s_prev, layer, row0, H, K, V):
    HW = H * K
    Bt = min(16, state.shape[1])
    blk0 = row0 // Bt
    row = lambda col: (lambda i: (blk0 + i, col))
    in_specs = [pl.BlockSpec((Bt, HW), row(0)),
                pl.BlockSpec((Bt, HW), row(1)),
                pl.BlockSpec((Bt, HW), row(2)),
                pl.BlockSpec((Bt, HW), row(0)),
                pl.BlockSpec((Bt, HW), row(0)),
                pl.BlockSpec((None, 1, V), lambda i: (layer, 0, 0))]
    return _step_call(_hgrn_step_kernel, "hgrn_step", (zb, zb, zb, ld, kh, hgrn_norm),
                      in_specs, state, o_prev, s_prev, layer, row0, Bt, H, K, V, HW)


def _gla_step(zb, glr, wgg, bgg, gla_norm, state, o_prev, s_prev, layer, row0, H, K, V):
    KW, VW = H * K, H * V
    Bt = min(16, state.shape[1])
    blk0 = row0 // Bt
    row = lambda col: (lambda i: (blk0 + i, col))
    in_specs = [pl.BlockSpec((Bt, 2 * KW), row(3)),
                pl.BlockSpec((Bt, VW), row(4)),
                pl.BlockSpec((Bt, VW), row(5)),
                pl.BlockSpec((Bt, LANES), row(0)),
                pl.BlockSpec((None, LANES, KW), lambda i: (layer, 0, 0)),
                pl.BlockSpec((None, 1, KW), lambda i: (layer, 0, 0)),
                pl.BlockSpec((None, 1, V), lambda i: (layer, 0, 0))]
    return _step_call(_gla_step_kernel, "gla_step", (zb, zb, zb, glr, wgg, bgg, gla_norm),
                      in_specs, state, o_prev, s_prev, layer, row0, Bt, H, K, V, KW)


def _postmix_kernel(oh_ref, og_ref, mh_ref, mg_ref, h_ref, whu_ref, wgu_ref, wout_ref,
                    npost_ref, npre_ref, h1_ref, c_ref):
    yh = _dot(oh_ref[...], whu_ref[...])
    yg = _dot(og_ref[...], wgu_ref[...])
    merged = mh_ref[...].astype(f32) * yh + mg_ref[...].astype(f32) * yg
    t = _dot(merged.astype(bf16), wout_ref[...])
    h1 = h_ref[...] + _rms(t, npost_ref[...])
    h1_ref[...] = h1
    c_ref[...] = _rms(h1, npre_ref[...]).astype(c_ref.dtype)


def _postmix(oh, og, zm, h, whu, wgu, wout, npost, npre, layer):
    M, D = h.shape
    HW, VW = oh.shape[1], og.shape[1]
    tm = _row_tile(M, 320)
    lay = lambda m: (layer, 0, 0)
    return pl.pallas_call(
        _postmix_kernel,
        grid=(M // tm,),
        in_specs=[pl.BlockSpec((tm, HW), lambda m: (m, 0)),
                  pl.BlockSpec((tm, VW), lambda m: (m, 0)),
                  pl.BlockSpec((tm, D), lambda m: (m, 0)),
                  pl.BlockSpec((tm, D), lambda m: (m, 1)),
                  pl.BlockSpec((tm, D), lambda m: (m, 0)),
                  _const_spec((None, HW, D), lay),
                  _const_spec((None, VW, D), lay),
                  _const_spec((None, D, D), lay),
                  pl.BlockSpec((None, 1, D), lay),
                  pl.BlockSpec((None, 1, D), lay)],
        out_specs=[pl.BlockSpec((tm, D), lambda m: (m, 0)),
                   pl.BlockSpec((tm, D), lambda m: (m, 0))],
        out_shape=[jax.ShapeDtypeStruct((M, D), f32),
                   jax.ShapeDtypeStruct((M, D), bf16)],
        compiler_params=_params(("parallel",)),
        name="postmix",
    )(oh, og, zm, zm, h, whu, wgu, wout, npost, npre)


def _ple_kernel(t_ref, h_ref, p_ref, wg_ref, wp_ref, npost_ref, nnext_ref, h3_ref, a_ref):
    h = h_ref[...] + _rms(t_ref[...], npost_ref[...])
    gate = _sigmoid(_dot(h.astype(bf16), wg_ref[...]))
    pe = _dot(p_ref[...].astype(bf16), wp_ref[...])
    h3 = h + gate * pe
    h3_ref[...] = h3
    a_ref[...] = _rms(h3, nnext_ref[...]).astype(a_ref.dtype)


def _ple(t2, h1, p, wg, wp, npost, nnext, layer, next_layer):
    M, D = h1.shape
    P = p.shape[-1]
    tm = _row_tile(M, 512)
    lay = lambda m: (layer, 0, 0)
    return pl.pallas_call(
        _ple_kernel,
        grid=(M // tm,),
        in_specs=[pl.BlockSpec((tm, D), lambda m: (m, 0)),
                  pl.BlockSpec((tm, D), lambda m: (m, 0)),
                  pl.BlockSpec((None, tm, P), lambda m: (layer, m, 0)),
                  _const_spec((None, D, D), lay),
                  _const_spec((None, P, D), lay),
                  pl.BlockSpec((None, 1, D), lay),
                  pl.BlockSpec((None, 1, D), lambda m: (next_layer, 0, 0))],
        out_specs=[pl.BlockSpec((tm, D), lambda m: (m, 0)),
                   pl.BlockSpec((tm, D), lambda m: (m, 0))],
        out_shape=[jax.ShapeDtypeStruct((M, D), f32),
                   jax.ShapeDtypeStruct((M, D), bf16)],
        compiler_params=_params(("parallel",)),
        name="ple",
    )(t2, h1, p, wg, wp, npost, nnext)


def kernel(x_prompt, x_sample, p_prompt, p_sample, state_hgrn, state_gla, norm_pre_mix,
           norm_post_mix, norm_pre_ffn, norm_post_ffn, w_in, lb_param, hgrn_norm, w_hgrn_up,
           w_gla_gate, b_gla_gate, gla_norm, w_gla_up, w_out, w_ff1, w_ff2, w_ple, w_ple_gate):
    B, L, D = x_prompt.shape
    Bs = x_sample.shape[0]
    depth = w_in.shape[0]
    _, _, HH, HK, HV = state_hgrn.shape
    _, _, GH, GK, GV = state_gla.shape
    HW, GKW, GVW = HH * HK, GH * GK, GH * GV
    R = w_gla_gate.shape[1]
    F = w_ff1.shape[2]
    assert HK == LANES and GK == LANES and HV % LANES == 0 and GV % LANES == 0
    assert HH * HV == HW and 2 * GKW == HW and GVW == HW and R <= LANES
    assert x_sample.shape[1] == 1 and L % CHUNK == 0 and D % HW == 0
    mixer_cols = 4 * HW + 2 * GKW + 2 * GVW

    assert w_in.shape[2] - mixer_cols >= LANES and (mixer_cols + R) % 8 == 0
    w_in_t = jnp.swapaxes(w_in, 1, 2)
    whu = w_hgrn_up.astype(bf16)
    wgu = w_gla_up.astype(bf16)
    wout = w_out.astype(bf16)
    wpg = w_ple_gate.astype(bf16)
    wp = w_ple.astype(bf16)
    wgg = jnp.concatenate([w_gla_gate.astype(f32), jnp.zeros((depth, LANES - R, GKW), f32)], axis=1)

    lbs = jnp.cumsum(jax.nn.softmax(lb_param.astype(f32), axis=0), axis=0)
    lbs = (lbs - lbs[0:1]).reshape(depth, 1, HW)
    r3 = lambda t: t.astype(f32).reshape(depth, 1, t.shape[-1])
    n_pre_mix, n_post_mix, n_pre_ffn, n_post_ffn = map(
        r3, (norm_pre_mix, norm_post_mix, norm_pre_ffn, norm_post_ffn))
    hn, gn, bgg = r3(hgrn_norm), r3(gla_norm), r3(b_gla_gate)
    mall, lvl = _level_tables()
    seg_w = (HW, HW, HW, GKW, GKW, GVW, GVW)
    silu_flag = jnp.asarray(np.concatenate(
        [np.full((1, w), v, np.float32) for w, v in zip(seg_w, (1, 0, 1, 0, 0, 0, 1))], axis=1))
    col_scale = jnp.asarray(np.concatenate(
        [np.full((1, w), v, np.float32) for w, v in zip(seg_w, (1, 1, 1, GK ** -0.5, 1, 1, 1))],
        axis=1))
    tm_mg = _pick(D, (1024, 512, 256, 128))
    tf = _pick(F, (1024, 512, 256, 128))
    tn2 = _pick(D, (512, 256, 128))

    def in_projections(a):
        col_spec = pl.BlockSpec((1, HW), lambda n, m: (0, n))
        zb, = _proj(a, w_in_t, i, lambda n: (n + jnp.minimum(n, 1)) * HW, 6, HW, _ep_main,
                    (silu_flag, col_scale), (col_spec, col_spec), (bf16,), "in_proj", True,
                    tm_cap=2048)
        ld, kh = _proj(a, w_in_t, i, lambda n: HW, 1, HW, _ep_forget, (lbs,),
                       (pl.BlockSpec((None, 1, HW), lambda n, m: (i, 0, 0)),),
                       (f32, f32), "in_proj_forget", True, tm_cap=512)
        zm, = _proj(a, w_in_t, i, lambda n: mixer_cols + R + n * tm_mg, 2 * D // tm_mg, tm_mg,
                    _ep_sigmoid, (), (), (bf16,), "in_proj_merge", True, tm_cap=2048)
        glr, = _proj(a, w_in_t, i, lambda n: mixer_cols, 1, LANES, _ep_copy, (), (), (f32,),
                     "in_proj_lowrank", True, tm_cap=2048)
        return zb, ld, kh, zm, glr

    def dense_tail(h, oh, og, zm, p):
        h1, c = _postmix(oh, og, zm, h, whu, wgu, wout, n_post_mix, n_pre_ffn, i)
        u, = _proj(c, w_ff1, i, lambda n: n, F // tf, tf, _ep_relu2, (), (), (bf16,),
                   "ffn_up", False, tm_cap=2048)
        t2, = _proj(u, w_ff2, i, lambda n: n, D // tn2, tn2, _ep_copy, (), (), (f32,),
                    "ffn_down", False, tm_cap=640, w_single=True)
        return _ple(t2, h1, p, wpg, wp, n_post_ffn, n_pre_mix, i, (i + 1) % depth)

    BL = B * L
    assert BL % 16 == 0 and Bs % 16 == 0
    h = jnp.concatenate([x_prompt.reshape(BL, D), x_sample.reshape(Bs, D)], axis=0)
    p = jnp.concatenate([p_prompt.reshape(depth, BL, -1), p_sample.reshape(depth, Bs, -1)], axis=1)
    a = _norm(h, n_pre_mix, 0)

    hgrn_p, gla_p = [], []
    hgrn_s = gla_s = None
    for i in range(depth):
        zb, ld, kh, zm, glr = in_projections(a)
        oh, sh = _hgrn_prompt(zb, ld, kh, hn, mall, lvl, i, B, L, HH, HK, HV)
        oh, hgrn_s = _hgrn_step(zb, ld, kh, hn, state_hgrn, oh, hgrn_s, i, BL, HH, HK, HV)
        og, sg = _gla_prompt(zb, glr, wgg, bgg, gn, mall, lvl, i, B, L, GH, GK, GV)
        og, gla_s = _gla_step(zb, glr, wgg, bgg, gn, state_gla, og, gla_s, i, BL, GH, GK, GV)
        h, a = dense_tail(h, oh, og, zm, p)
        hgrn_p.append(sh)
        gla_p.append(sg)

    return (h[:BL].reshape(B, L, D), h[BL:].reshape(Bs, 1, D),
            jnp.stack(hgrn_p).astype(state_hgrn.dtype), jnp.stack(gla_p).astype(state_gla.dtype),
            hgrn_s.astype(state_hgrn.dtype), gla_s.astype(state_gla.dtype))
```

```python
import functools

import numpy as np
import jax
import jax.numpy as jnp
from jax import lax
from jax.experimental import pallas as pl
from jax.experimental.pallas import tpu as pltpu

EPS = 1e-6
GLA_GATE_NORM = 16.0
LOG2E = 1.4426950408889634
LANES = 128
CHUNK = 64
N_LEVELS = 6
VMEM_LIMIT = 56 * 1024 * 1024

f32 = jnp.float32
bf16 = jnp.bfloat16


def _sigmoid(x):
    return 0.5 * jnp.tanh(0.5 * x) + 0.5


def _silu(x):
    return x * _sigmoid(x)


def _log_sigmoid(x):
    return jnp.minimum(x, 0.0) - jnp.log(1.0 + jnp.exp(-jnp.abs(x)))


def _rms(x, g):
    return x * lax.rsqrt(jnp.mean(x * x, axis=-1, keepdims=True) + EPS) * g


def _dot(a, b):
    return jnp.dot(a, b, preferred_element_type=f32)


def _dot_nt(a, b):
    return lax.dot_general(a, b, (((1,), (1,)), ((), ())), preferred_element_type=f32)


def _dot_tn(a, b):
    return lax.dot_general(a, b, (((0,), (0,)), ((), ())), preferred_element_type=f32)


def _split2(x):
    hi = x.astype(bf16)
    lo = (x - hi.astype(f32)).astype(bf16)
    return hi, lo


def _params(sem):
    return pltpu.CompilerParams(dimension_semantics=sem, vmem_limit_bytes=VMEM_LIMIT)


def _const_spec(shape, index_map):
    return pl.BlockSpec(shape, index_map, pipeline_mode=pl.Buffered(1))


def _pick(n, cands):
    for c in cands:
        if n % c == 0:
            return c
    return n


def _row_tile(m, cap):
    for t in range(min(cap, m), 0, -1):
        if m % t == 0 and t % 16 == 0:
            return t
    return m


def _level_tables():
    C = CHUNK
    t = np.arange(C)[:, None]
    j = np.arange(C)[None, :]
    mats = [(j <= t), (j > t)]
    lvl = np.full((C, C), -1, np.int32)
    lvl[t == j] = 0
    w = C // 2
    level = 1
    while w >= 1:
        start = (t // (2 * w)) * (2 * w)
        m = start + w - 1
        second = (t - start) >= w
        mats.append(np.where(second, (j > m) & (j <= t), (j > t) & (j <= m)))
        same = (t // (2 * w)) == (j // (2 * w))
        lvl[same & second & ((j - (j // (2 * w)) * (2 * w)) < w)] = level
        w //= 2
        level += 1
    mall = np.concatenate(mats[:-1], axis=0).astype(np.float32)
    return jnp.asarray(np.concatenate([mall, mall], axis=1), bf16), jnp.asarray(lvl)


def _norm_kernel(x_ref, g_ref, o_ref):
    o_ref[...] = _rms(x_ref[...], g_ref[...]).astype(o_ref.dtype)


def _norm(x, gains, layer):
    M, D = x.shape
    tm = _row_tile(M, 512)
    return pl.pallas_call(
        _norm_kernel,
        grid=(M // tm,),
        in_specs=[pl.BlockSpec((tm, D), lambda m: (m, 0)),
                  pl.BlockSpec((None, 1, D), lambda m: (layer, 0, 0))],
        out_specs=pl.BlockSpec((tm, D), lambda m: (m, 0)),
        out_shape=jax.ShapeDtypeStruct((M, D), bf16),
        compiler_params=_params(("parallel",)),
        name="rmsnorm",
    )(x, gains)


def _proj_kernel(*refs, n_extra, epilogue, transposed, tn, slab):
    a_ref, w_ref = refs[0], refs[1]
    extra = refs[2:2 + n_extra]
    outs = refs[2 + n_extra:-1]
    wbf_ref = refs[-1]

    @pl.when(pl.program_id(1) == 0)
    def _():
        w = w_ref[0] if transposed else w_ref[...]
        wbf_ref[...] = w.astype(bf16)

    a = a_ref[...]
    for j in range(tn // slab):
        cols = slice(j * slab, (j + 1) * slab)
        if transposed:
            acc = _dot_nt(a, wbf_ref[cols, :])
        else:
            acc = _dot(a, wbf_ref[:, cols])
        epilogue(acc, cols, extra, outs)


def _proj(a, w, layer, w_off, n_tiles, tn, epilogue, extra, extra_specs, out_dtypes, name,
          transposed, tm_cap=1024, w_single=False):
    M, K = a.shape
    tm = _row_tile(M, tm_cap)
    if transposed:
        w_shape = (pl.Element(1), pl.Element(tn), pl.Element(K))
        def w_map(n, m):
            off = w_off(n)
            return (layer, off if isinstance(off, int) else pl.multiple_of(off, 8), 0)
        scratch = pltpu.VMEM((tn, K), bf16)
    else:
        w_shape = (None, K, tn)
        w_map = lambda n, m: (layer, 0, w_off(n))
        scratch = pltpu.VMEM((K, tn), bf16)
    w_spec = _const_spec(w_shape, w_map) if w_single else pl.BlockSpec(w_shape, w_map)
    kern = functools.partial(_proj_kernel, n_extra=len(extra), epilogue=epilogue,
                             transposed=transposed, tn=tn, slab=min(tn, 2 * LANES))
    return pl.pallas_call(
        kern,
        grid=(n_tiles, M // tm),
        in_specs=[pl.BlockSpec((tm, K), lambda n, m: (m, 0)), w_spec] + list(extra_specs),
        out_specs=[pl.BlockSpec((tm, tn), lambda n, m: (m, n)) for _ in out_dtypes],
        out_shape=[jax.ShapeDtypeStruct((M, n_tiles * tn), dt) for dt in out_dtypes],
        scratch_shapes=[scratch],
        compiler_params=_params(("parallel", "arbitrary")),
        name=name,
    )(a, w, *extra)


def _hgrn_gates(fl, lb):
    fl2 = fl * LOG2E
    e = jnp.exp2(-jnp.abs(fl2))
    x2 = jnp.log2(1.0 - lb) + (jnp.minimum(fl2, 0.0) - jnp.log2(1.0 + e))
    x1 = jnp.log2(lb)
    log2_f = jnp.maximum(x1, x2) + jnp.log2(1.0 + jnp.exp2(-jnp.abs(x1 - x2)))
    r = 1.0 / (1.0 + e)
    k = (1.0 - lb) * jnp.where(fl >= 0.0, e * r, r)
    return log2_f, k


def _ep_main(acc, cols, extra, outs):
    silu_flag = extra[0][:, cols]
    scale = extra[1][:, cols]
    gate = jnp.where(silu_flag > 0.0, _sigmoid(acc), 1.0)
    outs[0][:, cols] = (acc * scale * gate).astype(outs[0].dtype)


def _ep_forget(acc, cols, extra, outs):
    ld_ref, k_ref = outs
    log2_f, k = _hgrn_gates(acc, extra[0][:, cols])
    ld_ref[:, cols] = log2_f
    k_ref[:, cols] = k


def _ep_sigmoid(acc, cols, extra, outs):
    outs[0][:, cols] = _sigmoid(acc).astype(outs[0].dtype)


def _ep_copy(acc, cols, extra, outs):
    outs[0][:, cols] = acc.astype(outs[0].dtype)


def _ep_relu2(acc, cols, extra, outs):
    outs[0][:, cols] = jnp.square(jnp.maximum(acc, 0.0)).astype(outs[0].dtype)


def _recurrence_chunks(T, H, K, V, q_ref, load_k, ld_ref, v_ref, g_ref, gain,
                       mall_ref, lvl_ref, st_ref, e_ref, o_ref):
    C = CHUNK

    def chunk(c, carry):
        r0 = pl.multiple_of(c * C, C)
        rows = pl.ds(r0, C)
        lvl = lvl_ref[...]
        hi, lo = _split2(ld_ref[rows, :])
        e_ref[...] = jnp.exp2(_dot(mall_ref[...], jnp.concatenate([hi, lo], axis=0)))
        heads = []
        for h in range(H):
            cs = slice(h * K, (h + 1) * K)
            vs = slice(h * V, (h + 1) * V)
            qc = q_ref[rows, cs].astype(f32)
            kc = load_k(rows, cs)
            vb = v_ref[rows, vs]
            st = st_ref[h]

            o = _dot_nt((qc * e_ref[0:C, cs]).astype(bf16), st.astype(bf16))

            diag = jnp.sum(qc * kc, axis=-1, keepdims=True)
            adj = jnp.sum(qc * pltpu.roll(kc, 1, axis=0) * jnp.exp2(ld_ref[rows, cs]),
                          axis=-1, keepdims=True)
            levels = []
            for l in range(1, N_LEVELS):
                el = e_ref[(l + 1) * C:(l + 2) * C, cs]
                levels.append(_dot_nt((qc * el).astype(bf16), (kc * el).astype(bf16)))

            kd = (kc * e_ref[C:2 * C, cs]).astype(bf16)
            alpha = e_ref[C - 1:C, cs]
            heads.append((h, vs, vb, st, kd, alpha, o, diag, adj, levels))

        outs = []
        for h, vs, vb, st, kd, alpha, o, diag, adj, levels in heads:
            a = jnp.where(lvl == 0, diag, jnp.where(lvl == N_LEVELS, adj, 0.0))
            for l, al in enumerate(levels, start=1):
                a = jnp.where(lvl == l, al, a)
            outs.append(o + _dot(a.astype(bf16), vb))

        for h, vs, vb, st, kd, alpha, o, diag, adj, levels in heads:
            st_ref[h] = st * alpha + _dot_tn(vb, kd)

        for (h, vs, *_), o in zip(heads, outs):
            gate = g_ref[rows, vs].astype(f32)
            o_ref[rows, vs] = (_rms(o, gain) * gate).astype(o_ref.dtype)
        return carry

    lax.fori_loop(0, T // C, chunk, 0)


def _write_state(tb, st_ref, s_ref, H):
    @pl.when(tb == pl.num_programs(1) - 1)
    def _():
        for h in range(H):
            s_ref[0, h] = st_ref[h].T


def _hgrn_prompt_kernel(zq_ref, zi_ref, zg_ref, ld_ref, k_ref, gain_ref, mall_ref,
                        lvl_ref, o_ref, s_ref, st_ref, e_ref, *, T, H, K, V):
    tb = pl.program_id(1)

    @pl.when(tb == 0)
    def _():
        st_ref[...] = jnp.zeros_like(st_ref)

    _recurrence_chunks(T, H, K, V, zq_ref, lambda rows, cs: k_ref[rows, cs], ld_ref,
                       zi_ref, zg_ref, gain_ref[...], mall_ref, lvl_ref, st_ref, e_ref, o_ref)
    _write_state(tb, st_ref, s_ref, H)


def _gla_log2_decay(glr, wg, bg):
    g_hi, g_lo = _split2(glr)
    w_hi, w_lo = _split2(wg)
    pre = _dot(g_hi, w_hi) + _dot(g_hi, w_lo) + _dot(g_lo, w_hi) + bg
    return _log_sigmoid(pre) * (LOG2E / GLA_GATE_NORM)


def _gla_prompt_kernel(gqk_ref, gv_ref, gr_ref, glr_ref, wg_ref, bg_ref, gain_ref, mall_ref,
                       lvl_ref, o_ref, s_ref, st_ref, ld_ref, e_ref, *, T, H, K, V):
    tb = pl.program_id(1)

    @pl.when(tb == 0)
    def _():
        st_ref[...] = jnp.zeros_like(st_ref)

    ld_ref[...] = _gla_log2_decay(glr_ref[...], wg_ref[...], bg_ref[...])
    KW = H * K
    k_ref = gqk_ref.at[:, KW:2 * KW]
    _recurrence_chunks(T, H, K, V, gqk_ref, lambda rows, cs: k_ref[rows, cs].astype(f32),
                       ld_ref, gv_ref, gr_ref, gain_ref[...], mall_ref, lvl_ref,
                       st_ref, e_ref, o_ref)
    _write_state(tb, st_ref, s_ref, H)


def _hgrn_prompt(zb, ld, kh, hgrn_norm, mall, lvl, layer, B, L, H, K, V):
    HW = H * K
    T = min(1024, L)
    nT = L // T
    tok = lambda col: (lambda b, t: (b * nT + t, col))
    cst = lambda b, t: (0, 0)
    kern = functools.partial(_hgrn_prompt_kernel, T=T, H=H, K=K, V=V)
    return pl.pallas_call(
        kern,
        grid=(B, nT),
        in_specs=[pl.BlockSpec((T, HW), tok(0)),
                  pl.BlockSpec((T, HW), tok(1)),
                  pl.BlockSpec((T, HW), tok(2)),
                  pl.BlockSpec((T, HW), tok(0)),
                  pl.BlockSpec((T, HW), tok(0)),
                  pl.BlockSpec((None, 1, V), lambda b, t: (layer, 0, 0)),
                  pl.BlockSpec(mall.shape, cst),
                  pl.BlockSpec(lvl.shape, cst)],
        out_specs=[pl.BlockSpec((T, HW), tok(0)),
                   pl.BlockSpec((1, H, K, V), lambda b, t: (b, 0, 0, 0))],
        out_shape=[jax.ShapeDtypeStruct((zb.shape[0], H * V), bf16),
                   jax.ShapeDtypeStruct((B, H, K, V), f32)],
        scratch_shapes=[pltpu.VMEM((H, V, K), f32),
                        pltpu.VMEM((mall.shape[0], HW), f32)],
        compiler_params=_params(("parallel", "arbitrary")),
        name="hgrn_prompt",
    )(zb, zb, zb, ld, kh, hgrn_norm, mall, lvl)


def _gla_prompt(zb, glr, wgg, bgg, gla_norm, mall, lvl, layer, B, L, H, K, V):
    KW, VW = H * K, H * V
    T = min(1024, L)
    nT = L // T
    tok = lambda col: (lambda b, t: (b * nT + t, col))
    cst = lambda b, t: (0, 0)
    kern = functools.partial(_gla_prompt_kernel, T=T, H=H, K=K, V=V)
    return pl.pallas_call(
        kern,
        grid=(B, nT),
        in_specs=[pl.BlockSpec((T, 2 * KW), tok(3)),
                  pl.BlockSpec((T, VW), tok(4)),
                  pl.BlockSpec((T, VW), tok(5)),
                  pl.BlockSpec((T, LANES), tok(0)),
                  pl.BlockSpec((None, LANES, KW), lambda b, t: (layer, 0, 0)),
                  pl.BlockSpec((None, 1, KW), lambda b, t: (layer, 0, 0)),
                  pl.BlockSpec((None, 1, V), lambda b, t: (layer, 0, 0)),
                  pl.BlockSpec(mall.shape, cst),
                  pl.BlockSpec(lvl.shape, cst)],
        out_specs=[pl.BlockSpec((T, VW), tok(0)),
                   pl.BlockSpec((1, H, K, V), lambda b, t: (b, 0, 0, 0))],
        out_shape=[jax.ShapeDtypeStruct((zb.shape[0], VW), bf16),
                   jax.ShapeDtypeStruct((B, H, K, V), f32)],
        scratch_shapes=[pltpu.VMEM((H, V, K), f32),
                        pltpu.VMEM((T, KW), f32),
                        pltpu.VMEM((mall.shape[0], KW), f32)],
        compiler_params=_params(("parallel", "arbitrary")),
        name="gla_prompt",
    )(zb, zb, zb, glr, wgg, bgg, gla_norm, mall, lvl)


def _step_rows(Bt, H, K, V, a_ref, k_ref, qa_ref, v_ref, s_in_ref, s_out_ref, oi_ref):
    r = lax.broadcasted_iota(jnp.int32, (K, K), 0)
    c = lax.broadcasted_iota(jnp.int32, (K, K), 1)
    eye = r == c
    ones = jnp.ones((K, LANES), bf16)
    nv = V // LANES

    def diag(x):
        return jnp.where(eye, jnp.broadcast_to(x, (K, K)), 0.0)

    def body(b, carry):
        pieces = []
        for h in range(H):
            cs = slice(h * K, (h + 1) * K)
            a = a_ref[b, :, cs]
            a_hi = a.astype(bf16).astype(f32)
            pieces += [diag(a_hi), diag(a - a_hi), diag(k_ref[b, :, cs]), diag(qa_ref[b, :, cs])]
        cb = _dot(jnp.concatenate(pieces, axis=0).astype(bf16), ones)
        for h in range(H):
            r0 = 4 * h * K
            a_col = cb[r0:r0 + K] + cb[r0 + K:r0 + 2 * K]
            k_col, q_col = cb[r0 + 2 * K:r0 + 3 * K], cb[r0 + 3 * K:r0 + 4 * K]
            for j in range(nv):
                ls = slice(h * V + j * LANES, h * V + (j + 1) * LANES)
                vj = slice(j * LANES, (j + 1) * LANES)
                s = s_in_ref[b, h, :, vj]
                s_out_ref[b, h, :, vj] = a_col * s + k_col * v_ref[b, :, ls]
                oi_ref[b, :, ls] = jnp.sum(q_col * s, axis=0, keepdims=True)
        return carry

    lax.fori_loop(0, Bt, body, 0)


def _store_rows(ref, x):
    for b in range(x.shape[0]):
        ref[b] = x[b:b + 1, :]


def _step_finish(H, K, V, q, k, v, g, gain, oi_ref, o_ref):
    ones = jnp.ones((K, LANES), bf16)
    nv = V // LANES
    for h in range(H):
        cs = slice(h * K, (h + 1) * K)
        vs = slice(h * V, (h + 1) * V)
        qk = _dot((q[:, cs] * k[:, cs]).astype(bf16), ones)
        if nv > 1:
            qk = jnp.concatenate([qk] * nv, axis=1)
        oi = jnp.concatenate([oi_ref[b, :, vs] for b in range(q.shape[0])], axis=0)
        o = qk * v[:, vs] + oi
        o_ref[:, vs] = (_rms(o, gain) * g[:, vs]).astype(o_ref.dtype)


def _step_common(Bt, H, K, V, a, q, k, v, g, gain, a_ref, k_ref, qa_ref, v_ref, oi_ref,
                 s_in_ref, s_out_ref, o_ref):
    _store_rows(a_ref, a)
    _store_rows(k_ref, k)
    _store_rows(qa_ref, q * a)
    _store_rows(v_ref, v)
    _step_rows(Bt, H, K, V, a_ref, k_ref, qa_ref, v_ref, s_in_ref, s_out_ref, oi_ref)
    _step_finish(H, K, V, q, k, v, g, gain, oi_ref, o_ref)


def _hgrn_step_kernel(*refs, Bt, H, K, V, aliased):
    (zq_ref, zi_ref, zg_ref, ld_ref, kh_ref, gain_ref, s_in_ref) = refs[:7]
    o_ref, s_out_ref, a_ref, k_ref, qa_ref, v_ref, oi_ref = refs[7 + aliased:]
    a = jnp.exp2(ld_ref[...])
    _step_common(Bt, H, K, V, a, zq_ref[...].astype(f32), kh_ref[...], zi_ref[...].astype(f32),
                 zg_ref[...].astype(f32), gain_ref[...], a_ref, k_ref, qa_ref, v_ref, oi_ref,
                 s_in_ref, s_out_ref, o_ref)


def _gla_step_kernel(*refs, Bt, H, K, V, aliased):
    (gqk_ref, gv_ref, gr_ref, glr_ref, wg_ref, bg_ref, gain_ref, s_in_ref) = refs[:8]
    o_ref, s_out_ref, a_ref, k_ref, qa_ref, v_ref, oi_ref = refs[8 + aliased:]
    KW = H * K
    a = jnp.exp2(_gla_log2_decay(glr_ref[...], wg_ref[...], bg_ref[...]))
    _step_common(Bt, H, K, V, a, gqk_ref[:, :KW].astype(f32), gqk_ref[:, KW:].astype(f32),
                 gv_ref[...].astype(f32), gr_ref[...].astype(f32), gain_ref[...],
                 a_ref, k_ref, qa_ref, v_ref, oi_ref, s_in_ref, s_out_ref, o_ref)


def _step_call(kern, name, ins, in_specs, state, o_prev, s_prev, layer, row0, Bt, H, K, V, KW):
    depth, Bs = state.shape[:2]
    VW = H * V
    blk0 = row0 // Bt
    in_specs = list(in_specs) + [pl.BlockSpec((None, Bt, H, K, V), lambda i: (layer, i, 0, 0, 0)),
                                 pl.BlockSpec(memory_space=pl.ANY)]
    args = list(ins) + [state, o_prev]
    aliases = {len(args) - 1: 0}
    if s_prev is not None:
        in_specs.append(pl.BlockSpec(memory_space=pl.ANY))
        args.append(s_prev)
        aliases[len(args) - 1] = 1
    return pl.pallas_call(
        functools.partial(kern, Bt=Bt, H=H, K=K, V=V, aliased=len(aliases)),
        grid=(Bs // Bt,),
        in_specs=in_specs,
        out_specs=[pl.BlockSpec((Bt, VW), lambda i: (blk0 + i, 0)),
                   pl.BlockSpec((None, Bt, H, K, V), lambda i: (layer, i, 0, 0, 0))],
        out_shape=[jax.ShapeDtypeStruct(o_prev.shape, bf16),
                   jax.ShapeDtypeStruct((depth, Bs, H, K, V), f32)],
        scratch_shapes=[pltpu.VMEM((Bt, 1, KW), f32)] * 3 + [pltpu.VMEM((Bt, 1, VW), f32)] * 2,
        input_output_aliases=aliases,
        compiler_params=_params(("arbitrary",)),
        name=name,
    )(*args)


def _hgrn_step(zb, ld, kh, hgrn_norm, state, o_prev, s_prev, layer, row0, H, K, V):
    HW = H * K
    Bt = min(16, state.shape[1])
    blk0 = row0 // Bt
    row = lambda col: (lambda i: (blk0 + i, col))
    in_specs = [pl.BlockSpec((Bt, HW), row(0)),
                pl.BlockSpec((Bt, HW), row(1)),
                pl.BlockSpec((Bt, HW), row(2)),
                pl.BlockSpec((Bt, HW), row(0)),
                pl.BlockSpec((Bt, HW), row(0)),
                pl.BlockSpec((None, 1, V), lambda i: (layer, 0, 0))]
    return _step_call(_hgrn_step_kernel, "hgrn_step", (zb, zb, zb, ld, kh, hgrn_norm),
                      in_specs, state, o_prev, s_prev, layer, row0, Bt, H, K, V, HW)


def _gla_step(zb, glr, wgg, bgg, gla_norm, state, o_prev, s_prev, layer, row0, H, K, V):
    KW, VW = H * K, H * V
    Bt = min(16, state.shape[1])
    blk0 = row0 // Bt
    row = lambda col: (lambda i: (blk0 + i, col))
    in_specs = [pl.BlockSpec((Bt, 2 * KW), row(3)),
                pl.BlockSpec((Bt, VW), row(4)),
                pl.BlockSpec((Bt, VW), row(5)),
                pl.BlockSpec((Bt, LANES), row(0)),
                pl.BlockSpec((None, LANES, KW), lambda i: (layer, 0, 0)),
                pl.BlockSpec((None, 1, KW), lambda i: (layer, 0, 0)),
                pl.BlockSpec((None, 1, V), lambda i: (layer, 0, 0))]
    return _step_call(_gla_step_kernel, "gla_step", (zb, zb, zb, glr, wgg, bgg, gla_norm),
                      in_specs, state, o_prev, s_prev, layer, row0, Bt, H, K, V, KW)


def _postmix_kernel(oh_ref, og_ref, mh_ref, mg_ref, h_ref, whu_ref, wgu_ref, wout_ref,
                    npost_ref, npre_ref, h1_ref, c_ref):
    yh = _dot(oh_ref[...], whu_ref[...])
    yg = _dot(og_ref[...], wgu_ref[...])
    merged = mh_ref[...].astype(f32) * yh + mg_ref[...].astype(f32) * yg
    t = _dot(merged.astype(bf16), wout_ref[...])
    h1 = h_ref[...] + _rms(t, npost_ref[...])
    h1_ref[...] = h1
    c_ref[...] = _rms(h1, npre_ref[...]).astype(c_ref.dtype)


def _postmix(oh, og, zm, h, whu, wgu, wout, npost, npre, layer):
    M, D = h.shape
    HW, VW = oh.shape[1], og.shape[1]
    tm = _row_tile(M, 320)
    lay = lambda m: (layer, 0, 0)
    return pl.pallas_call(
        _postmix_kernel,
        grid=(M // tm,),
        in_specs=[pl.BlockSpec((tm, HW), lambda m: (m, 0)),
                  pl.BlockSpec((tm, VW), lambda m: (m, 0)),
                  pl.BlockSpec((tm, D), lambda m: (m, 0)),
                  pl.BlockSpec((tm, D), lambda m: (m, 1)),
                  pl.BlockSpec((tm, D), lambda m: (m, 0)),
                  _const_spec((None, HW, D), lay),
                  _const_spec((None, VW, D), lay),
                  _const_spec((None, D, D), lay),
                  pl.BlockSpec((None, 1, D), lay),
                  pl.BlockSpec((None, 1, D), lay)],
        out_specs=[pl.BlockSpec((tm, D), lambda m: (m, 0)),
                   pl.BlockSpec((tm, D), lambda m: (m, 0))],
        out_shape=[jax.ShapeDtypeStruct((M, D), f32),
                   jax.ShapeDtypeStruct((M, D), bf16)],
        compiler_params=_params(("parallel",)),
        name="postmix",
    )(oh, og, zm, zm, h, whu, wgu, wout, npost, npre)


def _ple_kernel(t_ref, h_ref, p_ref, wg_ref, wp_ref, npost_ref, nnext_ref, h3_ref, a_ref):
    h = h_ref[...] + _rms(t_ref[...], npost_ref[...])
    gate = _sigmoid(_dot(h.astype(bf16), wg_ref[...]))
    pe = _dot(p_ref[...].astype(bf16), wp_ref[...])
    h3 = h + gate * pe
    h3_ref[...] = h3
    a_ref[...] = _rms(h3, nnext_ref[...]).astype(a_ref.dtype)


def _ple(t2, h1, p, wg, wp, npost, nnext, layer, next_layer):
    M, D = h1.shape
    P = p.shape[-1]
    tm = _row_tile(M, 512)
    lay = lambda m: (layer, 0, 0)
    return pl.pallas_call(
        _ple_kernel,
        grid=(M // tm,),
        in_specs=[pl.BlockSpec((tm, D), lambda m: (m, 0)),
                  pl.BlockSpec((tm, D), lambda m: (m, 0)),
                  pl.BlockSpec((None, tm, P), lambda m: (layer, m, 0)),
                  _const_spec((None, D, D), lay),
                  _const_spec((None, P, D), lay),
                  pl.BlockSpec((None, 1, D), lay),
                  pl.BlockSpec((None, 1, D), lambda m: (next_layer, 0, 0))],
        out_specs=[pl.BlockSpec((tm, D), lambda m: (m, 0)),
                   pl.BlockSpec((tm, D), lambda m: (m, 0))],
        out_shape=[jax.ShapeDtypeStruct((M, D), f32),
                   jax.ShapeDtypeStruct((M, D), bf16)],
        compiler_params=_params(("parallel",)),
        name="ple",
    )(t2, h1, p, wg, wp, npost, nnext)


def kernel(x_prompt, x_sample, p_prompt, p_sample, state_hgrn, state_gla, norm_pre_mix,
           norm_post_mix, norm_pre_ffn, norm_post_ffn, w_in, lb_param, hgrn_norm, w_hgrn_up,
           w_gla_gate, b_gla_gate, gla_norm, w_gla_up, w_out, w_ff1, w_ff2, w_ple, w_ple_gate):
    B, L, D = x_prompt.shape
    Bs = x_sample.shape[0]
    depth = w_in.shape[0]
    _, _, HH, HK, HV = state_hgrn.shape
    _, _, GH, GK, GV = state_gla.shape
    HW, GKW, GVW = HH * HK, GH * GK, GH * GV
    R = w_gla_gate.shape[1]
    F = w_ff1.shape[2]
    assert HK == LANES and GK == LANES and HV % LANES == 0 and GV % LANES == 0
    assert HH * HV == HW and 2 * GKW == HW and GVW == HW and R <= LANES
    assert x_sample.shape[1] == 1 and L % CHUNK == 0 and D % HW == 0
    mixer_cols = 4 * HW + 2 * GKW + 2 * GVW

    assert w_in.shape[2] - mixer_cols >= LANES and (mixer_cols + R) % 8 == 0
    w_in_t = jnp.swapaxes(w_in, 1, 2)
    whu = w_hgrn_up.astype(bf16)
    wgu = w_gla_up.astype(bf16)
    wout = w_out.astype(bf16)
    wpg = w_ple_gate.astype(bf16)
    wp = w_ple.astype(bf16)
    wgg = jnp.concatenate([w_gla_gate.astype(f32), jnp.zeros((depth, LANES - R, GKW), f32)], axis=1)

    lbs = jnp.cumsum(jax.nn.softmax(lb_param.astype(f32), axis=0), axis=0)
    lbs = (lbs - lbs[0:1]).reshape(depth, 1, HW)
    r3 = lambda t: t.astype(f32).reshape(depth, 1, t.shape[-1])
    n_pre_mix, n_post_mix, n_pre_ffn, n_post_ffn = map(
        r3, (norm_pre_mix, norm_post_mix, norm_pre_ffn, norm_post_ffn))
    hn, gn, bgg = r3(hgrn_norm), r3(gla_norm), r3(b_gla_gate)
    mall, lvl = _level_tables()
    seg_w = (HW, HW, HW, GKW, GKW, GVW, GVW)
    silu_flag = jnp.asarray(np.concatenate(
        [np.full((1, w), v, np.float32) for w, v in zip(seg_w, (1, 0, 1, 0, 0, 0, 1))], axis=1))
    col_scale = jnp.asarray(np.concatenate(
        [np.full((1, w), v, np.float32) for w, v in zip(seg_w, (1, 1, 1, GK ** -0.5, 1, 1, 1))],
        axis=1))
    tm_mg = _pick(D, (1024, 512, 256, 128))
    tf = _pick(F, (1024, 512, 256, 128))
    tn2 = _pick(D, (512, 256, 128))

    def in_projections(a):
        col_spec = pl.BlockSpec((1, HW), lambda n, m: (0, n))
        zb, = _proj(a, w_in_t, i, lambda n: (n + jnp.minimum(n, 1)) * HW, 6, HW, _ep_main,
                    (silu_flag, col_scale), (col_spec, col_spec), (bf16,), "in_proj", True,
                    tm_cap=2048)
        ld, kh = _proj(a, w_in_t, i, lambda n: HW, 1, HW, _ep_forget, (lbs,),
                       (pl.BlockSpec((None, 1, HW), lambda n, m: (i, 0, 0)),),
                       (f32, f32), "in_proj_forget", True, tm_cap=512)
        zm, = _proj(a, w_in_t, i, lambda n: mixer_cols + R + n * tm_mg, 2 * D // tm_mg, tm_mg,
                    _ep_sigmoid, (), (), (bf16,), "in_proj_merge", True, tm_cap=2048)
        glr, = _proj(a, w_in_t, i, lambda n: mixer_cols, 1, LANES, _ep_copy, (), (), (f32,),
                     "in_proj_lowrank", True, tm_cap=2048)
        return zb, ld, kh, zm, glr

    def dense_tail(h, oh, og, zm, p):
        h1, c = _postmix(oh, og, zm, h, whu, wgu, wout, n_post_mix, n_pre_ffn, i)
        u, = _proj(c, w_ff1, i, lambda n: n, F // tf, tf, _ep_relu2, (), (), (bf16,),
                   "ffn_up", False, tm_cap=2048)
        t2, = _proj(u, w_ff2, i, lambda n: n, D // tn2, tn2, _ep_copy, (), (), (f32,),
                    "ffn_down", False, tm_cap=832, w_single=True)
        return _ple(t2, h1, p, wpg, wp, n_post_ffn, n_pre_mix, i, (i + 1) % depth)

    BL = B * L
    assert BL % 16 == 0 and Bs % 16 == 0
    h = jnp.concatenate([x_prompt.reshape(BL, D), x_sample.reshape(Bs, D)], axis=0)
    p = jnp.concatenate([p_prompt.reshape(depth, BL, -1), p_sample.reshape(depth, Bs, -1)], axis=1)
    a = _norm(h, n_pre_mix, 0)

    hgrn_p, gla_p = [], []
    hgrn_s = gla_s = None
    for i in range(depth):
        zb, ld, kh, zm, glr = in_projections(a)
        oh, sh = _hgrn_prompt(zb, ld, kh, hn, mall, lvl, i, B, L, HH, HK, HV)
        oh, hgrn_s = _hgrn_step(zb, ld, kh, hn, state_hgrn, oh, hgrn_s, i, BL, HH, HK, HV)
        og, sg = _gla_prompt(zb, glr, wgg, bgg, gn, mall, lvl, i, B, L, GH, GK, GV)
        og, gla_s = _gla_step(zb, glr, wgg, bgg, gn, state_gla, og, gla_s, i, BL, GH, GK, GV)
        h, a = dense_tail(h, oh, og, zm, p)
        hgrn_p.append(sh)
        gla_p.append(sg)

    return (h[:BL].reshape(B, L, D), h[BL:].reshape(Bs, 1, D),
            jnp.stack(hgrn_p).astype(state_hgrn.dtype), jnp.stack(gla_p).astype(state_gla.dtype),
            hgrn_s.astype(state_hgrn.dtype), gla_s.astype(state_gla.dtype))
```

```python
import functools

import numpy as np
import jax
import jax.numpy as jnp
from jax import lax
from jax.experimental import pallas as pl
from jax.experimental.pallas import tpu as pltpu

EPS = 1e-6
GLA_GATE_NORM = 16.0
LOG2E = 1.4426950408889634
LANES = 128
CHUNK = 64
N_LEVELS = 6
VMEM_LIMIT = 56 * 1024 * 1024

f32 = jnp.float32
bf16 = jnp.bfloat16


def _sigmoid(x):
    return 0.5 * jnp.tanh(0.5 * x) + 0.5


def _silu(x):
    return x * _sigmoid(x)


def _log_sigmoid(x):
    return jnp.minimum(x, 0.0) - jnp.log(1.0 + jnp.exp(-jnp.abs(x)))


def _rms(x, g):
    return x * lax.rsqrt(jnp.mean(x * x, axis=-1, keepdims=True) + EPS) * g


def _dot(a, b):
    return jnp.dot(a, b, preferred_element_type=f32)


def _dot_nt(a, b):
    return lax.dot_general(a, b, (((1,), (1,)), ((), ())), preferred_element_type=f32)


def _dot_tn(a, b):
    return lax.dot_general(a, b, (((0,), (0,)), ((), ())), preferred_element_type=f32)


def _split2(x):
    hi = x.astype(bf16)
    lo = (x - hi.astype(f32)).astype(bf16)
    return hi, lo


def _params(sem):
    return pltpu.CompilerParams(dimension_semantics=sem, vmem_limit_bytes=VMEM_LIMIT)


def _const_spec(shape, index_map):
    return pl.BlockSpec(shape, index_map, pipeline_mode=pl.Buffered(1))


def _pick(n, cands):
    for c in cands:
        if n % c == 0:
            return c
    return n


def _row_tile(m, cap):
    for t in range(min(cap, m), 0, -1):
        if m % t == 0 and t % 16 == 0:
            return t
    return m


def _level_tables():
    C = CHUNK
    t = np.arange(C)[:, None]
    j = np.arange(C)[None, :]
    mats = [(j <= t), (j > t)]
    lvl = np.full((C, C), -1, np.int32)
    lvl[t == j] = 0
    w = C // 2
    level = 1
    while w >= 1:
        start = (t // (2 * w)) * (2 * w)
        m = start + w - 1
        second = (t - start) >= w
        mats.append(np.where(second, (j > m) & (j <= t), (j > t) & (j <= m)))
        same = (t // (2 * w)) == (j // (2 * w))
        lvl[same & second & ((j - (j // (2 * w)) * (2 * w)) < w)] = level
        w //= 2
        level += 1
    mall = np.concatenate(mats[:-1], axis=0).astype(np.float32)
    return jnp.asarray(np.concatenate([mall, mall], axis=1), bf16), jnp.asarray(lvl)


def _norm_kernel(x_ref, g_ref, o_ref):
    o_ref[...] = _rms(x_ref[...], g_ref[...]).astype(o_ref.dtype)


def _norm(x, gains, layer):
    M, D = x.shape
    tm = _row_tile(M, 512)
    return pl.pallas_call(
        _norm_kernel,
        grid=(M // tm,),
        in_specs=[pl.BlockSpec((tm, D), lambda m: (m, 0)),
                  pl.BlockSpec((None, 1, D), lambda m: (layer, 0, 0))],
        out_specs=pl.BlockSpec((tm, D), lambda m: (m, 0)),
        out_shape=jax.ShapeDtypeStruct((M, D), bf16),
        compiler_params=_params(("parallel",)),
        name="rmsnorm",
    )(x, gains)


def _proj_kernel(*refs, n_extra, epilogue, transposed, tn, slab):
    a_ref, w_ref = refs[0], refs[1]
    extra = refs[2:2 + n_extra]
    outs = refs[2 + n_extra:-1]
    wbf_ref = refs[-1]

    @pl.when(pl.program_id(1) == 0)
    def _():
        w = w_ref[0] if transposed else w_ref[...]
        wbf_ref[...] = w.astype(bf16)

    def body(ep):
        a = a_ref[...]
        for j in range(tn // slab):
            cols = slice(j * slab, (j + 1) * slab)
            if transposed:
                acc = _dot_nt(a, wbf_ref[cols, :])
            else:
                acc = _dot(a, wbf_ref[:, cols])
            ep(acc, cols, extra, outs)

    if isinstance(epilogue, tuple):
        pred_fn, ep_true, ep_false = epilogue
        pred = pred_fn(pl.program_id(0))
        pl.when(pred)(functools.partial(body, ep_true))
        pl.when(jnp.logical_not(pred))(functools.partial(body, ep_false))
    else:
        body(epilogue)


def _proj(a, w, layer, w_off, n_tiles, tn, epilogue, extra, extra_specs, out_dtypes, name,
          transposed, tm_cap=1024, w_single=False):
    M, K = a.shape
    tm = _row_tile(M, tm_cap)
    if transposed:
        w_shape = (pl.Element(1), pl.Element(tn), pl.Element(K))
        def w_map(n, m):
            off = w_off(n)
            return (layer, off if isinstance(off, int) else pl.multiple_of(off, 8), 0)
        scratch = pltpu.VMEM((tn, K), bf16)
    else:
        w_shape = (None, K, tn)
        w_map = lambda n, m: (layer, 0, w_off(n))
        scratch = pltpu.VMEM((K, tn), bf16)
    w_spec = _const_spec(w_shape, w_map) if w_single else pl.BlockSpec(w_shape, w_map)
    kern = functools.partial(_proj_kernel, n_extra=len(extra), epilogue=epilogue,
                             transposed=transposed, tn=tn, slab=min(tn, 2 * LANES))
    return pl.pallas_call(
        kern,
        grid=(n_tiles, M // tm),
        in_specs=[pl.BlockSpec((tm, K), lambda n, m: (m, 0)), w_spec] + list(extra_specs),
        out_specs=[pl.BlockSpec((tm, tn), lambda n, m: (m, n)) for _ in out_dtypes],
        out_shape=[jax.ShapeDtypeStruct((M, n_tiles * tn), dt) for dt in out_dtypes],
        scratch_shapes=[scratch],
        compiler_params=_params(("parallel", "arbitrary")),
        name=name,
    )(a, w, *extra)


def _hgrn_gates(fl, lb):
    fl2 = fl * LOG2E
    e = jnp.exp2(-jnp.abs(fl2))
    t = 1.0 + e
    pos = fl >= 0.0
    num = jnp.where(pos, 1.0 + lb * e, lb + e)
    log2_num = jnp.where(pos | (lb > 0.0), jnp.log2(num), fl2)
    log2_f = log2_num - jnp.log2(t)
    r = 1.0 / t
    k = (1.0 - lb) * jnp.where(pos, e * r, r)
    return log2_f, k


def _ep_silu(acc, cols, extra, outs):
    outs[0][:, cols] = _silu(acc).astype(outs[0].dtype)


def _ep_scale(acc, cols, extra, outs):
    outs[0][:, cols] = (acc * extra[0][:, cols]).astype(outs[0].dtype)


def _ep_forget(acc, cols, extra, outs):
    ld_ref, k_ref = outs
    log2_f, k = _hgrn_gates(acc, extra[0][:, cols])
    ld_ref[:, cols] = log2_f
    k_ref[:, cols] = k


def _ep_sigmoid(acc, cols, extra, outs):
    outs[0][:, cols] = _sigmoid(acc).astype(outs[0].dtype)


def _ep_copy(acc, cols, extra, outs):
    outs[0][:, cols] = acc.astype(outs[0].dtype)


def _ep_relu2(acc, cols, extra, outs):
    outs[0][:, cols] = jnp.square(jnp.maximum(acc, 0.0)).astype(outs[0].dtype)


def _recurrence_chunks(T, H, K, V, q_ref, load_k, ld_ref, v_ref, g_ref, gain,
                       mall_ref, lvl_ref, st_ref, e_ref, o_ref):
    C = CHUNK

    def chunk(c, carry):
        r0 = pl.multiple_of(c * C, C)
        rows = pl.ds(r0, C)
        lvl = lvl_ref[...]
        hi, lo = _split2(ld_ref[rows, :])
        e_ref[...] = jnp.exp2(_dot(mall_ref[...], jnp.concatenate([hi, lo], axis=0)))
        heads = []
        for h in range(H):
            cs = slice(h * K, (h + 1) * K)
            vs = slice(h * V, (h + 1) * V)
            qc = q_ref[rows, cs].astype(f32)
            kc = load_k(rows, cs)
            vb = v_ref[rows, vs]
            st = st_ref[h]

            o = _dot_nt((qc * e_ref[0:C, cs]).astype(bf16), st.astype(bf16))

            diag = jnp.sum(qc * kc, axis=-1, keepdims=True)
            adj = jnp.sum(qc * pltpu.roll(kc, 1, axis=0) * jnp.exp2(ld_ref[rows, cs]),
                          axis=-1, keepdims=True)
            levels = []
            for l in range(1, N_LEVELS):
                el = e_ref[(l + 1) * C:(l + 2) * C, cs]
                levels.append(_dot_nt((qc * el).astype(bf16), (kc * el).astype(bf16)))

            kd = (kc * e_ref[C:2 * C, cs]).astype(bf16)
            alpha = e_ref[C - 1:C, cs]
            heads.append((h, vs, vb, st, kd, alpha, o, diag, adj, levels))

        outs = []
        for h, vs, vb, st, kd, alpha, o, diag, adj, levels in heads:
            a = jnp.where(lvl == 0, diag, jnp.where(lvl == N_LEVELS, adj, 0.0))
            for l, al in enumerate(levels, start=1):
                a = jnp.where(lvl == l, al, a)
            outs.append(o + _dot(a.astype(bf16), vb))

        for h, vs, vb, st, kd, alpha, o, diag, adj, levels in heads:
            st_ref[h] = st * alpha + _dot_tn(vb, kd)

        for (h, vs, *_), o in zip(heads, outs):
            gate = g_ref[rows, vs].astype(f32)
            o_ref[rows, vs] = (_rms(o, gain) * gate).astype(o_ref.dtype)
        return carry

    lax.fori_loop(0, T // C, chunk, 0)


def _write_state(tb, st_ref, s_ref, H):
    @pl.when(tb == pl.num_programs(1) - 1)
    def _():
        for h in range(H):
            s_ref[0, h] = st_ref[h].T


def _hgrn_prompt_kernel(zq_ref, zi_ref, zg_ref, ld_ref, k_ref, gain_ref, mall_ref,
                        lvl_ref, o_ref, s_ref, st_ref, e_ref, *, T, H, K, V):
    tb = pl.program_id(1)

    @pl.when(tb == 0)
    def _():
        st_ref[...] = jnp.zeros_like(st_ref)

    _recurrence_chunks(T, H, K, V, zq_ref, lambda rows, cs: k_ref[rows, cs], ld_ref,
                       zi_ref, zg_ref, gain_ref[...], mall_ref, lvl_ref, st_ref, e_ref, o_ref)
    _write_state(tb, st_ref, s_ref, H)


def _gla_log2_decay(glr, wg, bg):
    g_hi, g_lo = _split2(glr)
    w_hi, w_lo = _split2(wg)
    pre = _dot(g_hi, w_hi) + _dot(g_hi, w_lo) + _dot(g_lo, w_hi) + bg
    return _log_sigmoid(pre) * (LOG2E / GLA_GATE_NORM)


def _gla_prompt_kernel(gqk_ref, gv_ref, gr_ref, glr_ref, wg_ref, bg_ref, gain_ref, mall_ref,
                       lvl_ref, o_ref, s_ref, st_ref, ld_ref, e_ref, *, T, H, K, V):
    tb = pl.program_id(1)

    @pl.when(tb == 0)
    def _():
        st_ref[...] = jnp.zeros_like(st_ref)

    ld_ref[...] = _gla_log2_decay(glr_ref[...], wg_ref[...], bg_ref[...])
    KW = H * K
    k_ref = gqk_ref.at[:, KW:2 * KW]
    _recurrence_chunks(T, H, K, V, gqk_ref, lambda rows, cs: k_ref[rows, cs].astype(f32),
                       ld_ref, gv_ref, gr_ref, gain_ref[...], mall_ref, lvl_ref,
                       st_ref, e_ref, o_ref)
    _write_state(tb, st_ref, s_ref, H)


def _hgrn_prompt(zb, ld, kh, hgrn_norm, mall, lvl, layer, B, L, H, K, V):
    HW = H * K
    T = min(1024, L)
    nT = L // T
    tok = lambda col: (lambda b, t: (b * nT + t, col))
    cst = lambda b, t: (0, 0)
    kern = functools.partial(_hgrn_prompt_kernel, T=T, H=H, K=K, V=V)
    return pl.pallas_call(
        kern,
        grid=(B, nT),
        in_specs=[pl.BlockSpec((T, HW), tok(0)),
                  pl.BlockSpec((T, HW), tok(1)),
                  pl.BlockSpec((T, HW), tok(2)),
                  pl.BlockSpec((T, HW), tok(0)),
                  pl.BlockSpec((T, HW), tok(0)),
                  pl.BlockSpec((None, 1, V), lambda b, t: (layer, 0, 0)),
                  pl.BlockSpec(mall.shape, cst),
                  pl.BlockSpec(lvl.shape, cst)],
        out_specs=[pl.BlockSpec((T, HW), tok(0)),
                   pl.BlockSpec((1, H, K, V), lambda b, t: (b, 0, 0, 0))],
        out_shape=[jax.ShapeDtypeStruct((zb.shape[0], H * V), bf16),
                   jax.ShapeDtypeStruct((B, H, K, V), f32)],
        scratch_shapes=[pltpu.VMEM((H, V, K), f32),
                        pltpu.VMEM((mall.shape[0], HW), f32)],
        compiler_params=_params(("parallel", "arbitrary")),
        name="hgrn_prompt",
    )(zb, zb, zb, ld, kh, hgrn_norm, mall, lvl)


def _gla_prompt(zb, glr, wgg, bgg, gla_norm, mall, lvl, layer, B, L, H, K, V):
    KW, VW = H * K, H * V
    T = min(1024, L)
    nT = L // T
    tok = lambda col: (lambda b, t: (b * nT + t, col))
    cst = lambda b, t: (0, 0)
    kern = functools.partial(_gla_prompt_kernel, T=T, H=H, K=K, V=V)
    return pl.pallas_call(
        kern,
        grid=(B, nT),
        in_specs=[pl.BlockSpec((T, 2 * KW), tok(3)),
                  pl.BlockSpec((T, VW), tok(4)),
                  pl.BlockSpec((T, VW), tok(5)),
                  pl.BlockSpec((T, LANES), tok(0)),
                  pl.BlockSpec((None, LANES, KW), lambda b, t: (layer, 0, 0)),
                  pl.BlockSpec((None, 1, KW), lambda b, t: (layer, 0, 0)),
                  pl.BlockSpec((None, 1, V), lambda b, t: (layer, 0, 0)),
                  pl.BlockSpec(mall.shape, cst),
                  pl.BlockSpec(lvl.shape, cst)],
        out_specs=[pl.BlockSpec((T, VW), tok(0)),
                   pl.BlockSpec((1, H, K, V), lambda b, t: (b, 0, 0, 0))],
        out_shape=[jax.ShapeDtypeStruct((zb.shape[0], VW), bf16),
                   jax.ShapeDtypeStruct((B, H, K, V), f32)],
        scratch_shapes=[pltpu.VMEM((H, V, K), f32),
                        pltpu.VMEM((T, KW), f32),
                        pltpu.VMEM((mall.shape[0], KW), f32)],
        compiler_params=_params(("parallel", "arbitrary")),
        name="gla_prompt",
    )(zb, zb, zb, glr, wgg, bgg, gla_norm, mall, lvl)


def _step_rows(Bt, H, K, V, a_ref, k_ref, qa_ref, v_ref, s_in_ref, s_out_ref, oi_ref):
    r = lax.broadcasted_iota(jnp.int32, (K, K), 0)
    c = lax.broadcasted_iota(jnp.int32, (K, K), 1)
    eye = r == c
    ones = jnp.ones((K, LANES), bf16)
    nv = V // LANES

    def diag(x):
        return jnp.where(eye, jnp.broadcast_to(x, (K, K)), 0.0)

    def body(b, carry):
        pieces = []
        for h in range(H):
            cs = slice(h * K, (h + 1) * K)
            a = a_ref[b, :, cs]
            a_hi = a.astype(bf16).astype(f32)
            pieces += [diag(a_hi), diag(a - a_hi), diag(k_ref[b, :, cs]), diag(qa_ref[b, :, cs])]
        cb = _dot(jnp.concatenate(pieces, axis=0).astype(bf16), ones)
        for h in range(H):
            r0 = 4 * h * K
            a_col = cb[r0:r0 + K] + cb[r0 + K:r0 + 2 * K]
            k_col, q_col = cb[r0 + 2 * K:r0 + 3 * K], cb[r0 + 3 * K:r0 + 4 * K]
            for j in range(nv):
                ls = slice(h * V + j * LANES, h * V + (j + 1) * LANES)
                vj = slice(j * LANES, (j + 1) * LANES)
                s = s_in_ref[b, h, :, vj]
                s_out_ref[b, h, :, vj] = a_col * s + k_col * v_ref[b, :, ls]
                oi_ref[b, :, ls] = jnp.sum(q_col * s, axis=0, keepdims=True)
        return carry

    lax.fori_loop(0, Bt, body, 0)


def _store_rows(ref, x):
    for b in range(x.shape[0]):
        ref[b] = x[b:b + 1, :]


def _step_finish(H, K, V, q, k, v, g, gain, oi_ref, o_ref):
    ones = jnp.ones((K, LANES), bf16)
    nv = V // LANES
    for h in range(H):
        cs = slice(h * K, (h + 1) * K)
        vs = slice(h * V, (h + 1) * V)
        qk = _dot((q[:, cs] * k[:, cs]).astype(bf16), ones)
        if nv > 1:
            qk = jnp.concatenate([qk] * nv, axis=1)
        oi = jnp.concatenate([oi_ref[b, :, vs] for b in range(q.shape[0])], axis=0)
        o = qk * v[:, vs] + oi
        o_ref[:, vs] = (_rms(o, gain) * g[:, vs]).astype(o_ref.dtype)


def _step_common(Bt, H, K, V, a, q, k, v, g, gain, a_ref, k_ref, qa_ref, v_ref, oi_ref,
                 s_in_ref, s_out_ref, o_ref):
    _store_rows(a_ref, a)
    _store_rows(k_ref, k)
    _store_rows(qa_ref, q * a)
    _store_rows(v_ref, v)
    _step_rows(Bt, H, K, V, a_ref, k_ref, qa_ref, v_ref, s_in_ref, s_out_ref, oi_ref)
    _step_finish(H, K, V, q, k, v, g, gain, oi_ref, o_ref)


def _hgrn_step_kernel(*refs, Bt, H, K, V, aliased):
    (zq_ref, zi_ref, zg_ref, ld_ref, kh_ref, gain_ref, s_in_ref) = refs[:7]
    o_ref, s_out_ref, a_ref, k_ref, qa_ref, v_ref, oi_ref = refs[7 + aliased:]
    a = jnp.exp2(ld_ref[...])
    _step_common(Bt, H, K, V, a, zq_ref[...].astype(f32), kh_ref[...], zi_ref[...].astype(f32),
                 zg_ref[...].astype(f32), gain_ref[...], a_ref, k_ref, qa_ref, v_ref, oi_ref,
                 s_in_ref, s_out_ref, o_ref)


def _gla_step_kernel(*refs, Bt, H, K, V, aliased):
    (gqk_ref, gv_ref, gr_ref, glr_ref, wg_ref, bg_ref, gain_ref, s_in_ref) = refs[:8]
    o_ref, s_out_ref, a_ref, k_ref, qa_ref, v_ref, oi_ref = refs[8 + aliased:]
    KW = H * K
    a = jnp.exp2(_gla_log2_decay(glr_ref[...], wg_ref[...], bg_ref[...]))
    _step_common(Bt, H, K, V, a, gqk_ref[:, :KW].astype(f32), gqk_ref[:, KW:].astype(f32),
                 gv_ref[...].astype(f32), gr_ref[...].astype(f32), gain_ref[...],
                 a_ref, k_ref, qa_ref, v_ref, oi_ref, s_in_ref, s_out_ref, o_ref)


def _step_call(kern, name, ins, in_specs, state, o_prev, s_prev, layer, row0, Bt, H, K, V, KW):
    depth, Bs = state.shape[:2]
    VW = H * V
    blk0 = row0 // Bt
    in_specs = list(in_specs) + [pl.BlockSpec((None, Bt, H, K, V), lambda i: (layer, i, 0, 0, 0)),
                                 pl.BlockSpec(memory_space=pl.ANY)]
    args = list(ins) + [state, o_prev]
    aliases = {len(args) - 1: 0}
    if s_prev is not None:
        in_specs.append(pl.BlockSpec(memory_space=pl.ANY))
        args.append(s_prev)
        aliases[len(args) - 1] = 1
    return pl.pallas_call(
        functools.partial(kern, Bt=Bt, H=H, K=K, V=V, aliased=len(aliases)),
        grid=(Bs // Bt,),
        in_specs=in_specs,
        out_specs=[pl.BlockSpec((Bt, VW), lambda i: (blk0 + i, 0)),
                   pl.BlockSpec((None, Bt, H, K, V), lambda i: (layer, i, 0, 0, 0))],
        out_shape=[jax.ShapeDtypeStruct(o_prev.shape, bf16),
                   jax.ShapeDtypeStruct((depth, Bs, H, K, V), f32)],
        scratch_shapes=[pltpu.VMEM((Bt, 1, KW), f32)] * 3 + [pltpu.VMEM((Bt, 1, VW), f32)] * 2,
        input_output_aliases=aliases,
        compiler_params=_params(("arbitrary",)),
        name=name,
    )(*args)


def _hgrn_step(zb, ld, kh, hgrn_norm, state, o_prev, s_prev, layer, row0, H, K, V):
    HW = H * K
    Bt = min(16, state.shape[1])
    blk0 = row0 // Bt
    row = lambda col: (lambda i: (blk0 + i, col))
    in_specs = [pl.BlockSpec((Bt, HW), row(0)),
                pl.BlockSpec((Bt, HW), row(1)),
                pl.BlockSpec((Bt, HW), row(2)),
                pl.BlockSpec((Bt, HW), row(0)),
                pl.BlockSpec((Bt, HW), row(0)),
                pl.BlockSpec((None, 1, V), lambda i: (layer, 0, 0))]
    return _step_call(_hgrn_step_kernel, "hgrn_step", (zb, zb, zb, ld, kh, hgrn_norm),
                      in_specs, state, o_prev, s_prev, layer, row0, Bt, H, K, V, HW)


def _gla_step(zb, glr, wgg, bgg, gla_norm, state, o_prev, s_prev, layer, row0, H, K, V):
    KW, VW = H * K, H * V
    Bt = min(16, state.shape[1])
    blk0 = row0 // Bt
    row = lambda col: (lambda i: (blk0 + i, col))
    in_specs = [pl.BlockSpec((Bt, 2 * KW), row(3)),
                pl.BlockSpec((Bt, VW), row(4)),
                pl.BlockSpec((Bt, VW), row(5)),
                pl.BlockSpec((Bt, LANES), row(0)),
                pl.BlockSpec((None, LANES, KW), lambda i: (layer, 0, 0)),
                pl.BlockSpec((None, 1, KW), lambda i: (layer, 0, 0)),
                pl.BlockSpec((None, 1, V), lambda i: (layer, 0, 0))]
    return _step_call(_gla_step_kernel, "gla_step", (zb, zb, zb, glr, wgg, bgg, gla_norm),
                      in_specs, state, o_prev, s_prev, layer, row0, Bt, H, K, V, KW)


def _postmix_kernel(oh_ref, og_ref, mh_ref, mg_ref, h_ref, whu_ref, wgu_ref, wout_ref,
                    npost_ref, npre_ref, h1_ref, c_ref):
    yh = _dot(oh_ref[...], whu_ref[...])
    yg = _dot(og_ref[...], wgu_ref[...])
    merged = mh_ref[...].astype(f32) * yh + mg_ref[...].astype(f32) * yg
    t = _dot(merged.astype(bf16), wout_ref[...])
    h1 = h_ref[...] + _rms(t, npost_ref[...])
    h1_ref[...] = h1
    c_ref[...] = _rms(h1, npre_ref[...]).astype(c_ref.dtype)


def _postmix(oh, og, zm, h, whu, wgu, wout, npost, npre, layer):
    M, D = h.shape
    HW, VW = oh.shape[1], og.shape[1]
    tm = _row_tile(M, 320)
    lay = lambda m: (layer, 0, 0)
    return pl.pallas_call(
        _postmix_kernel,
        grid=(M // tm,),
        in_specs=[pl.BlockSpec((tm, HW), lambda m: (m, 0)),
                  pl.BlockSpec((tm, VW), lambda m: (m, 0)),
                  pl.BlockSpec((tm, D), lambda m: (m, 0)),
                  pl.BlockSpec((tm, D), lambda m: (m, 1)),
                  pl.BlockSpec((tm, D), lambda m: (m, 0)),
                  _const_spec((None, HW, D), lay),
                  _const_spec((None, VW, D), lay),
                  _const_spec((None, D, D), lay),
                  pl.BlockSpec((None, 1, D), lay),
                  pl.BlockSpec((None, 1, D), lay)],
        out_specs=[pl.BlockSpec((tm, D), lambda m: (m, 0)),
                   pl.BlockSpec((tm, D), lambda m: (m, 0))],
        out_shape=[jax.ShapeDtypeStruct((M, D), f32),
                   jax.ShapeDtypeStruct((M, D), bf16)],
        compiler_params=_params(("parallel",)),
        name="postmix",
    )(oh, og, zm, zm, h, whu, wgu, wout, npost, npre)


def _ple_kernel(t_ref, h_ref, p_ref, wg_ref, wp_ref, npost_ref, nnext_ref, h3_ref, a_ref):
    h = h_ref[...] + _rms(t_ref[...], npost_ref[...])
    gate = _sigmoid(_dot(h.astype(bf16), wg_ref[...]))
    pe = _dot(p_ref[...].astype(bf16), wp_ref[...])
    h3 = h + gate * pe
    h3_ref[...] = h3
    a_ref[...] = _rms(h3, nnext_ref[...]).astype(a_ref.dtype)


def _ple(t2, h1, p, wg, wp, npost, nnext, layer, next_layer):
    M, D = h1.shape
    P = p.shape[-1]
    tm = _row_tile(M, 512)
    lay = lambda m: (layer, 0, 0)
    return pl.pallas_call(
        _ple_kernel,
        grid=(M // tm,),
        in_specs=[pl.BlockSpec((tm, D), lambda m: (m, 0)),
                  pl.BlockSpec((tm, D), lambda m: (m, 0)),
                  pl.BlockSpec((None, tm, P), lambda m: (layer, m, 0)),
                  _const_spec((None, D, D), lay),
                  _const_spec((None, P, D), lay),
                  pl.BlockSpec((None, 1, D), lay),
                  pl.BlockSpec((None, 1, D), lambda m: (next_layer, 0, 0))],
        out_specs=[pl.BlockSpec((tm, D), lambda m: (m, 0)),
                   pl.BlockSpec((tm, D), lambda m: (m, 0))],
        out_shape=[jax.ShapeDtypeStruct((M, D), f32),
                   jax.ShapeDtypeStruct((M, D), bf16)],
        compiler_params=_params(("parallel",)),
        name="ple",
    )(t2, h1, p, wg, wp, npost, nnext)


def kernel(x_prompt, x_sample, p_prompt, p_sample, state_hgrn, state_gla, norm_pre_mix,
           norm_post_mix, norm_pre_ffn, norm_post_ffn, w_in, lb_param, hgrn_norm, w_hgrn_up,
           w_gla_gate, b_gla_gate, gla_norm, w_gla_up, w_out, w_ff1, w_ff2, w_ple, w_ple_gate):
    B, L, D = x_prompt.shape
    Bs = x_sample.shape[0]
    depth = w_in.shape[0]
    _, _, HH, HK, HV = state_hgrn.shape
    _, _, GH, GK, GV = state_gla.shape
    HW, GKW, GVW = HH * HK, GH * GK, GH * GV
    R = w_gla_gate.shape[1]
    F = w_ff1.shape[2]
    assert HK == LANES and GK == LANES and HV % LANES == 0 and GV % LANES == 0
    assert HH * HV == HW and 2 * GKW == HW and GVW == HW and R <= LANES
    assert x_sample.shape[1] == 1 and L % CHUNK == 0 and D % HW == 0
    mixer_cols = 4 * HW + 2 * GKW + 2 * GVW

    assert w_in.shape[2] - mixer_cols >= LANES and (mixer_cols + R) % 8 == 0
    w_in_t = jnp.swapaxes(w_in, 1, 2)
    whu = w_hgrn_up.astype(bf16)
    wgu = w_gla_up.astype(bf16)
    wout = w_out.astype(bf16)
    wpg = w_ple_gate.astype(bf16)
    wp = w_ple.astype(bf16)
    wgg = jnp.concatenate([w_gla_gate.astype(f32), jnp.zeros((depth, LANES - R, GKW), f32)], axis=1)

    lbs = jnp.cumsum(jax.nn.softmax(lb_param.astype(f32), axis=0), axis=0)
    lbs = (lbs - lbs[0:1]).reshape(depth, 1, HW)
    r3 = lambda t: t.astype(f32).reshape(depth, 1, t.shape[-1])
    n_pre_mix, n_post_mix, n_pre_ffn, n_post_ffn = map(
        r3, (norm_pre_mix, norm_post_mix, norm_pre_ffn, norm_post_ffn))
    hn, gn, bgg = r3(hgrn_norm), r3(gla_norm), r3(b_gla_gate)
    mall, lvl = _level_tables()
    seg_w = (HW, HW, HW, GKW, GKW, GVW, GVW)
    col_scale = jnp.asarray(np.concatenate(
        [np.full((1, w), v, np.float32) for w, v in zip(seg_w, (1, 1, 1, GK ** -0.5, 1, 1, 1))],
        axis=1))
    ep_main = (lambda n: (n == 0) | (n == 2) | (n == 5), _ep_silu, _ep_scale)
    tm_mg = _pick(D, (1024, 512, 256, 128))
    tf = _pick(F, (1024, 512, 256, 128))
    tn2 = _pick(D, (512, 256, 128))

    def in_projections(a):
        col_spec = pl.BlockSpec((1, HW), lambda n, m: (0, n))
        zb, = _proj(a, w_in_t, i, lambda n: (n + jnp.minimum(n, 1)) * HW, 6, HW, ep_main,
                    (col_scale,), (col_spec,), (bf16,), "in_proj", True, tm_cap=2048)
        ld, kh = _proj(a, w_in_t, i, lambda n: HW, 1, HW, _ep_forget, (lbs,),
                       (pl.BlockSpec((None, 1, HW), lambda n, m: (i, 0, 0)),),
                       (f32, f32), "in_proj_forget", True, tm_cap=512)
        zm, = _proj(a, w_in_t, i, lambda n: mixer_cols + R + n * tm_mg, 2 * D // tm_mg, tm_mg,
                    _ep_sigmoid, (), (), (bf16,), "in_proj_merge", True, tm_cap=2048)
        glr, = _proj(a, w_in_t, i, lambda n: mixer_cols, 1, LANES, _ep_copy, (), (), (f32,),
                     "in_proj_lowrank", True, tm_cap=2048)
        return zb, ld, kh, zm, glr

    def dense_tail(h, oh, og, zm, p):
        h1, c = _postmix(oh, og, zm, h, whu, wgu, wout, n_post_mix, n_pre_ffn, i)
        u, = _proj(c, w_ff1, i, lambda n: n, F // tf, tf, _ep_relu2, (), (), (bf16,),
                   "ffn_up", False, tm_cap=2048)
        t2, = _proj(u, w_ff2, i, lambda n: n, D // tn2, tn2, _ep_copy, (), (), (f32,),
                    "ffn_down", False, tm_cap=832, w_single=True)
        return _ple(t2, h1, p, wpg, wp, n_post_ffn, n_pre_mix, i, (i + 1) % depth)

    BL = B * L
    assert BL % 16 == 0 and Bs % 16 == 0
    h = jnp.concatenate([x_prompt.reshape(BL, D), x_sample.reshape(Bs, D)], axis=0)
    p = jnp.concatenate([p_prompt.reshape(depth, BL, -1), p_sample.reshape(depth, Bs, -1)], axis=1)
    a = _norm(h, n_pre_mix, 0)

    hgrn_p, gla_p = [], []
    hgrn_s = gla_s = None
    for i in range(depth):
        zb, ld, kh, zm, glr = in_projections(a)
        oh, sh = _hgrn_prompt(zb, ld, kh, hn, mall, lvl, i, B, L, HH, HK, HV)
        oh, hgrn_s = _hgrn_step(zb, ld, kh, hn, state_hgrn, oh, hgrn_s, i, BL, HH, HK, HV)
        og, sg = _gla_prompt(zb, glr, wgg, bgg, gn, mall, lvl, i, B, L, GH, GK, GV)
        og, gla_s = _gla_step(zb, glr, wgg, bgg, gn, state_gla, og, gla_s, i, BL, GH, GK, GV)
        h, a = dense_tail(h, oh, og, zm, p)
        hgrn_p.append(sh)
        gla_p.append(sg)

    return (h[:BL].reshape(B, L, D), h[BL:].reshape(Bs, 1, D),
            jnp.stack(hgrn_p).astype(state_hgrn.dtype), jnp.stack(gla_p).astype(state_gla.dtype),
            hgrn_s.astype(state_hgrn.dtype), gla_s.astype(state_gla.dtype))
```

```python
import functools

import numpy as np
import jax
import jax.numpy as jnp
from jax import lax
from jax.experimental import pallas as pl
from jax.experimental.pallas import tpu as pltpu

EPS = 1e-6
GLA_GATE_NORM = 16.0
LOG2E = 1.4426950408889634
LANES = 128
CHUNK = 64
N_LEVELS = 6
VMEM_LIMIT = 56 * 1024 * 1024
ROW_CAP = dict(norm=512, in_proj=2048, in_proj_forget=512, in_proj_merge=2048,
               in_proj_lowrank=2048, ffn_up=2048, ffn_down=832, postmix=320, ple=512)

f32 = jnp.float32
bf16 = jnp.bfloat16


def _sigmoid(x):
    return 0.5 * jnp.tanh(0.5 * x) + 0.5


def _silu(x):
    hx = 0.5 * x
    return hx + hx * jnp.tanh(hx)


def _log_sigmoid(x):
    return jnp.minimum(x, 0.0) - jnp.log(1.0 + jnp.exp(-jnp.abs(x)))


def _rms(x, g):
    return x * lax.rsqrt(jnp.mean(x * x, axis=-1, keepdims=True) + EPS) * g


def _dot(a, b):
    return jnp.dot(a, b, preferred_element_type=f32)


def _dot_nt(a, b):
    return lax.dot_general(a, b, (((1,), (1,)), ((), ())), preferred_element_type=f32)


def _dot_tn(a, b):
    return lax.dot_general(a, b, (((0,), (0,)), ((), ())), preferred_element_type=f32)


def _split2(x):
    hi = x.astype(bf16)
    lo = (x - hi.astype(f32)).astype(bf16)
    return hi, lo


def _params(sem):
    return pltpu.CompilerParams(dimension_semantics=sem, vmem_limit_bytes=VMEM_LIMIT)


def _const_spec(shape, index_map):
    return pl.BlockSpec(shape, index_map, pipeline_mode=pl.Buffered(1))


def _pick(n, cands):
    for c in cands:
        if n % c == 0:
            return c
    return n


def _row_tile(m, cap):
    for t in range(min(cap, m), 0, -1):
        if m % t == 0 and t % 16 == 0:
            return t
    return m


def _level_tables():
    C = CHUNK
    t = np.arange(C)[:, None]
    j = np.arange(C)[None, :]
    mats = [(j <= t), (j > t)]
    lvl = np.full((C, C), -1, np.int32)
    lvl[t == j] = 0
    w = C // 2
    level = 1
    while w >= 1:
        start = (t // (2 * w)) * (2 * w)
        m = start + w - 1
        second = (t - start) >= w
        mats.append(np.where(second, (j > m) & (j <= t), (j > t) & (j <= m)))
        same = (t // (2 * w)) == (j // (2 * w))
        lvl[same & second & ((j - (j // (2 * w)) * (2 * w)) < w)] = level
        w //= 2
        level += 1
    mall = np.concatenate(mats[:-1], axis=0).astype(np.float32)
    return jnp.asarray(np.concatenate([mall, mall], axis=1), bf16), jnp.asarray(lvl)


def _norm_kernel(x_ref, g_ref, o_ref):
    o_ref[...] = _rms(x_ref[...], g_ref[...]).astype(o_ref.dtype)


def _norm(x, gains, layer):
    M, D = x.shape
    tm = _row_tile(M, ROW_CAP["norm"])
    return pl.pallas_call(
        _norm_kernel,
        grid=(M // tm,),
        in_specs=[pl.BlockSpec((tm, D), lambda m: (m, 0)),
                  pl.BlockSpec((None, 1, D), lambda m: (layer, 0, 0))],
        out_specs=pl.BlockSpec((tm, D), lambda m: (m, 0)),
        out_shape=jax.ShapeDtypeStruct((M, D), bf16),
        compiler_params=_params(("parallel",)),
        name="rmsnorm",
    )(x, gains)


def _proj_kernel(*refs, n_extra, epilogue, transposed, tn, slab):
    a_ref, w_ref = refs[0], refs[1]
    extra = refs[2:2 + n_extra]
    outs = refs[2 + n_extra:-1]
    wbf_ref = refs[-1]

    @pl.when(pl.program_id(1) == 0)
    def _():
        w = w_ref[0] if transposed else w_ref[...]
        wbf_ref[...] = w.astype(bf16)

    def body(ep):
        a = a_ref[...]
        for j in range(tn // slab):
            cols = slice(j * slab, (j + 1) * slab)
            if transposed:
                acc = _dot_nt(a, wbf_ref[cols, :])
            else:
                acc = _dot(a, wbf_ref[:, cols])
            ep(acc, cols, extra, outs)

    if isinstance(epilogue, tuple):
        pred_fn, ep_true, ep_false = epilogue
        pred = pred_fn(pl.program_id(0))
        pl.when(pred)(functools.partial(body, ep_true))
        pl.when(jnp.logical_not(pred))(functools.partial(body, ep_false))
    else:
        body(epilogue)


def _proj(a, w, layer, w_off, n_tiles, tn, epilogue, extra, extra_specs, out_dtypes, name,
          transposed, w_single=False):
    M, K = a.shape
    tm = _row_tile(M, ROW_CAP[name])
    if transposed:
        w_shape = (pl.Element(1), pl.Element(tn), pl.Element(K))
        def w_map(n, m):
            off = w_off(n)
            return (layer, off if isinstance(off, int) else pl.multiple_of(off, 8), 0)
        scratch = pltpu.VMEM((tn, K), bf16)
    else:
        w_shape = (None, K, tn)
        w_map = lambda n, m: (layer, 0, w_off(n))
        scratch = pltpu.VMEM((K, tn), bf16)
    w_spec = _const_spec(w_shape, w_map) if w_single else pl.BlockSpec(w_shape, w_map)
    kern = functools.partial(_proj_kernel, n_extra=len(extra), epilogue=epilogue,
                             transposed=transposed, tn=tn, slab=min(tn, 2 * LANES))
    return pl.pallas_call(
        kern,
        grid=(n_tiles, M // tm),
        in_specs=[pl.BlockSpec((tm, K), lambda n, m: (m, 0)), w_spec] + list(extra_specs),
        out_specs=[pl.BlockSpec((tm, tn), lambda n, m: (m, n)) for _ in out_dtypes],
        out_shape=[jax.ShapeDtypeStruct((M, n_tiles * tn), dt) for dt in out_dtypes],
        scratch_shapes=[scratch],
        compiler_params=_params(("parallel", "arbitrary")),
        name=name,
    )(a, w, *extra)


def _hgrn_gates(fl, lb):
    fl2 = fl * LOG2E
    e = jnp.exp2(-jnp.abs(fl2))
    t = 1.0 + e
    pos = fl >= 0.0
    num = jnp.where(pos, 1.0 + lb * e, lb + e)
    log2_num = jnp.where(pos | (lb > 0.0), jnp.log2(num), fl2)
    log2_f = log2_num - jnp.log2(t)
    r = 1.0 / t
    k = (1.0 - lb) * jnp.where(pos, e * r, r)
    return log2_f, k


def _ep_silu(acc, cols, extra, outs):
    outs[0][:, cols] = _silu(acc).astype(outs[0].dtype)


def _ep_scale(acc, cols, extra, outs):
    outs[0][:, cols] = (acc * extra[0][:, cols]).astype(outs[0].dtype)


def _ep_forget(acc, cols, extra, outs):
    ld_ref, k_ref = outs
    log2_f, k = _hgrn_gates(acc, extra[0][:, cols])
    ld_ref[:, cols] = log2_f
    k_ref[:, cols] = k


def _ep_sigmoid(acc, cols, extra, outs):
    outs[0][:, cols] = _sigmoid(acc).astype(outs[0].dtype)


def _ep_copy(acc, cols, extra, outs):
    outs[0][:, cols] = acc.astype(outs[0].dtype)


def _ep_relu2(acc, cols, extra, outs):
    outs[0][:, cols] = jnp.square(jnp.maximum(acc, 0.0)).astype(outs[0].dtype)


def _recurrence_chunks(T, H, K, V, q_ref, load_k, ld_ref, v_ref, g_ref, gain,
                       mall_ref, lvl_ref, st_ref, e_ref, o_ref):
    C = CHUNK

    def chunk(c, carry):
        r0 = pl.multiple_of(c * C, C)
        rows = pl.ds(r0, C)
        lvl = lvl_ref[...]
        hi, lo = _split2(ld_ref[rows, :])
        e_ref[...] = jnp.exp2(_dot(mall_ref[...], jnp.concatenate([hi, lo], axis=0)))
        heads = []
        for h in range(H):
            cs = slice(h * K, (h + 1) * K)
            vs = slice(h * V, (h + 1) * V)
            qc = q_ref[rows, cs].astype(f32)
            kc = load_k(rows, cs)
            vb = v_ref[rows, vs]
            st = st_ref[h]

            o = _dot_nt((qc * e_ref[0:C, cs]).astype(bf16), st.astype(bf16))

            diag = jnp.sum(qc * kc, axis=-1, keepdims=True)
            adj = jnp.sum(qc * pltpu.roll(kc, 1, axis=0) * jnp.exp2(ld_ref[rows, cs]),
                          axis=-1, keepdims=True)
            levels = []
            for l in range(1, N_LEVELS):
                el = e_ref[(l + 1) * C:(l + 2) * C, cs]
                levels.append(_dot_nt((qc * el).astype(bf16), (kc * el).astype(bf16)))

            kd = (kc * e_ref[C:2 * C, cs]).astype(bf16)
            alpha = e_ref[C - 1:C, cs]
            heads.append((h, vs, vb, st, kd, alpha, o, diag, adj, levels))

        outs = []
        for h, vs, vb, st, kd, alpha, o, diag, adj, levels in heads:
            a = jnp.where(lvl == 0, diag, jnp.where(lvl == N_LEVELS, adj, 0.0))
            for l, al in enumerate(levels, start=1):
                a = jnp.where(lvl == l, al, a)
            outs.append(o + _dot(a.astype(bf16), vb))

        for h, vs, vb, st, kd, alpha, o, diag, adj, levels in heads:
            st_ref[h] = st * alpha + _dot_tn(vb, kd)

        for (h, vs, *_), o in zip(heads, outs):
            gate = g_ref[rows, vs].astype(f32)
            o_ref[rows, vs] = (_rms(o, gain) * gate).astype(o_ref.dtype)
        return carry

    lax.fori_loop(0, T // C, chunk, 0)


def _write_state(tb, st_ref, s_ref, H):
    @pl.when(tb == pl.num_programs(1) - 1)
    def _():
        for h in range(H):
            s_ref[0, h] = st_ref[h].T


def _hgrn_prompt_kernel(zq_ref, zi_ref, zg_ref, ld_ref, k_ref, gain_ref, mall_ref,
                        lvl_ref, o_ref, s_ref, st_ref, e_ref, *, T, H, K, V):
    tb = pl.program_id(1)

    @pl.when(tb == 0)
    def _():
        st_ref[...] = jnp.zeros_like(st_ref)

    _recurrence_chunks(T, H, K, V, zq_ref, lambda rows, cs: k_ref[rows, cs], ld_ref,
                       zi_ref, zg_ref, gain_ref[...], mall_ref, lvl_ref, st_ref, e_ref, o_ref)
    _write_state(tb, st_ref, s_ref, H)


def _gla_log2_decay(glr, wg, bg):
    g_hi, g_lo = _split2(glr)
    w_hi, w_lo = _split2(wg)
    pre = _dot(g_hi, w_hi) + _dot(g_hi, w_lo) + _dot(g_lo, w_hi) + bg
    return _log_sigmoid(pre) * (LOG2E / GLA_GATE_NORM)


def _gla_prompt_kernel(gqk_ref, gv_ref, gr_ref, glr_ref, wg_ref, bg_ref, gain_ref, mall_ref,
                       lvl_ref, o_ref, s_ref, st_ref, ld_ref, e_ref, *, T, H, K, V):
    tb = pl.program_id(1)

    @pl.when(tb == 0)
    def _():
        st_ref[...] = jnp.zeros_like(st_ref)

    ld_ref[...] = _gla_log2_decay(glr_ref[...], wg_ref[...], bg_ref[...])
    KW = H * K
    k_ref = gqk_ref.at[:, KW:2 * KW]
    _recurrence_chunks(T, H, K, V, gqk_ref, lambda rows, cs: k_ref[rows, cs].astype(f32),
                       ld_ref, gv_ref, gr_ref, gain_ref[...], mall_ref, lvl_ref,
                       st_ref, e_ref, o_ref)
    _write_state(tb, st_ref, s_ref, H)


def _hgrn_prompt(zb, ld, kh, hgrn_norm, mall, lvl, layer, B, L, H, K, V):
    HW = H * K
    T = min(1024, L)
    nT = L // T
    tok = lambda col: (lambda b, t: (b * nT + t, col))
    cst = lambda b, t: (0, 0)
    kern = functools.partial(_hgrn_prompt_kernel, T=T, H=H, K=K, V=V)
    return pl.pallas_call(
        kern,
        grid=(B, nT),
        in_specs=[pl.BlockSpec((T, HW), tok(0)),
                  pl.BlockSpec((T, HW), tok(1)),
                  pl.BlockSpec((T, HW), tok(2)),
                  pl.BlockSpec((T, HW), tok(0)),
                  pl.BlockSpec((T, HW), tok(0)),
                  pl.BlockSpec((None, 1, V), lambda b, t: (layer, 0, 0)),
                  pl.BlockSpec(mall.shape, cst),
                  pl.BlockSpec(lvl.shape, cst)],
        out_specs=[pl.BlockSpec((T, HW), tok(0)),
                   pl.BlockSpec((1, H, K, V), lambda b, t: (b, 0, 0, 0))],
        out_shape=[jax.ShapeDtypeStruct((zb.shape[0], H * V), bf16),
                   jax.ShapeDtypeStruct((B, H, K, V), f32)],
        scratch_shapes=[pltpu.VMEM((H, V, K), f32),
                        pltpu.VMEM((mall.shape[0], HW), f32)],
        compiler_params=_params(("parallel", "arbitrary")),
        name="hgrn_prompt",
    )(zb, zb, zb, ld, kh, hgrn_norm, mall, lvl)


def _gla_prompt(zb, glr, wgg, bgg, gla_norm, mall, lvl, layer, B, L, H, K, V):
    KW, VW = H * K, H * V
    T = min(1024, L)
    nT = L // T
    tok = lambda col: (lambda b, t: (b * nT + t, col))
    cst = lambda b, t: (0, 0)
    kern = functools.partial(_gla_prompt_kernel, T=T, H=H, K=K, V=V)
    return pl.pallas_call(
        kern,
        grid=(B, nT),
        in_specs=[pl.BlockSpec((T, 2 * KW), tok(3)),
                  pl.BlockSpec((T, VW), tok(4)),
                  pl.BlockSpec((T, VW), tok(5)),
                  pl.BlockSpec((T, LANES), tok(0)),
                  pl.BlockSpec((None, LANES, KW), lambda b, t: (layer, 0, 0)),
                  pl.BlockSpec((None, 1, KW), lambda b, t: (layer, 0, 0)),
                  pl.BlockSpec((None, 1, V), lambda b, t: (layer, 0, 0)),
                  pl.BlockSpec(mall.shape, cst),
                  pl.BlockSpec(lvl.shape, cst)],
        out_specs=[pl.BlockSpec((T, VW), tok(0)),
                   pl.BlockSpec((1, H, K, V), lambda b, t: (b, 0, 0, 0))],
        out_shape=[jax.ShapeDtypeStruct((zb.shape[0], VW), bf16),
                   jax.ShapeDtypeStruct((B, H, K, V), f32)],
        scratch_shapes=[pltpu.VMEM((H, V, K), f32),
                        pltpu.VMEM((T, KW), f32),
                        pltpu.VMEM((mall.shape[0], KW), f32)],
        compiler_params=_params(("parallel", "arbitrary")),
        name="gla_prompt",
    )(zb, zb, zb, glr, wgg, bgg, gla_norm, mall, lvl)


def _step_rows(Bt, H, K, V, a_ref, k_ref, qa_ref, v_ref, s_in_ref, s_out_ref, oi_ref):
    r = lax.broadcasted_iota(jnp.int32, (K, K), 0)
    c = lax.broadcasted_iota(jnp.int32, (K, K), 1)
    eye = r == c
    ones = jnp.ones((K, LANES), bf16)
    nv = V // LANES

    def diag(x):
        return jnp.where(eye, jnp.broadcast_to(x, (K, K)), 0.0)

    def body(b, carry):
        pieces = []
        for h in range(H):
            cs = slice(h * K, (h + 1) * K)
            a = a_ref[b, :, cs]
            a_hi = a.astype(bf16).astype(f32)
            pieces += [diag(a_hi), diag(a - a_hi), diag(k_ref[b, :, cs]), diag(qa_ref[b, :, cs])]
        cb = _dot(jnp.concatenate(pieces, axis=0).astype(bf16), ones)
        for h in range(H):
            r0 = 4 * h * K
            a_col = cb[r0:r0 + K] + cb[r0 + K:r0 + 2 * K]
            k_col, q_col = cb[r0 + 2 * K:r0 + 3 * K], cb[r0 + 3 * K:r0 + 4 * K]
            for j in range(nv):
                ls = slice(h * V + j * LANES, h * V + (j + 1) * LANES)
                vj = slice(j * LANES, (j + 1) * LANES)
                s = s_in_ref[b, h, :, vj]
                s_out_ref[b, h, :, vj] = a_col * s + k_col * v_ref[b, :, ls]
                oi_ref[b, :, ls] = jnp.sum(q_col * s, axis=0, keepdims=True)
        return carry

    lax.fori_loop(0, Bt, body, 0)


def _store_rows(ref, x):
    for b in range(x.shape[0]):
        ref[b] = x[b:b + 1, :]


def _step_finish(H, K, V, q, k, v, g, gain, oi_ref, o_ref):
    ones = jnp.ones((K, LANES), bf16)
    nv = V // LANES
    for h in range(H):
        cs = slice(h * K, (h + 1) * K)
        vs = slice(h * V, (h + 1) * V)
        qk = _dot((q[:, cs] * k[:, cs]).astype(bf16), ones)
        if nv > 1:
            qk = jnp.concatenate([qk] * nv, axis=1)
        oi = jnp.concatenate([oi_ref[b, :, vs] for b in range(q.shape[0])], axis=0)
        o = qk * v[:, vs] + oi
        o_ref[:, vs] = (_rms(o, gain) * g[:, vs]).astype(o_ref.dtype)


def _step_common(Bt, H, K, V, a, q, k, v, g, gain, a_ref, k_ref, qa_ref, v_ref, oi_ref,
                 s_in_ref, s_out_ref, o_ref):
    _store_rows(a_ref, a)
    _store_rows(k_ref, k)
    _store_rows(qa_ref, q * a)
    _store_rows(v_ref, v)
    _step_rows(Bt, H, K, V, a_ref, k_ref, qa_ref, v_ref, s_in_ref, s_out_ref, oi_ref)
    _step_finish(H, K, V, q, k, v, g, gain, oi_ref, o_ref)


def _hgrn_step_kernel(*refs, Bt, H, K, V, aliased):
    (zq_ref, zi_ref, zg_ref, ld_ref, kh_ref, gain_ref, s_in_ref) = refs[:7]
    o_ref, s_out_ref, a_ref, k_ref, qa_ref, v_ref, oi_ref = refs[7 + aliased:]
    a = jnp.exp2(ld_ref[...])
    _step_common(Bt, H, K, V, a, zq_ref[...].astype(f32), kh_ref[...], zi_ref[...].astype(f32),
                 zg_ref[...].astype(f32), gain_ref[...], a_ref, k_ref, qa_ref, v_ref, oi_ref,
                 s_in_ref, s_out_ref, o_ref)


def _gla_step_kernel(*refs, Bt, H, K, V, aliased):
    (gqk_ref, gv_ref, gr_ref, glr_ref, wg_ref, bg_ref, gain_ref, s_in_ref) = refs[:8]
    o_ref, s_out_ref, a_ref, k_ref, qa_ref, v_ref, oi_ref = refs[8 + aliased:]
    KW = H * K
    a = jnp.exp2(_gla_log2_decay(glr_ref[...], wg_ref[...], bg_ref[...]))
    _step_common(Bt, H, K, V, a, gqk_ref[:, :KW].astype(f32), gqk_ref[:, KW:].astype(f32),
                 gv_ref[...].astype(f32), gr_ref[...].astype(f32), gain_ref[...],
                 a_ref, k_ref, qa_ref, v_ref, oi_ref, s_in_ref, s_out_ref, o_ref)


def _step_call(kern, name, ins, in_specs, state, o_prev, s_prev, layer, row0, Bt, H, K, V, KW):
    depth, Bs = state.shape[:2]
    VW = H * V
    blk0 = row0 // Bt
    in_specs = list(in_specs) + [pl.BlockSpec((None, Bt, H, K, V), lambda i: (layer, i, 0, 0, 0)),
                                 pl.BlockSpec(memory_space=pl.ANY)]
    args = list(ins) + [state, o_prev]
    aliases = {len(args) - 1: 0}
    if s_prev is not None:
        in_specs.append(pl.BlockSpec(memory_space=pl.ANY))
        args.append(s_prev)
        aliases[len(args) - 1] = 1
    return pl.pallas_call(
        functools.partial(kern, Bt=Bt, H=H, K=K, V=V, aliased=len(aliases)),
        grid=(Bs // Bt,),
        in_specs=in_specs,
        out_specs=[pl.BlockSpec((Bt, VW), lambda i: (blk0 + i, 0)),
                   pl.BlockSpec((None, Bt, H, K, V), lambda i: (layer, i, 0, 0, 0))],
        out_shape=[jax.ShapeDtypeStruct(o_prev.shape, bf16),
                   jax.ShapeDtypeStruct((depth, Bs, H, K, V), f32)],
        scratch_shapes=[pltpu.VMEM((Bt, 1, KW), f32)] * 3 + [pltpu.VMEM((Bt, 1, VW), f32)] * 2,
        input_output_aliases=aliases,
        compiler_params=_params(("arbitrary",)),
        name=name,
    )(*args)


def _hgrn_step(zb, ld, kh, hgrn_norm, state, o_prev, s_prev, layer, row0, H, K, V):
    HW = H * K
    Bt = min(16, state.shape[1])
    blk0 = row0 // Bt
    row = lambda col: (lambda i: (blk0 + i, col))
    in_specs = [pl.BlockSpec((Bt, HW), row(0)),
                pl.BlockSpec((Bt, HW), row(1)),
                pl.BlockSpec((Bt, HW), row(2)),
                pl.BlockSpec((Bt, HW), row(0)),
                pl.BlockSpec((Bt, HW), row(0)),
                pl.BlockSpec((None, 1, V), lambda i: (layer, 0, 0))]
    return _step_call(_hgrn_step_kernel, "hgrn_step", (zb, zb, zb, ld, kh, hgrn_norm),
                      in_specs, state, o_prev, s_prev, layer, row0, Bt, H, K, V, HW)


def _gla_step(zb, glr, wgg, bgg, gla_norm, state, o_prev, s_prev, layer, row0, H, K, V):
    KW, VW = H * K, H * V
    Bt = min(16, state.shape[1])
    blk0 = row0 // Bt
    row = lambda col: (lambda i: (blk0 + i, col))
    in_specs = [pl.BlockSpec((Bt, 2 * KW), row(3)),
                pl.BlockSpec((Bt, VW), row(4)),
                pl.BlockSpec((Bt, VW), row(5)),
                pl.BlockSpec((Bt, LANES), row(0)),
                pl.BlockSpec((None, LANES, KW), lambda i: (layer, 0, 0)),
                pl.BlockSpec((None, 1, KW), lambda i: (layer, 0, 0)),
                pl.BlockSpec((None, 1, V), lambda i: (layer, 0, 0))]
    return _step_call(_gla_step_kernel, "gla_step", (zb, zb, zb, glr, wgg, bgg, gla_norm),
                      in_specs, state, o_prev, s_prev, layer, row0, Bt, H, K, V, KW)


def _postmix_kernel(oh_ref, og_ref, mh_ref, mg_ref, h_ref, whu_ref, wgu_ref, wout_ref,
                    npost_ref, npre_ref, h1_ref, c_ref):
    yh = _dot(oh_ref[...], whu_ref[...])
    yg = _dot(og_ref[...], wgu_ref[...])
    merged = mh_ref[...].astype(f32) * yh + mg_ref[...].astype(f32) * yg
    t = _dot(merged.astype(bf16), wout_ref[...])
    h1 = h_ref[...] + _rms(t, npost_ref[...])
    h1_ref[...] = h1
    c_ref[...] = _rms(h1, npre_ref[...]).astype(c_ref.dtype)


def _postmix(oh, og, zm, h, whu, wgu, wout, npost, npre, layer):
    M, D = h.shape
    HW, VW = oh.shape[1], og.shape[1]
    tm = _row_tile(M, ROW_CAP["postmix"])
    lay = lambda m: (layer, 0, 0)
    return pl.pallas_call(
        _postmix_kernel,
        grid=(M // tm,),
        in_specs=[pl.BlockSpec((tm, HW), lambda m: (m, 0)),
                  pl.BlockSpec((tm, VW), lambda m: (m, 0)),
                  pl.BlockSpec((tm, D), lambda m: (m, 0)),
                  pl.BlockSpec((tm, D), lambda m: (m, 1)),
                  pl.BlockSpec((tm, D), lambda m: (m, 0)),
                  _const_spec((None, HW, D), lay),
                  _const_spec((None, VW, D), lay),
                  _const_spec((None, D, D), lay),
                  pl.BlockSpec((None, 1, D), lay),
                  pl.BlockSpec((None, 1, D), lay)],
        out_specs=[pl.BlockSpec((tm, D), lambda m: (m, 0)),
                   pl.BlockSpec((tm, D), lambda m: (m, 0))],
        out_shape=[jax.ShapeDtypeStruct((M, D), f32),
                   jax.ShapeDtypeStruct((M, D), bf16)],
        compiler_params=_params(("parallel",)),
        name="postmix",
    )(oh, og, zm, zm, h, whu, wgu, wout, npost, npre)


def _ple_kernel(t_ref, h_ref, p_ref, wg_ref, wp_ref, npost_ref, nnext_ref, h3_ref, a_ref):
    h = h_ref[...] + _rms(t_ref[...], npost_ref[...])
    gate = _sigmoid(_dot(h.astype(bf16), wg_ref[...]))
    pe = _dot(p_ref[...].astype(bf16), wp_ref[...].astype(bf16))
    h3 = h + gate * pe
    h3_ref[...] = h3
    a_ref[...] = _rms(h3, nnext_ref[...]).astype(a_ref.dtype)


def _ple(t2, h1, p, wg, wp, npost, nnext, layer, next_layer):
    M, D = h1.shape
    P = p.shape[-1]
    tm = _row_tile(M, ROW_CAP["ple"])
    lay = lambda m: (layer, 0, 0)
    return pl.pallas_call(
        _ple_kernel,
        grid=(M // tm,),
        in_specs=[pl.BlockSpec((tm, D), lambda m: (m, 0)),
                  pl.BlockSpec((tm, D), lambda m: (m, 0)),
                  pl.BlockSpec((None, tm, P), lambda m: (layer, m, 0)),
                  _const_spec((None, D, D), lay),
                  _const_spec((None, P, D), lay),
                  pl.BlockSpec((None, 1, D), lay),
                  pl.BlockSpec((None, 1, D), lambda m: (next_layer, 0, 0))],
        out_specs=[pl.BlockSpec((tm, D), lambda m: (m, 0)),
                   pl.BlockSpec((tm, D), lambda m: (m, 0))],
        out_shape=[jax.ShapeDtypeStruct((M, D), f32),
                   jax.ShapeDtypeStruct((M, D), bf16)],
        compiler_params=_params(("parallel",)),
        name="ple",
    )(t2, h1, p, wg, wp, npost, nnext)


def kernel(x_prompt, x_sample, p_prompt, p_sample, state_hgrn, state_gla, norm_pre_mix,
           norm_post_mix, norm_pre_ffn, norm_post_ffn, w_in, lb_param, hgrn_norm, w_hgrn_up,
           w_gla_gate, b_gla_gate, gla_norm, w_gla_up, w_out, w_ff1, w_ff2, w_ple, w_ple_gate):
    B, L, D = x_prompt.shape
    Bs = x_sample.shape[0]
    depth = w_in.shape[0]
    _, _, HH, HK, HV = state_hgrn.shape
    _, _, GH, GK, GV = state_gla.shape
    HW, GKW, GVW = HH * HK, GH * GK, GH * GV
    R = w_gla_gate.shape[1]
    F = w_ff1.shape[2]
    assert HK == LANES and GK == LANES and HV % LANES == 0 and GV % LANES == 0
    assert HH * HV == HW and 2 * GKW == HW and GVW == HW and R <= LANES
    assert x_sample.shape[1] == 1 and L % CHUNK == 0 and D % HW == 0
    mixer_cols = 4 * HW + 2 * GKW + 2 * GVW

    assert w_in.shape[2] - mixer_cols >= LANES and (mixer_cols + R) % 8 == 0
    w_in_t = jnp.swapaxes(w_in, 1, 2)
    whu = w_hgrn_up.astype(bf16)
    wgu = w_gla_up.astype(bf16)
    wout = w_out.astype(bf16)
    wpg = w_ple_gate.astype(bf16)
    wgg = jnp.concatenate([w_gla_gate.astype(f32), jnp.zeros((depth, LANES - R, GKW), f32)], axis=1)

    lbs = jnp.cumsum(jax.nn.softmax(lb_param.astype(f32), axis=0), axis=0)
    lbs = (lbs - lbs[0:1]).reshape(depth, 1, HW)
    r3 = lambda t: t.astype(f32).reshape(depth, 1, t.shape[-1])
    n_pre_mix, n_post_mix, n_pre_ffn, n_post_ffn = map(
        r3, (norm_pre_mix, norm_post_mix, norm_pre_ffn, norm_post_ffn))
    hn, gn, bgg = r3(hgrn_norm), r3(gla_norm), r3(b_gla_gate)
    mall, lvl = _level_tables()
    seg_w = (HW, HW, HW, GKW, GKW, GVW, GVW)
    col_scale = jnp.asarray(np.concatenate(
        [np.full((1, w), v, np.float32) for w, v in zip(seg_w, (1, 1, 1, GK ** -0.5, 1, 1, 1))],
        axis=1))
    ep_main = (lambda n: (n == 0) | (n == 2) | (n == 5), _ep_silu, _ep_scale)
    tm_mg = _pick(D, (1024, 512, 256, 128))
    tf = _pick(F, (1024, 512, 256, 128))
    tn2 = _pick(D, (512, 256, 128))

    def in_projections(a):
        col_spec = pl.BlockSpec((1, HW), lambda n, m: (0, n))
        zb, = _proj(a, w_in_t, i, lambda n: (n + jnp.minimum(n, 1)) * HW, 6, HW, ep_main,
                    (col_scale,), (col_spec,), (bf16,), "in_proj", True)
        ld, kh = _proj(a, w_in_t, i, lambda n: HW, 1, HW, _ep_forget, (lbs,),
                       (pl.BlockSpec((None, 1, HW), lambda n, m: (i, 0, 0)),),
                       (f32, f32), "in_proj_forget", True)
        zm, = _proj(a, w_in_t, i, lambda n: mixer_cols + R + n * tm_mg, 2 * D // tm_mg, tm_mg,
                    _ep_sigmoid, (), (), (bf16,), "in_proj_merge", True)
        glr, = _proj(a, w_in_t, i, lambda n: mixer_cols, 1, LANES, _ep_copy, (), (), (f32,),
                     "in_proj_lowrank", True)
        return zb, ld, kh, zm, glr

    def dense_tail(h, oh, og, zm, p):
        h1, c = _postmix(oh, og, zm, h, whu, wgu, wout, n_post_mix, n_pre_ffn, i)
        u, = _proj(c, w_ff1, i, lambda n: n, F // tf, tf, _ep_relu2, (), (), (bf16,),
                   "ffn_up", False)
        t2, = _proj(u, w_ff2, i, lambda n: n, D // tn2, tn2, _ep_copy, (), (), (f32,),
                    "ffn_down", False, w_single=True)
        return _ple(t2, h1, p, wpg, w_ple, n_post_ffn, n_pre_mix, i, (i + 1) % depth)

    BL = B * L
    assert BL % 16 == 0 and Bs % 16 == 0
    h = jnp.concatenate([x_prompt.reshape(BL, D), x_sample.reshape(Bs, D)], axis=0)
    p = jnp.concatenate([p_prompt.reshape(depth, BL, -1), p_sample.reshape(depth, Bs, -1)], axis=1)
    a = _norm(h, n_pre_mix, 0)

    hgrn_p, gla_p = [], []
    hgrn_s = gla_s = None
    for i in range(depth):
        zb, ld, kh, zm, glr = in_projections(a)
        oh, sh = _hgrn_prompt(zb, ld, kh, hn, mall, lvl, i, B, L, HH, HK, HV)
        oh, hgrn_s = _hgrn_step(zb, ld, kh, hn, state_hgrn, oh, hgrn_s, i, BL, HH, HK, HV)
        og, sg = _gla_prompt(zb, glr, wgg, bgg, gn, mall, lvl, i, B, L, GH, GK, GV)
        og, gla_s = _gla_step(zb, glr, wgg, bgg, gn, state_gla, og, gla_s, i, BL, GH, GK, GV)
        h, a = dense_tail(h, oh, og, zm, p)
        hgrn_p.append(sh)
        gla_p.append(sg)

    return (h[:BL].reshape(B, L, D), h[BL:].reshape(Bs, 1, D),
            jnp.stack(hgrn_p).astype(state_hgrn.dtype), jnp.stack(gla_p).astype(state_gla.dtype),
            hgrn_s.astype(state_hgrn.dtype), gla_s.astype(state_gla.dtype))
```

```python
import functools

import numpy as np
import jax
import jax.numpy as jnp
from jax import lax
from jax.experimental import pallas as pl
from jax.experimental.pallas import tpu as pltpu

EPS = 1e-6
GLA_GATE_NORM = 16.0
LOG2E = 1.4426950408889634
LANES = 128
CHUNK = 64
N_LEVELS = 6
VMEM_LIMIT = 56 * 1024 * 1024
ROW_CAP = dict(norm=512, in_proj=2048, in_proj_forget=1024, in_proj_merge=2048,
               in_proj_lowrank=2048, ffn_up=2048, ffn_down=832, postmix=416, ple=512)

f32 = jnp.float32
bf16 = jnp.bfloat16


def _sigmoid(x):
    return 0.5 * jnp.tanh(0.5 * x) + 0.5


def _silu(x):
    hx = 0.5 * x
    return hx + hx * jnp.tanh(hx)


def _log_sigmoid(x):
    return jnp.minimum(x, 0.0) - jnp.log(1.0 + jnp.exp(-jnp.abs(x)))


def _rms(x, g):
    return x * lax.rsqrt(jnp.mean(x * x, axis=-1, keepdims=True) + EPS) * g


def _dot(a, b):
    return jnp.dot(a, b, preferred_element_type=f32)


def _dot_nt(a, b):
    return lax.dot_general(a, b, (((1,), (1,)), ((), ())), preferred_element_type=f32)


def _dot_tn(a, b):
    return lax.dot_general(a, b, (((0,), (0,)), ((), ())), preferred_element_type=f32)


def _split2(x):
    hi = x.astype(bf16)
    lo = (x - hi.astype(f32)).astype(bf16)
    return hi, lo


def _params(sem):
    return pltpu.CompilerParams(dimension_semantics=sem, vmem_limit_bytes=VMEM_LIMIT)


def _const_spec(shape, index_map):
    return pl.BlockSpec(shape, index_map, pipeline_mode=pl.Buffered(1))


def _pick(n, cands):
    for c in cands:
        if n % c == 0:
            return c
    return n


def _row_tile(m, cap):
    for t in range(min(cap, m), 0, -1):
        if m % t == 0 and t % 16 == 0:
            return t
    return m


def _level_tables():
    C = CHUNK
    t = np.arange(C)[:, None]
    j = np.arange(C)[None, :]
    mats = [(j <= t), (j > t)]
    lvl = np.full((C, C), -1, np.int32)
    lvl[t == j] = 0
    w = C // 2
    level = 1
    while w >= 1:
        start = (t // (2 * w)) * (2 * w)
        m = start + w - 1
        second = (t - start) >= w
        mats.append(np.where(second, (j > m) & (j <= t), (j > t) & (j <= m)))
        same = (t // (2 * w)) == (j // (2 * w))
        lvl[same & second & ((j - (j // (2 * w)) * (2 * w)) < w)] = level
        w //= 2
        level += 1
    mall = np.concatenate(mats[:-1], axis=0).astype(np.float32)
    return jnp.asarray(np.concatenate([mall, mall], axis=1), bf16), jnp.asarray(lvl)


def _norm_kernel(x_ref, g_ref, o_ref):
    o_ref[...] = _rms(x_ref[...], g_ref[...]).astype(o_ref.dtype)


def _norm(x, gains, layer):
    M, D = x.shape
    tm = _row_tile(M, ROW_CAP["norm"])
    return pl.pallas_call(
        _norm_kernel,
        grid=(M // tm,),
        in_specs=[pl.BlockSpec((tm, D), lambda m: (m, 0)),
                  pl.BlockSpec((None, 1, D), lambda m: (layer, 0, 0))],
        out_specs=pl.BlockSpec((tm, D), lambda m: (m, 0)),
        out_shape=jax.ShapeDtypeStruct((M, D), bf16),
        compiler_params=_params(("parallel",)),
        name="rmsnorm",
    )(x, gains)


def _proj_kernel(*refs, n_extra, epilogue, transposed, tn, slab):
    a_ref, w_ref = refs[0], refs[1]
    extra = refs[2:2 + n_extra]
    outs = refs[2 + n_extra:-1]
    wbf_ref = refs[-1]

    @pl.when(pl.program_id(1) == 0)
    def _():
        w = w_ref[0] if transposed else w_ref[...]
        wbf_ref[...] = w.astype(bf16)

    def body(ep):
        a = a_ref[...]
        for j in range(tn // slab):
            cols = slice(j * slab, (j + 1) * slab)
            if transposed:
                acc = _dot_nt(a, wbf_ref[cols, :])
            else:
                acc = _dot(a, wbf_ref[:, cols])
            ep(acc, cols, extra, outs)

    if isinstance(epilogue, tuple):
        pred_fn, ep_true, ep_false = epilogue
        pred = pred_fn(pl.program_id(0))
        pl.when(pred)(functools.partial(body, ep_true))
        pl.when(jnp.logical_not(pred))(functools.partial(body, ep_false))
    else:
        body(epilogue)


def _proj(a, w, layer, w_off, n_tiles, tn, epilogue, extra, extra_specs, out_dtypes, name,
          transposed, w_single=False):
    M, K = a.shape
    tm = _row_tile(M, ROW_CAP[name])
    if transposed:
        w_shape = (pl.Element(1), pl.Element(tn), pl.Element(K))
        def w_map(n, m):
            off = w_off(n)
            return (layer, off if isinstance(off, int) else pl.multiple_of(off, 8), 0)
        scratch = pltpu.VMEM((tn, K), bf16)
    else:
        w_shape = (None, K, tn)
        w_map = lambda n, m: (layer, 0, w_off(n))
        scratch = pltpu.VMEM((K, tn), bf16)
    w_spec = _const_spec(w_shape, w_map) if w_single else pl.BlockSpec(w_shape, w_map)
    kern = functools.partial(_proj_kernel, n_extra=len(extra), epilogue=epilogue,
                             transposed=transposed, tn=tn, slab=min(tn, 2 * LANES))
    return pl.pallas_call(
        kern,
        grid=(n_tiles, M // tm),
        in_specs=[pl.BlockSpec((tm, K), lambda n, m: (m, 0)), w_spec] + list(extra_specs),
        out_specs=[pl.BlockSpec((tm, tn), lambda n, m: (m, n)) for _ in out_dtypes],
        out_shape=[jax.ShapeDtypeStruct((M, n_tiles * tn), dt) for dt in out_dtypes],
        scratch_shapes=[scratch],
        compiler_params=_params(("parallel", "arbitrary")),
        name=name,
    )(a, w, *extra)


def _hgrn_gates(fl, lb):
    fl2 = fl * LOG2E
    e = jnp.exp2(-jnp.abs(fl2))
    t = 1.0 + e
    pos = fl >= 0.0
    num = jnp.where(pos, 1.0 + lb * e, lb + e)
    log2_num = jnp.where(pos | (lb > 0.0), jnp.log2(num), fl2)
    log2_f = log2_num - jnp.log2(t)
    r = 1.0 / t
    k = (1.0 - lb) * jnp.where(pos, e * r, r)
    return log2_f, k


def _ep_silu(acc, cols, extra, outs):
    outs[0][:, cols] = _silu(acc).astype(outs[0].dtype)


def _ep_scale(acc, cols, extra, outs):
    outs[0][:, cols] = (acc * extra[0][:, cols]).astype(outs[0].dtype)


def _ep_forget(acc, cols, extra, outs):
    ld_ref, k_ref = outs
    log2_f, k = _hgrn_gates(acc, extra[0][:, cols])
    ld_ref[:, cols] = log2_f
    k_ref[:, cols] = k


def _ep_sigmoid(acc, cols, extra, outs):
    outs[0][:, cols] = _sigmoid(acc).astype(outs[0].dtype)


def _ep_copy(acc, cols, extra, outs):
    outs[0][:, cols] = acc.astype(outs[0].dtype)


def _ep_relu2(acc, cols, extra, outs):
    outs[0][:, cols] = jnp.square(jnp.maximum(acc, 0.0)).astype(outs[0].dtype)


def _recurrence_chunks(T, H, K, V, q_ref, load_k, ld_ref, v_ref, g_ref, gain,
                       mall_ref, lvl_ref, st_ref, e_ref, o_ref):
    C = CHUNK

    def chunk(c, carry):
        r0 = pl.multiple_of(c * C, C)
        rows = pl.ds(r0, C)
        lvl = lvl_ref[...]
        hi, lo = _split2(ld_ref[rows, :])
        e_ref[...] = jnp.exp2(_dot(mall_ref[...], jnp.concatenate([hi, lo], axis=0)))
        heads = []
        for h in range(H):
            cs = slice(h * K, (h + 1) * K)
            vs = slice(h * V, (h + 1) * V)
            qc = q_ref[rows, cs].astype(f32)
            kc = load_k(rows, cs)
            vb = v_ref[rows, vs]
            st = st_ref[h]

            o = _dot_nt((qc * e_ref[0:C, cs]).astype(bf16), st.astype(bf16))

            diag = jnp.sum(qc * kc, axis=-1, keepdims=True)
            adj = jnp.sum(qc * pltpu.roll(kc, 1, axis=0) * jnp.exp2(ld_ref[rows, cs]),
                          axis=-1, keepdims=True)
            levels = []
            for l in range(1, N_LEVELS):
                el = e_ref[(l + 1) * C:(l + 2) * C, cs]
                levels.append(_dot_nt((qc * el).astype(bf16), (kc * el).astype(bf16)))

            kd = (kc * e_ref[C:2 * C, cs]).astype(bf16)
            alpha = e_ref[C - 1:C, cs]
            heads.append((h, vs, vb, st, kd, alpha, o, diag, adj, levels))

        outs = []
        for h, vs, vb, st, kd, alpha, o, diag, adj, levels in heads:
            a = jnp.where(lvl == 0, diag, jnp.where(lvl == N_LEVELS, adj, 0.0))
            for l, al in enumerate(levels, start=1):
                a = jnp.where(lvl == l, al, a)
            outs.append(o + _dot(a.astype(bf16), vb))

        for h, vs, vb, st, kd, alpha, o, diag, adj, levels in heads:
            st_ref[h] = st * alpha + _dot_tn(vb, kd)

        for (h, vs, *_), o in zip(heads, outs):
            gate = g_ref[rows, vs].astype(f32)
            o_ref[rows, vs] = (_rms(o, gain) * gate).astype(o_ref.dtype)
        return carry

    lax.fori_loop(0, T // C, chunk, 0)


def _write_state(tb, st_ref, s_ref, H):
    @pl.when(tb == pl.num_programs(1) - 1)
    def _():
        for h in range(H):
            s_ref[0, h] = st_ref[h].T


def _hgrn_prompt_kernel(zq_ref, zi_ref, zg_ref, ld_ref, k_ref, gain_ref, mall_ref,
                        lvl_ref, o_ref, s_ref, st_ref, e_ref, *, T, H, K, V):
    tb = pl.program_id(1)

    @pl.when(tb == 0)
    def _():
        st_ref[...] = jnp.zeros_like(st_ref)

    _recurrence_chunks(T, H, K, V, zq_ref, lambda rows, cs: k_ref[rows, cs], ld_ref,
                       zi_ref, zg_ref, gain_ref[...], mall_ref, lvl_ref, st_ref, e_ref, o_ref)
    _write_state(tb, st_ref, s_ref, H)


def _gla_log2_decay(glr, wg, bg):
    g_hi, g_lo = _split2(glr)
    w_hi, w_lo = _split2(wg)
    pre = _dot(g_hi, w_hi) + _dot(g_hi, w_lo) + _dot(g_lo, w_hi) + bg
    return _log_sigmoid(pre) * (LOG2E / GLA_GATE_NORM)


def _gla_prompt_kernel(gqk_ref, gv_ref, gr_ref, glr_ref, wg_ref, bg_ref, gain_ref, mall_ref,
                       lvl_ref, o_ref, s_ref, st_ref, ld_ref, e_ref, *, T, H, K, V):
    tb = pl.program_id(1)

    @pl.when(tb == 0)
    def _():
        st_ref[...] = jnp.zeros_like(st_ref)

    ld_ref[...] = _gla_log2_decay(glr_ref[...], wg_ref[...], bg_ref[...])
    KW = H * K
    k_ref = gqk_ref.at[:, KW:2 * KW]
    _recurrence_chunks(T, H, K, V, gqk_ref, lambda rows, cs: k_ref[rows, cs].astype(f32),
                       ld_ref, gv_ref, gr_ref, gain_ref[...], mall_ref, lvl_ref,
                       st_ref, e_ref, o_ref)
    _write_state(tb, st_ref, s_ref, H)


def _hgrn_prompt(zb, ld, kh, hgrn_norm, mall, lvl, layer, B, L, H, K, V):
    HW = H * K
    T = min(1024, L)
    nT = L // T
    tok = lambda col: (lambda b, t: (b * nT + t, col))
    cst = lambda b, t: (0, 0)
    kern = functools.partial(_hgrn_prompt_kernel, T=T, H=H, K=K, V=V)
    return pl.pallas_call(
        kern,
        grid=(B, nT),
        in_specs=[pl.BlockSpec((T, HW), tok(0)),
                  pl.BlockSpec((T, HW), tok(1)),
                  pl.BlockSpec((T, HW), tok(2)),
                  pl.BlockSpec((T, HW), tok(0)),
                  pl.BlockSpec((T, HW), tok(0)),
                  pl.BlockSpec((None, 1, V), lambda b, t: (layer, 0, 0)),
                  pl.BlockSpec(mall.shape, cst),
                  pl.BlockSpec(lvl.shape, cst)],
        out_specs=[pl.BlockSpec((T, HW), tok(0)),
                   pl.BlockSpec((1, H, K, V), lambda b, t: (b, 0, 0, 0))],
        out_shape=[jax.ShapeDtypeStruct((zb.shape[0], H * V), bf16),
                   jax.ShapeDtypeStruct((B, H, K, V), f32)],
        scratch_shapes=[pltpu.VMEM((H, V, K), f32),
                        pltpu.VMEM((mall.shape[0], HW), f32)],
        compiler_params=_params(("parallel", "arbitrary")),
        name="hgrn_prompt",
    )(zb, zb, zb, ld, kh, hgrn_norm, mall, lvl)


def _gla_prompt(zb, glr, wgg, bgg, gla_norm, mall, lvl, layer, B, L, H, K, V):
    KW, VW = H * K, H * V
    T = min(1024, L)
    nT = L // T
    tok = lambda col: (lambda b, t: (b * nT + t, col))
    cst = lambda b, t: (0, 0)
    kern = functools.partial(_gla_prompt_kernel, T=T, H=H, K=K, V=V)
    return pl.pallas_call(
        kern,
        grid=(B, nT),
        in_specs=[pl.BlockSpec((T, 2 * KW), tok(3)),
                  pl.BlockSpec((T, VW), tok(4)),
                  pl.BlockSpec((T, VW), tok(5)),
                  pl.BlockSpec((T, LANES), tok(0)),
                  pl.BlockSpec((None, LANES, KW), lambda b, t: (layer, 0, 0)),
                  pl.BlockSpec((None, 1, KW), lambda b, t: (layer, 0, 0)),
                  pl.BlockSpec((None, 1, V), lambda b, t: (layer, 0, 0)),
                  pl.BlockSpec(mall.shape, cst),
                  pl.BlockSpec(lvl.shape, cst)],
        out_specs=[pl.BlockSpec((T, VW), tok(0)),
                   pl.BlockSpec((1, H, K, V), lambda b, t: (b, 0, 0, 0))],
        out_shape=[jax.ShapeDtypeStruct((zb.shape[0], VW), bf16),
                   jax.ShapeDtypeStruct((B, H, K, V), f32)],
        scratch_shapes=[pltpu.VMEM((H, V, K), f32),
                        pltpu.VMEM((T, KW), f32),
                        pltpu.VMEM((mall.shape[0], KW), f32)],
        compiler_params=_params(("parallel", "arbitrary")),
        name="gla_prompt",
    )(zb, zb, zb, glr, wgg, bgg, gla_norm, mall, lvl)


def _step_rows(Bt, H, K, V, a_ref, k_ref, qa_ref, v_ref, s_in_ref, s_out_ref, oi_ref):
    r = lax.broadcasted_iota(jnp.int32, (K, K), 0)
    c = lax.broadcasted_iota(jnp.int32, (K, K), 1)
    eye = r == c
    ones = jnp.ones((K, LANES), bf16)
    nv = V // LANES

    def diag(x):
        return jnp.where(eye, jnp.broadcast_to(x, (K, K)), 0.0)

    def body(b, carry):
        pieces = []
        for h in range(H):
            cs = slice(h * K, (h + 1) * K)
            a = a_ref[b, :, cs]
            a_hi = a.astype(bf16).astype(f32)
            pieces += [diag(a_hi), diag(a - a_hi), diag(k_ref[b, :, cs]), diag(qa_ref[b, :, cs])]
        cb = _dot(jnp.concatenate(pieces, axis=0).astype(bf16), ones)
        for h in range(H):
            r0 = 4 * h * K
            a_col = cb[r0:r0 + K] + cb[r0 + K:r0 + 2 * K]
            k_col, q_col = cb[r0 + 2 * K:r0 + 3 * K], cb[r0 + 3 * K:r0 + 4 * K]
            for j in range(nv):
                ls = slice(h * V + j * LANES, h * V + (j + 1) * LANES)
                vj = slice(j * LANES, (j + 1) * LANES)
                s = s_in_ref[b, h, :, vj]
                s_out_ref[b, h, :, vj] = a_col * s + k_col * v_ref[b, :, ls]
                oi_ref[b, :, ls] = jnp.sum(q_col * s, axis=0, keepdims=True)
        return carry

    lax.fori_loop(0, Bt, body, 0)


def _store_rows(ref, x):
    for b in range(x.shape[0]):
        ref[b] = x[b:b + 1, :]


def _step_finish(H, K, V, q, k, v, g, gain, oi_ref, o_ref):
    ones = jnp.ones((K, LANES), bf16)
    nv = V // LANES
    for h in range(H):
        cs = slice(h * K, (h + 1) * K)
        vs = slice(h * V, (h + 1) * V)
        qk = _dot((q[:, cs] * k[:, cs]).astype(bf16), ones)
        if nv > 1:
            qk = jnp.concatenate([qk] * nv, axis=1)
        oi = jnp.concatenate([oi_ref[b, :, vs] for b in range(q.shape[0])], axis=0)
        o = qk * v[:, vs] + oi
        o_ref[:, vs] = (_rms(o, gain) * g[:, vs]).astype(o_ref.dtype)


def _step_common(Bt, H, K, V, a, q, k, v, g, gain, a_ref, k_ref, qa_ref, v_ref, oi_ref,
                 s_in_ref, s_out_ref, o_ref):
    _store_rows(a_ref, a)
    _store_rows(k_ref, k)
    _store_rows(qa_ref, q * a)
    _store_rows(v_ref, v)
    _step_rows(Bt, H, K, V, a_ref, k_ref, qa_ref, v_ref, s_in_ref, s_out_ref, oi_ref)
    _step_finish(H, K, V, q, k, v, g, gain, oi_ref, o_ref)


def _hgrn_step_kernel(*refs, Bt, H, K, V, aliased):
    (zq_ref, zi_ref, zg_ref, ld_ref, kh_ref, gain_ref, s_in_ref) = refs[:7]
    o_ref, s_out_ref, a_ref, k_ref, qa_ref, v_ref, oi_ref = refs[7 + aliased:]
    a = jnp.exp2(ld_ref[...])
    _step_common(Bt, H, K, V, a, zq_ref[...].astype(f32), kh_ref[...], zi_ref[...].astype(f32),
                 zg_ref[...].astype(f32), gain_ref[...], a_ref, k_ref, qa_ref, v_ref, oi_ref,
                 s_in_ref, s_out_ref, o_ref)


def _gla_step_kernel(*refs, Bt, H, K, V, aliased):
    (gqk_ref, gv_ref, gr_ref, glr_ref, wg_ref, bg_ref, gain_ref, s_in_ref) = refs[:8]
    o_ref, s_out_ref, a_ref, k_ref, qa_ref, v_ref, oi_ref = refs[8 + aliased:]
    KW = H * K
    a = jnp.exp2(_gla_log2_decay(glr_ref[...], wg_ref[...], bg_ref[...]))
    _step_common(Bt, H, K, V, a, gqk_ref[:, :KW].astype(f32), gqk_ref[:, KW:].astype(f32),
                 gv_ref[...].astype(f32), gr_ref[...].astype(f32), gain_ref[...],
                 a_ref, k_ref, qa_ref, v_ref, oi_ref, s_in_ref, s_out_ref, o_ref)


def _step_call(kern, name, ins, in_specs, state, o_prev, s_prev, layer, row0, Bt, H, K, V, KW):
    depth, Bs = state.shape[:2]
    VW = H * V
    blk0 = row0 // Bt
    in_specs = list(in_specs) + [pl.BlockSpec((None, Bt, H, K, V), lambda i: (layer, i, 0, 0, 0)),
                                 pl.BlockSpec(memory_space=pl.ANY)]
    args = list(ins) + [state, o_prev]
    aliases = {len(args) - 1: 0}
    if s_prev is not None:
        in_specs.append(pl.BlockSpec(memory_space=pl.ANY))
        args.append(s_prev)
        aliases[len(args) - 1] = 1
    return pl.pallas_call(
        functools.partial(kern, Bt=Bt, H=H, K=K, V=V, aliased=len(aliases)),
        grid=(Bs // Bt,),
        in_specs=in_specs,
        out_specs=[pl.BlockSpec((Bt, VW), lambda i: (blk0 + i, 0)),
                   pl.BlockSpec((None, Bt, H, K, V), lambda i: (layer, i, 0, 0, 0))],
        out_shape=[jax.ShapeDtypeStruct(o_prev.shape, bf16),
                   jax.ShapeDtypeStruct((depth, Bs, H, K, V), f32)],
        scratch_shapes=[pltpu.VMEM((Bt, 1, KW), f32)] * 3 + [pltpu.VMEM((Bt, 1, VW), f32)] * 2,
        input_output_aliases=aliases,
        compiler_params=_params(("arbitrary",)),
        name=name,
    )(*args)


def _hgrn_step(zb, ld, kh, hgrn_norm, state, o_prev, s_prev, layer, row0, H, K, V):
    HW = H * K
    Bt = min(16, state.shape[1])
    blk0 = row0 // Bt
    row = lambda col: (lambda i: (blk0 + i, col))
    in_specs = [pl.BlockSpec((Bt, HW), row(0)),
                pl.BlockSpec((Bt, HW), row(1)),
                pl.BlockSpec((Bt, HW), row(2)),
                pl.BlockSpec((Bt, HW), row(0)),
                pl.BlockSpec((Bt, HW), row(0)),
                pl.BlockSpec((None, 1, V), lambda i: (layer, 0, 0))]
    return _step_call(_hgrn_step_kernel, "hgrn_step", (zb, zb, zb, ld, kh, hgrn_norm),
                      in_specs, state, o_prev, s_prev, layer, row0, Bt, H, K, V, HW)


def _gla_step(zb, glr, wgg, bgg, gla_norm, state, o_prev, s_prev, layer, row0, H, K, V):
    KW, VW = H * K, H * V
    Bt = min(16, state.shape[1])
    blk0 = row0 // Bt
    row = lambda col: (lambda i: (blk0 + i, col))
    in_specs = [pl.BlockSpec((Bt, 2 * KW), row(3)),
                pl.BlockSpec((Bt, VW), row(4)),
                pl.BlockSpec((Bt, VW), row(5)),
                pl.BlockSpec((Bt, LANES), row(0)),
                pl.BlockSpec((None, LANES, KW), lambda i: (layer, 0, 0)),
                pl.BlockSpec((None, 1, KW), lambda i: (layer, 0, 0)),
                pl.BlockSpec((None, 1, V), lambda i: (layer, 0, 0))]
    return _step_call(_gla_step_kernel, "gla_step", (zb, zb, zb, glr, wgg, bgg, gla_norm),
                      in_specs, state, o_prev, s_prev, layer, row0, Bt, H, K, V, KW)


def _postmix_kernel(oh_ref, og_ref, mh_ref, mg_ref, h_ref, whu_ref, wgu_ref, wout_ref,
                    npost_ref, npre_ref, h1_ref, c_ref):
    yh = _dot(oh_ref[...], whu_ref[...])
    yg = _dot(og_ref[...], wgu_ref[...])
    merged = mh_ref[...].astype(f32) * yh + mg_ref[...].astype(f32) * yg
    t = _dot(merged.astype(bf16), wout_ref[...])
    h1 = h_ref[...] + _rms(t, npost_ref[...])
    h1_ref[...] = h1
    c_ref[...] = _rms(h1, npre_ref[...]).astype(c_ref.dtype)


def _postmix(oh, og, zm, h, whu, wgu, wout, npost, npre, layer):
    M, D = h.shape
    HW, VW = oh.shape[1], og.shape[1]
    tm = _row_tile(M, ROW_CAP["postmix"])
    lay = lambda m: (layer, 0, 0)
    return pl.pallas_call(
        _postmix_kernel,
        grid=(M // tm,),
        in_specs=[pl.BlockSpec((tm, HW), lambda m: (m, 0)),
                  pl.BlockSpec((tm, VW), lambda m: (m, 0)),
                  pl.BlockSpec((tm, D), lambda m: (m, 0)),
                  pl.BlockSpec((tm, D), lambda m: (m, 1)),
                  pl.BlockSpec((tm, D), lambda m: (m, 0)),
                  _const_spec((None, HW, D), lay),
                  _const_spec((None, VW, D), lay),
                  _const_spec((None, D, D), lay),
                  pl.BlockSpec((None, 1, D), lay),
                  pl.BlockSpec((None, 1, D), lay)],
        out_specs=[pl.BlockSpec((tm, D), lambda m: (m, 0)),
                   pl.BlockSpec((tm, D), lambda m: (m, 0))],
        out_shape=[jax.ShapeDtypeStruct((M, D), f32),
                   jax.ShapeDtypeStruct((M, D), bf16)],
        compiler_params=_params(("parallel",)),
        name="postmix",
    )(oh, og, zm, zm, h, whu, wgu, wout, npost, npre)


def _ple_kernel(t_ref, h_ref, p_ref, wg_ref, wp_ref, npost_ref, nnext_ref, h3_ref, a_ref):
    h = h_ref[...] + _rms(t_ref[...], npost_ref[...])
    gate = _sigmoid(_dot(h.astype(bf16), wg_ref[...]))
    pe = _dot(p_ref[...].astype(bf16), wp_ref[...].astype(bf16))
    h3 = h + gate * pe
    h3_ref[...] = h3
    a_ref[...] = _rms(h3, nnext_ref[...]).astype(a_ref.dtype)


def _ple(t2, h1, p, wg, wp, npost, nnext, layer, next_layer):
    M, D = h1.shape
    P = p.shape[-1]
    tm = _row_tile(M, ROW_CAP["ple"])
    lay = lambda m: (layer, 0, 0)
    return pl.pallas_call(
        _ple_kernel,
        grid=(M // tm,),
        in_specs=[pl.BlockSpec((tm, D), lambda m: (m, 0)),
                  pl.BlockSpec((tm, D), lambda m: (m, 0)),
                  pl.BlockSpec((None, tm, P), lambda m: (layer, m, 0)),
                  _const_spec((None, D, D), lay),
                  _const_spec((None, P, D), lay),
                  pl.BlockSpec((None, 1, D), lay),
                  pl.BlockSpec((None, 1, D), lambda m: (next_layer, 0, 0))],
        out_specs=[pl.BlockSpec((tm, D), lambda m: (m, 0)),
                   pl.BlockSpec((tm, D), lambda m: (m, 0))],
        out_shape=[jax.ShapeDtypeStruct((M, D), f32),
                   jax.ShapeDtypeStruct((M, D), bf16)],
        compiler_params=_params(("parallel",)),
        name="ple",
    )(t2, h1, p, wg, wp, npost, nnext)


def kernel(x_prompt, x_sample, p_prompt, p_sample, state_hgrn, state_gla, norm_pre_mix,
           norm_post_mix, norm_pre_ffn, norm_post_ffn, w_in, lb_param, hgrn_norm, w_hgrn_up,
           w_gla_gate, b_gla_gate, gla_norm, w_gla_up, w_out, w_ff1, w_ff2, w_ple, w_ple_gate):
    B, L, D = x_prompt.shape
    Bs = x_sample.shape[0]
    depth = w_in.shape[0]
    _, _, HH, HK, HV = state_hgrn.shape
    _, _, GH, GK, GV = state_gla.shape
    HW, GKW, GVW = HH * HK, GH * GK, GH * GV
    R = w_gla_gate.shape[1]
    F = w_ff1.shape[2]
    assert HK == LANES and GK == LANES and HV % LANES == 0 and GV % LANES == 0
    assert HH * HV == HW and 2 * GKW == HW and GVW == HW and R <= LANES
    assert x_sample.shape[1] == 1 and L % CHUNK == 0 and D % HW == 0
    mixer_cols = 4 * HW + 2 * GKW + 2 * GVW

    assert w_in.shape[2] - mixer_cols >= LANES and (mixer_cols + R) % 8 == 0
    w_in_t = jnp.swapaxes(w_in, 1, 2)
    whu = w_hgrn_up.astype(bf16)
    wgu = w_gla_up.astype(bf16)
    wout = w_out.astype(bf16)
    wpg = w_ple_gate.astype(bf16)
    wgg = jnp.concatenate([w_gla_gate.astype(f32), jnp.zeros((depth, LANES - R, GKW), f32)], axis=1)

    lbs = jnp.cumsum(jax.nn.softmax(lb_param.astype(f32), axis=0), axis=0)
    lbs = (lbs - lbs[0:1]).reshape(depth, 1, HW)
    r3 = lambda t: t.astype(f32).reshape(depth, 1, t.shape[-1])
    n_pre_mix, n_post_mix, n_pre_ffn, n_post_ffn = map(
        r3, (norm_pre_mix, norm_post_mix, norm_pre_ffn, norm_post_ffn))
    hn, gn, bgg = r3(hgrn_norm), r3(gla_norm), r3(b_gla_gate)
    mall, lvl = _level_tables()
    seg_w = (HW, HW, HW, GKW, GKW, GVW, GVW)
    col_scale = jnp.asarray(np.concatenate(
        [np.full((1, w), v, np.float32) for w, v in zip(seg_w, (1, 1, 1, GK ** -0.5, 1, 1, 1))],
        axis=1))
    ep_main = (lambda n: (n == 0) | (n == 2) | (n == 5), _ep_silu, _ep_scale)
    tm_mg = _pick(D, (1024, 512, 256, 128))
    tf = _pick(F, (1024, 512, 256, 128))
    tn2 = _pick(D, (512, 256, 128))

    def in_projections(a):
        col_spec = pl.BlockSpec((1, HW), lambda n, m: (0, n))
        zb, = _proj(a, w_in_t, i, lambda n: (n + jnp.minimum(n, 1)) * HW, 6, HW, ep_main,
                    (col_scale,), (col_spec,), (bf16,), "in_proj", True)
        ld, kh = _proj(a, w_in_t, i, lambda n: HW, 1, HW, _ep_forget, (lbs,),
                       (pl.BlockSpec((None, 1, HW), lambda n, m: (i, 0, 0)),),
                       (f32, f32), "in_proj_forget", True)
        zm, = _proj(a, w_in_t, i, lambda n: mixer_cols + R + n * tm_mg, 2 * D // tm_mg, tm_mg,
                    _ep_sigmoid, (), (), (bf16,), "in_proj_merge", True)
        glr, = _proj(a, w_in_t, i, lambda n: mixer_cols, 1, LANES, _ep_copy, (), (), (f32,),
                     "in_proj_lowrank", True)
        return zb, ld, kh, zm, glr

    def dense_tail(h, oh, og, zm, p):
        h1, c = _postmix(oh, og, zm, h, whu, wgu, wout, n_post_mix, n_pre_ffn, i)
        u, = _proj(c, w_ff1, i, lambda n: n, F // tf, tf, _ep_relu2, (), (), (bf16,),
                   "ffn_up", False)
        t2, = _proj(u, w_ff2, i, lambda n: n, D // tn2, tn2, _ep_copy, (), (), (f32,),
                    "ffn_down", False, w_single=True)
        return _ple(t2, h1, p, wpg, w_ple, n_post_ffn, n_pre_mix, i, (i + 1) % depth)

    BL = B * L
    assert BL % 16 == 0 and Bs % 16 == 0
    h = jnp.concatenate([x_prompt.reshape(BL, D), x_sample.reshape(Bs, D)], axis=0)
    p = jnp.concatenate([p_prompt.reshape(depth, BL, -1), p_sample.reshape(depth, Bs, -1)], axis=1)
    a = _norm(h, n_pre_mix, 0)

    hgrn_p, gla_p = [], []
    hgrn_s = gla_s = None
    for i in range(depth):
        zb, ld, kh, zm, glr = in_projections(a)
        oh, sh = _hgrn_prompt(zb, ld, kh, hn, mall, lvl, i, B, L, HH, HK, HV)
        oh, hgrn_s = _hgrn_step(zb, ld, kh, hn, state_hgrn, oh, hgrn_s, i, BL, HH, HK, HV)
        og, sg = _gla_prompt(zb, glr, wgg, bgg, gn, mall, lvl, i, B, L, GH, GK, GV)
        og, gla_s = _gla_step(zb, glr, wgg, bgg, gn, state_gla, og, gla_s, i, BL, GH, GK, GV)
        h, a = dense_tail(h, oh, og, zm, p)
        hgrn_p.append(sh)
        gla_p.append(sg)

    return (h[:BL].reshape(B, L, D), h[BL:].reshape(Bs, 1, D),
            jnp.stack(hgrn_p).astype(state_hgrn.dtype), jnp.stack(gla_p).astype(state_gla.dtype),
            hgrn_s.astype(state_hgrn.dtype), gla_s.astype(state_gla.dtype))
```

```python
import functools

import numpy as np
import jax
import jax.numpy as jnp
from jax import lax
from jax.experimental import pallas as pl
from jax.experimental.pallas import tpu as pltpu

EPS = 1e-6
GLA_GATE_NORM = 16.0
LOG2E = 1.4426950408889634
LANES = 128
CHUNK = 64
N_LEVELS = 6
VMEM_LIMIT = 56 * 1024 * 1024
ROW_CAP = dict(norm=512, in_proj=2048, in_proj_forget=1024, in_proj_merge=2048,
               in_proj_lowrank=2048, ffn_up=2048, ffn_down=832, postmix=416, ple=512)

f32 = jnp.float32
bf16 = jnp.bfloat16


def _sigmoid(x):
    return 0.5 * jnp.tanh(0.5 * x) + 0.5


def _silu(x):
    hx = 0.5 * x
    return hx + hx * jnp.tanh(hx)


def _log_sigmoid(x):
    return jnp.minimum(x, 0.0) - jnp.log(1.0 + jnp.exp(-jnp.abs(x)))


def _rms(x, g):
    return x * lax.rsqrt(jnp.mean(x * x, axis=-1, keepdims=True) + EPS) * g


def _dot(a, b):
    return jnp.dot(a, b, preferred_element_type=f32)


def _dot_nt(a, b):
    return lax.dot_general(a, b, (((1,), (1,)), ((), ())), preferred_element_type=f32)


def _dot_tn(a, b):
    return lax.dot_general(a, b, (((0,), (0,)), ((), ())), preferred_element_type=f32)


def _split2(x):
    hi = x.astype(bf16)
    lo = (x - hi.astype(f32)).astype(bf16)
    return hi, lo


def _params(sem):
    return pltpu.CompilerParams(dimension_semantics=sem, vmem_limit_bytes=VMEM_LIMIT)


def _const_spec(shape, index_map):
    return pl.BlockSpec(shape, index_map, pipeline_mode=pl.Buffered(1))


def _pick(n, cands):
    for c in cands:
        if n % c == 0:
            return c
    return n


def _row_tile(m, cap):
    for t in range(min(cap, m), 0, -1):
        if m % t == 0 and t % 16 == 0:
            return t
    return m


def _level_tables():
    C = CHUNK
    t = np.arange(C)[:, None]
    j = np.arange(C)[None, :]
    mats = [(j <= t), (j > t)]
    lvl = np.full((C, C), -1, np.int32)
    lvl[t == j] = 0
    w = C // 2
    level = 1
    while w >= 1:
        start = (t // (2 * w)) * (2 * w)
        m = start + w - 1
        second = (t - start) >= w
        mats.append(np.where(second, (j > m) & (j <= t), (j > t) & (j <= m)))
        same = (t // (2 * w)) == (j // (2 * w))
        lvl[same & second & ((j - (j // (2 * w)) * (2 * w)) < w)] = level
        w //= 2
        level += 1
    mall = np.concatenate(mats[:-1], axis=0).astype(np.float32)
    return jnp.asarray(np.concatenate([mall, mall], axis=1), bf16), jnp.asarray(lvl)


def _norm_kernel(x_ref, g_ref, o_ref):
    o_ref[...] = _rms(x_ref[...], g_ref[...]).astype(o_ref.dtype)


def _norm(x, gains, layer):
    M, D = x.shape
    tm = _row_tile(M, ROW_CAP["norm"])
    return pl.pallas_call(
        _norm_kernel,
        grid=(M // tm,),
        in_specs=[pl.BlockSpec((tm, D), lambda m: (m, 0)),
                  pl.BlockSpec((None, 1, D), lambda m: (layer, 0, 0))],
        out_specs=pl.BlockSpec((tm, D), lambda m: (m, 0)),
        out_shape=jax.ShapeDtypeStruct((M, D), bf16),
        compiler_params=_params(("parallel",)),
        name="rmsnorm",
    )(x, gains)


def _proj_kernel(*refs, n_extra, epilogue, transposed, tn, slab):
    a_ref, w_ref = refs[0], refs[1]
    extra = refs[2:2 + n_extra]
    outs = refs[2 + n_extra:-1]
    wbf_ref = refs[-1]

    @pl.when(pl.program_id(1) == 0)
    def _():
        w = w_ref[0] if transposed else w_ref[...]
        wbf_ref[...] = w.astype(bf16)

    def body(ep):
        a = a_ref[...]
        for j in range(tn // slab):
            cols = slice(j * slab, (j + 1) * slab)
            if transposed:
                acc = _dot_nt(a, wbf_ref[cols, :])
            else:
                acc = _dot(a, wbf_ref[:, cols])
            ep(acc, cols, extra, outs)

    if isinstance(epilogue, tuple):
        pred_fn, ep_true, ep_false = epilogue
        pred = pred_fn(pl.program_id(0))
        pl.when(pred)(functools.partial(body, ep_true))
        pl.when(jnp.logical_not(pred))(functools.partial(body, ep_false))
    else:
        body(epilogue)


def _proj(a, w, layer, w_off, n_tiles, tn, epilogue, extra, extra_specs, out_dtypes, name,
          transposed, w_single=False):
    M, K = a.shape
    tm = _row_tile(M, ROW_CAP[name])
    if transposed:
        w_shape = (pl.Element(1), pl.Element(tn), pl.Element(K))
        def w_map(n, m):
            off = w_off(n)
            return (layer, off if isinstance(off, int) else pl.multiple_of(off, 8), 0)
        scratch = pltpu.VMEM((tn, K), bf16)
    else:
        w_shape = (None, K, tn)
        w_map = lambda n, m: (layer, 0, w_off(n))
        scratch = pltpu.VMEM((K, tn), bf16)
    w_spec = _const_spec(w_shape, w_map) if w_single else pl.BlockSpec(w_shape, w_map)
    kern = functools.partial(_proj_kernel, n_extra=len(extra), epilogue=epilogue,
                             transposed=transposed, tn=tn, slab=min(tn, 2 * LANES))
    return pl.pallas_call(
        kern,
        grid=(n_tiles, M // tm),
        in_specs=[pl.BlockSpec((tm, K), lambda n, m: (m, 0)), w_spec] + list(extra_specs),
        out_specs=[pl.BlockSpec((tm, tn), lambda n, m: (m, n)) for _ in out_dtypes],
        out_shape=[jax.ShapeDtypeStruct((M, n_tiles * tn), dt) for dt in out_dtypes],
        scratch_shapes=[scratch],
        compiler_params=_params(("parallel", "arbitrary")),
        name=name,
    )(a, w, *extra)


def _hgrn_gates(fl, lb):
    fl2 = fl * LOG2E
    e = jnp.exp2(-jnp.abs(fl2))
    t = 1.0 + e
    pos = fl >= 0.0
    num = jnp.where(pos, 1.0 + lb * e, lb + e)
    log2_num = jnp.where(pos | (lb > 0.0), jnp.log2(num), fl2)
    log2_f = log2_num - jnp.log2(t)
    r = 1.0 / t
    k = (1.0 - lb) * jnp.where(pos, e * r, r)
    return log2_f, k


def _ep_silu(acc, cols, extra, outs):
    outs[0][:, cols] = _silu(acc).astype(outs[0].dtype)


def _ep_scale(acc, cols, extra, outs):
    outs[0][:, cols] = (acc * extra[0][:, cols]).astype(outs[0].dtype)


def _ep_forget(acc, cols, extra, outs):
    ld_ref, k_ref = outs
    log2_f, k = _hgrn_gates(acc, extra[0][:, cols])
    ld_ref[:, cols] = log2_f
    k_ref[:, cols] = k


def _ep_sigmoid(acc, cols, extra, outs):
    outs[0][:, cols] = _sigmoid(acc).astype(outs[0].dtype)


def _ep_copy(acc, cols, extra, outs):
    outs[0][:, cols] = acc.astype(outs[0].dtype)


def _ep_relu2(acc, cols, extra, outs):
    outs[0][:, cols] = jnp.square(jnp.maximum(acc, 0.0)).astype(outs[0].dtype)


def _recurrence_chunks(T, H, K, V, q_ref, load_k, ld_ref, v_ref, g_ref, gain,
                       mall_ref, lvl_ref, st_ref, e_ref, o_ref):
    C = CHUNK

    def chunk(c, carry):
        r0 = pl.multiple_of(c * C, C)
        rows = pl.ds(r0, C)
        lvl = lvl_ref[...]
        hi, lo = _split2(ld_ref[rows, :])
        e_ref[...] = jnp.exp2(_dot(mall_ref[...], jnp.concatenate([hi, lo], axis=0)))
        heads = []
        for h in range(H):
            cs = slice(h * K, (h + 1) * K)
            vs = slice(h * V, (h + 1) * V)
            qc = q_ref[rows, cs].astype(f32)
            kc = load_k(rows, cs)
            vb = v_ref[rows, vs]
            st = st_ref[h]

            o = _dot_nt((qc * e_ref[0:C, cs]).astype(bf16), st.astype(bf16))

            diag = jnp.sum(qc * kc, axis=-1, keepdims=True)
            adj = jnp.sum(qc * pltpu.roll(kc, 1, axis=0) * jnp.exp2(ld_ref[rows, cs]),
                          axis=-1, keepdims=True)
            levels = []
            for l in range(1, N_LEVELS):
                el = e_ref[(l + 1) * C:(l + 2) * C, cs]
                levels.append(_dot_nt((qc * el).astype(bf16), (kc * el).astype(bf16)))

            kd = (kc * e_ref[C:2 * C, cs]).astype(bf16)
            alpha = e_ref[C - 1:C, cs]
            heads.append((h, vs, vb, st, kd, alpha, o, diag, adj, levels))

        outs = []
        for h, vs, vb, st, kd, alpha, o, diag, adj, levels in heads:
            a = jnp.where(lvl == 0, diag, jnp.where(lvl == N_LEVELS, adj, 0.0))
            for l, al in enumerate(levels, start=1):
                a = jnp.where(lvl == l, al, a)
            outs.append(o + _dot(a.astype(bf16), vb))

        for h, vs, vb, st, kd, alpha, o, diag, adj, levels in heads:
            st_ref[h] = st * alpha + _dot_tn(vb, kd)

        for (h, vs, *_), o in zip(heads, outs):
            gate = g_ref[rows, vs].astype(f32)
            o_ref[rows, vs] = (_rms(o, gain) * gate).astype(o_ref.dtype)
        return carry

    lax.fori_loop(0, T // C, chunk, 0, unroll=2)


def _write_state(tb, st_ref, s_ref, H):
    @pl.when(tb == pl.num_programs(1) - 1)
    def _():
        for h in range(H):
            s_ref[0, h] = st_ref[h].T


def _hgrn_prompt_kernel(zq_ref, zi_ref, zg_ref, ld_ref, k_ref, gain_ref, mall_ref,
                        lvl_ref, o_ref, s_ref, st_ref, e_ref, *, T, H, K, V):
    tb = pl.program_id(1)

    @pl.when(tb == 0)
    def _():
        st_ref[...] = jnp.zeros_like(st_ref)

    _recurrence_chunks(T, H, K, V, zq_ref, lambda rows, cs: k_ref[rows, cs], ld_ref,
                       zi_ref, zg_ref, gain_ref[...], mall_ref, lvl_ref, st_ref, e_ref, o_ref)
    _write_state(tb, st_ref, s_ref, H)


def _gla_log2_decay(glr, wg, bg):
    g_hi, g_lo = _split2(glr)
    w_hi, w_lo = _split2(wg)
    pre = _dot(g_hi, w_hi) + _dot(g_hi, w_lo) + _dot(g_lo, w_hi) + bg
    return _log_sigmoid(pre) * (LOG2E / GLA_GATE_NORM)


def _gla_prompt_kernel(gqk_ref, gv_ref, gr_ref, glr_ref, wg_ref, bg_ref, gain_ref, mall_ref,
                       lvl_ref, o_ref, s_ref, st_ref, ld_ref, e_ref, *, T, H, K, V):
    tb = pl.program_id(1)

    @pl.when(tb == 0)
    def _():
        st_ref[...] = jnp.zeros_like(st_ref)

    ld_ref[...] = _gla_log2_decay(glr_ref[...], wg_ref[...], bg_ref[...])
    KW = H * K
    k_ref = gqk_ref.at[:, KW:2 * KW]
    _recurrence_chunks(T, H, K, V, gqk_ref, lambda rows, cs: k_ref[rows, cs].astype(f32),
                       ld_ref, gv_ref, gr_ref, gain_ref[...], mall_ref, lvl_ref,
                       st_ref, e_ref, o_ref)
    _write_state(tb, st_ref, s_ref, H)


def _hgrn_prompt(zb, ld, kh, hgrn_norm, mall, lvl, layer, B, L, H, K, V):
    HW = H * K
    T = min(1024, L)
    nT = L // T
    tok = lambda col: (lambda b, t: (b * nT + t, col))
    cst = lambda b, t: (0, 0)
    kern = functools.partial(_hgrn_prompt_kernel, T=T, H=H, K=K, V=V)
    return pl.pallas_call(
        kern,
        grid=(B, nT),
        in_specs=[pl.BlockSpec((T, HW), tok(0)),
                  pl.BlockSpec((T, HW), tok(1)),
                  pl.BlockSpec((T, HW), tok(2)),
                  pl.BlockSpec((T, HW), tok(0)),
                  pl.BlockSpec((T, HW), tok(0)),
                  pl.BlockSpec((None, 1, V), lambda b, t: (layer, 0, 0)),
                  pl.BlockSpec(mall.shape, cst),
                  pl.BlockSpec(lvl.shape, cst)],
        out_specs=[pl.BlockSpec((T, HW), tok(0)),
                   pl.BlockSpec((1, H, K, V), lambda b, t: (b, 0, 0, 0))],
        out_shape=[jax.ShapeDtypeStruct((zb.shape[0], H * V), bf16),
                   jax.ShapeDtypeStruct((B, H, K, V), f32)],
        scratch_shapes=[pltpu.VMEM((H, V, K), f32),
                        pltpu.VMEM((mall.shape[0], HW), f32)],
        compiler_params=_params(("parallel", "arbitrary")),
        name="hgrn_prompt",
    )(zb, zb, zb, ld, kh, hgrn_norm, mall, lvl)


def _gla_prompt(zb, glr, wgg, bgg, gla_norm, mall, lvl, layer, B, L, H, K, V):
    KW, VW = H * K, H * V
    T = min(1024, L)
    nT = L // T
    tok = lambda col: (lambda b, t: (b * nT + t, col))
    cst = lambda b, t: (0, 0)
    kern = functools.partial(_gla_prompt_kernel, T=T, H=H, K=K, V=V)
    return pl.pallas_call(
        kern,
        grid=(B, nT),
        in_specs=[pl.BlockSpec((T, 2 * KW), tok(3)),
                  pl.BlockSpec((T, VW), tok(4)),
                  pl.BlockSpec((T, VW), tok(5)),
                  pl.BlockSpec((T, LANES), tok(0)),
                  pl.BlockSpec((None, LANES, KW), lambda b, t: (layer, 0, 0)),
                  pl.BlockSpec((None, 1, KW), lambda b, t: (layer, 0, 0)),
                  pl.BlockSpec((None, 1, V), lambda b, t: (layer, 0, 0)),
                  pl.BlockSpec(mall.shape, cst),
                  pl.BlockSpec(lvl.shape, cst)],
        out_specs=[pl.BlockSpec((T, VW), tok(0)),
                   pl.BlockSpec((1, H, K, V), lambda b, t: (b, 0, 0, 0))],
        out_shape=[jax.ShapeDtypeStruct((zb.shape[0], VW), bf16),
                   jax.ShapeDtypeStruct((B, H, K, V), f32)],
        scratch_shapes=[pltpu.VMEM((H, V, K), f32),
                        pltpu.VMEM((T, KW), f32),
                        pltpu.VMEM((mall.shape[0], KW), f32)],
        compiler_params=_params(("parallel", "arbitrary")),
        name="gla_prompt",
    )(zb, zb, zb, glr, wgg, bgg, gla_norm, mall, lvl)


def _step_rows(Bt, H, K, V, a_ref, k_ref, qa_ref, v_ref, s_in_ref, s_out_ref, oi_ref):
    r = lax.broadcasted_iota(jnp.int32, (K, K), 0)
    c = lax.broadcasted_iota(jnp.int32, (K, K), 1)
    eye = r == c
    ones = jnp.ones((K, LANES), bf16)
    nv = V // LANES

    def diag(x):
        return jnp.where(eye, jnp.broadcast_to(x, (K, K)), 0.0)

    def body(b, carry):
        pieces = []
        for h in range(H):
            cs = slice(h * K, (h + 1) * K)
            a = a_ref[b, :, cs]
            a_hi = a.astype(bf16).astype(f32)
            pieces += [diag(a_hi), diag(a - a_hi), diag(k_ref[b, :, cs]), diag(qa_ref[b, :, cs])]
        cb = _dot(jnp.concatenate(pieces, axis=0).astype(bf16), ones)
        for h in range(H):
            r0 = 4 * h * K
            a_col = cb[r0:r0 + K] + cb[r0 + K:r0 + 2 * K]
            k_col, q_col = cb[r0 + 2 * K:r0 + 3 * K], cb[r0 + 3 * K:r0 + 4 * K]
            for j in range(nv):
                ls = slice(h * V + j * LANES, h * V + (j + 1) * LANES)
                vj = slice(j * LANES, (j + 1) * LANES)
                s = s_in_ref[b, h, :, vj]
                s_out_ref[b, h, :, vj] = a_col * s + k_col * v_ref[b, :, ls]
                oi_ref[b, :, ls] = jnp.sum(q_col * s, axis=0, keepdims=True)
        return carry

    lax.fori_loop(0, Bt, body, 0, unroll=4)


def _store_rows(ref, x):
    for b in range(x.shape[0]):
        ref[b] = x[b:b + 1, :]


def _step_finish(H, K, V, q, k, v, g, gain, oi_ref, o_ref):
    ones = jnp.ones((K, LANES), bf16)
    nv = V // LANES
    for h in range(H):
        cs = slice(h * K, (h + 1) * K)
        vs = slice(h * V, (h + 1) * V)
        qk = _dot((q[:, cs] * k[:, cs]).astype(bf16), ones)
        if nv > 1:
            qk = jnp.concatenate([qk] * nv, axis=1)
        oi = jnp.concatenate([oi_ref[b, :, vs] for b in range(q.shape[0])], axis=0)
        o = qk * v[:, vs] + oi
        o_ref[:, vs] = (_rms(o, gain) * g[:, vs]).astype(o_ref.dtype)


def _step_common(Bt, H, K, V, a, q, k, v, g, gain, a_ref, k_ref, qa_ref, v_ref, oi_ref,
                 s_in_ref, s_out_ref, o_ref):
    _store_rows(a_ref, a)
    _store_rows(k_ref, k)
    _store_rows(qa_ref, q * a)
    _store_rows(v_ref, v)
    _step_rows(Bt, H, K, V, a_ref, k_ref, qa_ref, v_ref, s_in_ref, s_out_ref, oi_ref)
    _step_finish(H, K, V, q, k, v, g, gain, oi_ref, o_ref)


def _hgrn_step_kernel(*refs, Bt, H, K, V, aliased):
    (zq_ref, zi_ref, zg_ref, ld_ref, kh_ref, gain_ref, s_in_ref) = refs[:7]
    o_ref, s_out_ref, a_ref, k_ref, qa_ref, v_ref, oi_ref = refs[7 + aliased:]
    a = jnp.exp2(ld_ref[...])
    _step_common(Bt, H, K, V, a, zq_ref[...].astype(f32), kh_ref[...], zi_ref[...].astype(f32),
                 zg_ref[...].astype(f32), gain_ref[...], a_ref, k_ref, qa_ref, v_ref, oi_ref,
                 s_in_ref, s_out_ref, o_ref)


def _gla_step_kernel(*refs, Bt, H, K, V, aliased):
    (gqk_ref, gv_ref, gr_ref, glr_ref, wg_ref, bg_ref, gain_ref, s_in_ref) = refs[:8]
    o_ref, s_out_ref, a_ref, k_ref, qa_ref, v_ref, oi_ref = refs[8 + aliased:]
    KW = H * K
    a = jnp.exp2(_gla_log2_decay(glr_ref[...], wg_ref[...], bg_ref[...]))
    _step_common(Bt, H, K, V, a, gqk_ref[:, :KW].astype(f32), gqk_ref[:, KW:].astype(f32),
                 gv_ref[...].astype(f32), gr_ref[...].astype(f32), gain_ref[...],
                 a_ref, k_ref, qa_ref, v_ref, oi_ref, s_in_ref, s_out_ref, o_ref)


def _step_call(kern, name, ins, in_specs, state, o_prev, s_prev, layer, row0, Bt, H, K, V, KW):
    depth, Bs = state.shape[:2]
    VW = H * V
    blk0 = row0 // Bt
    in_specs = list(in_specs) + [pl.BlockSpec((None, Bt, H, K, V), lambda i: (layer, i, 0, 0, 0)),
                                 pl.BlockSpec(memory_space=pl.ANY)]
    args = list(ins) + [state, o_prev]
    aliases = {len(args) - 1: 0}
    if s_prev is not None:
        in_specs.append(pl.BlockSpec(memory_space=pl.ANY))
        args.append(s_prev)
        aliases[len(args) - 1] = 1
    return pl.pallas_call(
        functools.partial(kern, Bt=Bt, H=H, K=K, V=V, aliased=len(aliases)),
        grid=(Bs // Bt,),
        in_specs=in_specs,
        out_specs=[pl.BlockSpec((Bt, VW), lambda i: (blk0 + i, 0)),
                   pl.BlockSpec((None, Bt, H, K, V), lambda i: (layer, i, 0, 0, 0))],
        out_shape=[jax.ShapeDtypeStruct(o_prev.shape, bf16),
                   jax.ShapeDtypeStruct((depth, Bs, H, K, V), f32)],
        scratch_shapes=[pltpu.VMEM((Bt, 1, KW), f32)] * 3 + [pltpu.VMEM((Bt, 1, VW), f32)] * 2,
        input_output_aliases=aliases,
        compiler_params=_params(("arbitrary",)),
        name=name,
    )(*args)


def _hgrn_step(zb, ld, kh, hgrn_norm, state, o_prev, s_prev, layer, row0, H, K, V):
    HW = H * K
    Bt = min(16, state.shape[1])
    blk0 = row0 // Bt
    row = lambda col: (lambda i: (blk0 + i, col))
    in_specs = [pl.BlockSpec((Bt, HW), row(0)),
                pl.BlockSpec((Bt, HW), row(1)),
                pl.BlockSpec((Bt, HW), row(2)),
                pl.BlockSpec((Bt, HW), row(0)),
                pl.BlockSpec((Bt, HW), row(0)),
                pl.BlockSpec((None, 1, V), lambda i: (layer, 0, 0))]
    return _step_call(_hgrn_step_kernel, "hgrn_step", (zb, zb, zb, ld, kh, hgrn_norm),
                      in_specs, state, o_prev, s_prev, layer, row0, Bt, H, K, V, HW)


def _gla_step(zb, glr, wgg, bgg, gla_norm, state, o_prev, s_prev, layer, row0, H, K, V):
    KW, VW = H * K, H * V
    Bt = min(16, state.shape[1])
    blk0 = row0 // Bt
    row = lambda col: (lambda i: (blk0 + i, col))
    in_specs = [pl.BlockSpec((Bt, 2 * KW), row(3)),
                pl.BlockSpec((Bt, VW), row(4)),
                pl.BlockSpec((Bt, VW), row(5)),
                pl.BlockSpec((Bt, LANES), row(0)),
                pl.BlockSpec((None, LANES, KW), lambda i: (layer, 0, 0)),
                pl.BlockSpec((None, 1, KW), lambda i: (layer, 0, 0)),
                pl.BlockSpec((None, 1, V), lambda i: (layer, 0, 0))]
    return _step_call(_gla_step_kernel, "gla_step", (zb, zb, zb, glr, wgg, bgg, gla_norm),
                      in_specs, state, o_prev, s_prev, layer, row0, Bt, H, K, V, KW)


def _postmix_kernel(oh_ref, og_ref, mh_ref, mg_ref, h_ref, whu_ref, wgu_ref, wout_ref,
                    npost_ref, npre_ref, h1_ref, c_ref):
    yh = _dot(oh_ref[...], whu_ref[...])
    yg = _dot(og_ref[...], wgu_ref[...])
    merged = mh_ref[...].astype(f32) * yh + mg_ref[...].astype(f32) * yg
    t = _dot(merged.astype(bf16), wout_ref[...])
    h1 = h_ref[...] + _rms(t, npost_ref[...])
    h1_ref[...] = h1
    c_ref[...] = _rms(h1, npre_ref[...]).astype(c_ref.dtype)


def _postmix(oh, og, zm, h, whu, wgu, wout, npost, npre, layer):
    M, D = h.shape
    HW, VW = oh.shape[1], og.shape[1]
    tm = _row_tile(M, ROW_CAP["postmix"])
    lay = lambda m: (layer, 0, 0)
    return pl.pallas_call(
        _postmix_kernel,
        grid=(M // tm,),
        in_specs=[pl.BlockSpec((tm, HW), lambda m: (m, 0)),
                  pl.BlockSpec((tm, VW), lambda m: (m, 0)),
                  pl.BlockSpec((tm, D), lambda m: (m, 0)),
                  pl.BlockSpec((tm, D), lambda m: (m, 1)),
                  pl.BlockSpec((tm, D), lambda m: (m, 0)),
                  _const_spec((None, HW, D), lay),
                  _const_spec((None, VW, D), lay),
                  _const_spec((None, D, D), lay),
                  pl.BlockSpec((None, 1, D), lay),
                  pl.BlockSpec((None, 1, D), lay)],
        out_specs=[pl.BlockSpec((tm, D), lambda m: (m, 0)),
                   pl.BlockSpec((tm, D), lambda m: (m, 0))],
        out_shape=[jax.ShapeDtypeStruct((M, D), f32),
                   jax.ShapeDtypeStruct((M, D), bf16)],
        compiler_params=_params(("parallel",)),
        name="postmix",
    )(oh, og, zm, zm, h, whu, wgu, wout, npost, npre)


def _ple_kernel(t_ref, h_ref, p_ref, wg_ref, wp_ref, npost_ref, nnext_ref, h3_ref, a_ref):
    h = h_ref[...] + _rms(t_ref[...], npost_ref[...])
    gate = _sigmoid(_dot(h.astype(bf16), wg_ref[...]))
    pe = _dot(p_ref[...].astype(bf16), wp_ref[...].astype(bf16))
    h3 = h + gate * pe
    h3_ref[...] = h3
    a_ref[...] = _rms(h3, nnext_ref[...]).astype(a_ref.dtype)


def _ple(t2, h1, p, wg, wp, npost, nnext, layer, next_layer):
    M, D = h1.shape
    P = p.shape[-1]
    tm = _row_tile(M, ROW_CAP["ple"])
    lay = lambda m: (layer, 0, 0)
    return pl.pallas_call(
        _ple_kernel,
        grid=(M // tm,),
        in_specs=[pl.BlockSpec((tm, D), lambda m: (m, 0)),
                  pl.BlockSpec((tm, D), lambda m: (m, 0)),
                  pl.BlockSpec((None, tm, P), lambda m: (layer, m, 0)),
                  _const_spec((None, D, D), lay),
                  _const_spec((None, P, D), lay),
                  pl.BlockSpec((None, 1, D), lay),
                  pl.BlockSpec((None, 1, D), lambda m: (next_layer, 0, 0))],
        out_specs=[pl.BlockSpec((tm, D), lambda m: (m, 0)),
                   pl.BlockSpec((tm, D), lambda m: (m, 0))],
        out_shape=[jax.ShapeDtypeStruct((M, D), f32),
                   jax.ShapeDtypeStruct((M, D), bf16)],
        compiler_params=_params(("parallel",)),
        name="ple",
    )(t2, h1, p, wg, wp, npost, nnext)


def kernel(x_prompt, x_sample, p_prompt, p_sample, state_hgrn, state_gla, norm_pre_mix,
           norm_post_mix, norm_pre_ffn, norm_post_ffn, w_in, lb_param, hgrn_norm, w_hgrn_up,
           w_gla_gate, b_gla_gate, gla_norm, w_gla_up, w_out, w_ff1, w_ff2, w_ple, w_ple_gate):
    B, L, D = x_prompt.shape
    Bs = x_sample.shape[0]
    depth = w_in.shape[0]
    _, _, HH, HK, HV = state_hgrn.shape
    _, _, GH, GK, GV = state_gla.shape
    HW, GKW, GVW = HH * HK, GH * GK, GH * GV
    R = w_gla_gate.shape[1]
    F = w_ff1.shape[2]
    assert HK == LANES and GK == LANES and HV % LANES == 0 and GV % LANES == 0
    assert HH * HV == HW and 2 * GKW == HW and GVW == HW and R <= LANES
    assert x_sample.shape[1] == 1 and L % CHUNK == 0 and D % HW == 0
    mixer_cols = 4 * HW + 2 * GKW + 2 * GVW

    assert w_in.shape[2] - mixer_cols >= LANES and (mixer_cols + R) % 8 == 0
    w_in_t = jnp.swapaxes(w_in, 1, 2)
    whu = w_hgrn_up.astype(bf16)
    wgu = w_gla_up.astype(bf16)
    wout = w_out.astype(bf16)
    wpg = w_ple_gate.astype(bf16)
    wgg = jnp.concatenate([w_gla_gate.astype(f32), jnp.zeros((depth, LANES - R, GKW), f32)], axis=1)

    lbs = jnp.cumsum(jax.nn.softmax(lb_param.astype(f32), axis=0), axis=0)
    lbs = (lbs - lbs[0:1]).reshape(depth, 1, HW)
    r3 = lambda t: t.astype(f32).reshape(depth, 1, t.shape[-1])
    n_pre_mix, n_post_mix, n_pre_ffn, n_post_ffn = map(
        r3, (norm_pre_mix, norm_post_mix, norm_pre_ffn, norm_post_ffn))
    hn, gn, bgg = r3(hgrn_norm), r3(gla_norm), r3(b_gla_gate)
    mall, lvl = _level_tables()
    seg_w = (HW, HW, HW, GKW, GKW, GVW, GVW)
    col_scale = jnp.asarray(np.concatenate(
        [np.full((1, w), v, np.float32) for w, v in zip(seg_w, (1, 1, 1, GK ** -0.5, 1, 1, 1))],
        axis=1))
    ep_main = (lambda n: (n == 0) | (n == 2) | (n == 5), _ep_silu, _ep_scale)
    tm_mg = _pick(D, (1024, 512, 256, 128))
    tf = _pick(F, (1024, 512, 256, 128))
    tn2 = _pick(D, (512, 256, 128))

    def in_projections(a):
        col_spec = pl.BlockSpec((1, HW), lambda n, m: (0, n))
        zb, = _proj(a, w_in_t, i, lambda n: (n + jnp.minimum(n, 1)) * HW, 6, HW, ep_main,
                    (col_scale,), (col_spec,), (bf16,), "in_proj", True)
        ld, kh = _proj(a, w_in_t, i, lambda n: HW, 1, HW, _ep_forget, (lbs,),
                       (pl.BlockSpec((None, 1, HW), lambda n, m: (i, 0, 0)),),
                       (f32, f32), "in_proj_forget", True)
        zm, = _proj(a, w_in_t, i, lambda n: mixer_cols + R + n * tm_mg, 2 * D // tm_mg, tm_mg,
                    _ep_sigmoid, (), (), (bf16,), "in_proj_merge", True)
        glr, = _proj(a, w_in_t, i, lambda n: mixer_cols, 1, LANES, _ep_copy, (), (), (f32,),
                     "in_proj_lowrank", True)
        return zb, ld, kh, zm, glr

    def dense_tail(h, oh, og, zm, p):
        h1, c = _postmix(oh, og, zm, h, whu, wgu, wout, n_post_mix, n_pre_ffn, i)
        u, = _proj(c, w_ff1, i, lambda n: n, F // tf, tf, _ep_relu2, (), (), (bf16,),
                   "ffn_up", False)
        t2, = _proj(u, w_ff2, i, lambda n: n, D // tn2, tn2, _ep_copy, (), (), (f32,),
                    "ffn_down", False, w_single=True)
        return _ple(t2, h1, p, wpg, w_ple, n_post_ffn, n_pre_mix, i, (i + 1) % depth)

    BL = B * L
    assert BL % 16 == 0 and Bs % 16 == 0
    h = jnp.concatenate([x_prompt.reshape(BL, D), x_sample.reshape(Bs, D)], axis=0)
    p = jnp.concatenate([p_prompt.reshape(depth, BL, -1), p_sample.reshape(depth, Bs, -1)], axis=1)
    a = _norm(h, n_pre_mix, 0)

    hgrn_p, gla_p = [], []
    hgrn_s = gla_s = None
    for i in range(depth):
        zb, ld, kh, zm, glr = in_projections(a)
        oh, sh = _hgrn_prompt(zb, ld, kh, hn, mall, lvl, i, B, L, HH, HK, HV)
        oh, hgrn_s = _hgrn_step(zb, ld, kh, hn, state_hgrn, oh, hgrn_s, i, BL, HH, HK, HV)
        og, sg = _gla_prompt(zb, glr, wgg, bgg, gn, mall, lvl, i, B, L, GH, GK, GV)
        og, gla_s = _gla_step(zb, glr, wgg, bgg, gn, state_gla, og, gla_s, i, BL, GH, GK, GV)
        h, a = dense_tail(h, oh, og, zm, p)
        hgrn_p.append(sh)
        gla_p.append(sg)

    return (h[:BL].reshape(B, L, D), h[BL:].reshape(Bs, 1, D),
            jnp.stack(hgrn_p).astype(state_hgrn.dtype), jnp.stack(gla_p).astype(state_gla.dtype),
            hgrn_s.astype(state_hgrn.dtype), gla_s.astype(state_gla.dtype))
```

```python
import functools

import numpy as np
import jax
import jax.numpy as jnp
from jax import lax
from jax.experimental import pallas as pl
from jax.experimental.pallas import tpu as pltpu

EPS = 1e-6
GLA_GATE_NORM = 16.0
LOG2E = 1.4426950408889634
LANES = 128
CHUNK = 64
N_LEVELS = 6
VMEM_LIMIT = 56 * 1024 * 1024
ROW_CAP = dict(norm=512, in_proj=2048, in_proj_forget=1024, in_proj_merge=2048,
               in_proj_lowrank=2048, ffn_up=2048, ffn_down=832, postmix=416, ple=512)

f32 = jnp.float32
bf16 = jnp.bfloat16


def _sigmoid(x):
    return 0.5 * jnp.tanh(0.5 * x) + 0.5


def _silu(x):
    hx = 0.5 * x
    return hx + hx * jnp.tanh(hx)


def _log_sigmoid(x):
    return jnp.minimum(x, 0.0) - jnp.log(1.0 + jnp.exp(-jnp.abs(x)))


def _rms(x, g):
    return x * lax.rsqrt(jnp.mean(x * x, axis=-1, keepdims=True) + EPS) * g


def _dot(a, b):
    return jnp.dot(a, b, preferred_element_type=f32)


def _dot_nt(a, b):
    return lax.dot_general(a, b, (((1,), (1,)), ((), ())), preferred_element_type=f32)


def _dot_tn(a, b):
    return lax.dot_general(a, b, (((0,), (0,)), ((), ())), preferred_element_type=f32)


def _split2(x):
    hi = x.astype(bf16)
    lo = (x - hi.astype(f32)).astype(bf16)
    return hi, lo


def _params(sem):
    return pltpu.CompilerParams(dimension_semantics=sem, vmem_limit_bytes=VMEM_LIMIT)


def _const_spec(shape, index_map):
    return pl.BlockSpec(shape, index_map, pipeline_mode=pl.Buffered(1))


def _pick(n, cands):
    for c in cands:
        if n % c == 0:
            return c
    return n


def _row_tile(m, cap):
    for t in range(min(cap, m), 0, -1):
        if m % t == 0 and t % 16 == 0:
            return t
    return m


def _level_tables():
    C = CHUNK
    t = np.arange(C)[:, None]
    j = np.arange(C)[None, :]
    mats = [(j <= t), (j > t)]
    lvl = np.full((C, C), -1, np.int32)
    lvl[t == j] = 0
    w = C // 2
    level = 1
    while w >= 1:
        start = (t // (2 * w)) * (2 * w)
        m = start + w - 1
        second = (t - start) >= w
        mats.append(np.where(second, (j > m) & (j <= t), (j > t) & (j <= m)))
        same = (t // (2 * w)) == (j // (2 * w))
        lvl[same & second & ((j - (j // (2 * w)) * (2 * w)) < w)] = level
        w //= 2
        level += 1
    mall = np.concatenate(mats[:-1], axis=0).astype(np.float32)
    return jnp.asarray(np.concatenate([mall, mall], axis=1), bf16), jnp.asarray(lvl)


def _norm_kernel(x_ref, g_ref, o_ref):
    o_ref[...] = _rms(x_ref[...], g_ref[...]).astype(o_ref.dtype)


def _norm(x, gains, layer):
    M, D = x.shape
    tm = _row_tile(M, ROW_CAP["norm"])
    return pl.pallas_call(
        _norm_kernel,
        grid=(M // tm,),
        in_specs=[pl.BlockSpec((tm, D), lambda m: (m, 0)),
                  pl.BlockSpec((None, 1, D), lambda m: (layer, 0, 0))],
        out_specs=pl.BlockSpec((tm, D), lambda m: (m, 0)),
        out_shape=jax.ShapeDtypeStruct((M, D), bf16),
        compiler_params=_params(("parallel",)),
        name="rmsnorm",
    )(x, gains)


def _proj_kernel(*refs, n_extra, epilogue, transposed, tn, slab):
    a_ref, w_ref = refs[0], refs[1]
    extra = refs[2:2 + n_extra]
    outs = refs[2 + n_extra:-1]
    wbf_ref = refs[-1]

    @pl.when(pl.program_id(1) == 0)
    def _():
        w = w_ref[0] if transposed else w_ref[...]
        wbf_ref[...] = w.astype(bf16)

    def body(ep):
        a = a_ref[...]
        for j in range(tn // slab):
            cols = slice(j * slab, (j + 1) * slab)
            if transposed:
                acc = _dot_nt(a, wbf_ref[cols, :])
            else:
                acc = _dot(a, wbf_ref[:, cols])
            ep(acc, cols, extra, outs)

    if isinstance(epilogue, tuple):
        pred_fn, ep_true, ep_false = epilogue
        pred = pred_fn(pl.program_id(0))
        pl.when(pred)(functools.partial(body, ep_true))
        pl.when(jnp.logical_not(pred))(functools.partial(body, ep_false))
    else:
        body(epilogue)


def _proj(a, w, layer, w_off, n_tiles, tn, epilogue, extra, extra_specs, out_dtypes, name,
          transposed, w_single=False):
    M, K = a.shape
    tm = _row_tile(M, ROW_CAP[name])
    if transposed:
        w_shape = (pl.Element(1), pl.Element(tn), pl.Element(K))
        def w_map(n, m):
            off = w_off(n)
            return (layer, off if isinstance(off, int) else pl.multiple_of(off, 8), 0)
        scratch = pltpu.VMEM((tn, K), bf16)
    else:
        w_shape = (None, K, tn)
        w_map = lambda n, m: (layer, 0, w_off(n))
        scratch = pltpu.VMEM((K, tn), bf16)
    w_spec = _const_spec(w_shape, w_map) if w_single else pl.BlockSpec(w_shape, w_map)
    kern = functools.partial(_proj_kernel, n_extra=len(extra), epilogue=epilogue,
                             transposed=transposed, tn=tn, slab=min(tn, 2 * LANES))
    return pl.pallas_call(
        kern,
        grid=(n_tiles, M // tm),
        in_specs=[pl.BlockSpec((tm, K), lambda n, m: (m, 0)), w_spec] + list(extra_specs),
        out_specs=[pl.BlockSpec((tm, tn), lambda n, m: (m, n)) for _ in out_dtypes],
        out_shape=[jax.ShapeDtypeStruct((M, n_tiles * tn), dt) for dt in out_dtypes],
        scratch_shapes=[scratch],
        compiler_params=_params(("parallel", "arbitrary")),
        name=name,
    )(a, w, *extra)


def _hgrn_gates(fl, lb):
    fl2 = fl * LOG2E
    e = jnp.exp2(-jnp.abs(fl2))
    t = 1.0 + e
    pos = fl >= 0.0
    num = jnp.where(pos, 1.0 + lb * e, lb + e)
    log2_num = jnp.where(pos | (lb > 0.0), jnp.log2(num), fl2)
    log2_f = log2_num - jnp.log2(t)
    r = 1.0 / t
    k = (1.0 - lb) * jnp.where(pos, e * r, r)
    return log2_f, k


def _ep_silu(acc, cols, extra, outs):
    outs[0][:, cols] = _silu(acc).astype(outs[0].dtype)


def _ep_scale(acc, cols, extra, outs):
    outs[0][:, cols] = (acc * extra[0][:, cols]).astype(outs[0].dtype)


def _ep_forget(acc, cols, extra, outs):
    ld_ref, k_ref = outs
    log2_f, k = _hgrn_gates(acc, extra[0][:, cols])
    ld_ref[:, cols] = log2_f
    k_ref[:, cols] = k


def _ep_sigmoid(acc, cols, extra, outs):
    outs[0][:, cols] = _sigmoid(acc).astype(outs[0].dtype)


def _ep_copy(acc, cols, extra, outs):
    outs[0][:, cols] = acc.astype(outs[0].dtype)


def _ep_relu2(acc, cols, extra, outs):
    outs[0][:, cols] = jnp.square(jnp.maximum(acc, 0.0)).astype(outs[0].dtype)


def _recurrence_chunks(T, H, K, V, q_ref, load_k, ld_ref, v_ref, g_ref, gain,
                       mall_ref, lvl_ref, st_ref, e_ref, o_ref):
    C = CHUNK

    def chunk(c, carry):
        r0 = pl.multiple_of(c * C, C)
        rows = pl.ds(r0, C)
        lvl = lvl_ref[...]
        hi, lo = _split2(ld_ref[rows, :])
        e_ref[...] = jnp.exp2(_dot(mall_ref[...], jnp.concatenate([hi, lo], axis=0)))
        heads = []
        for h in range(H):
            cs = slice(h * K, (h + 1) * K)
            vs = slice(h * V, (h + 1) * V)
            qc = q_ref[rows, cs].astype(f32)
            kc = load_k(rows, cs)
            vb = v_ref[rows, vs]
            st = st_ref[h]

            o = _dot_nt((qc * e_ref[0:C, cs]).astype(bf16), st.astype(bf16))

            diag = jnp.sum(qc * kc, axis=-1, keepdims=True)
            adj = jnp.sum(qc * pltpu.roll(kc, 1, axis=0) * jnp.exp2(ld_ref[rows, cs]),
                          axis=-1, keepdims=True)
            levels = []
            for l in range(1, N_LEVELS):
                el = e_ref[(l + 1) * C:(l + 2) * C, cs]
                levels.append(_dot_nt((qc * el).astype(bf16), (kc * el).astype(bf16)))

            kd = (kc * e_ref[C:2 * C, cs]).astype(bf16)
            alpha = e_ref[C - 1:C, cs]
            heads.append((h, vs, vb, st, kd, alpha, o, diag, adj, levels))

        outs = []
        for h, vs, vb, st, kd, alpha, o, diag, adj, levels in heads:
            a = jnp.where(lvl == 0, diag, jnp.where(lvl == N_LEVELS, adj, 0.0))
            for l, al in enumerate(levels, start=1):
                a = jnp.where(lvl == l, al, a)
            outs.append(o + _dot(a.astype(bf16), vb))

        for h, vs, vb, st, kd, alpha, o, diag, adj, levels in heads:
            st_ref[h] = st * alpha + _dot_tn(vb, kd)

        for (h, vs, *_), o in zip(heads, outs):
            gate = g_ref[rows, vs].astype(f32)
            o_ref[rows, vs] = (_rms(o, gain) * gate).astype(o_ref.dtype)
        return carry

    lax.fori_loop(0, T // C, chunk, 0, unroll=4)


def _write_state(tb, st_ref, s_ref, H):
    @pl.when(tb == pl.num_programs(1) - 1)
    def _():
        for h in range(H):
            s_ref[0, h] = st_ref[h].T


def _hgrn_prompt_kernel(zq_ref, zi_ref, zg_ref, ld_ref, k_ref, gain_ref, mall_ref,
                        lvl_ref, o_ref, s_ref, st_ref, e_ref, *, T, H, K, V):
    tb = pl.program_id(1)

    @pl.when(tb == 0)
    def _():
        st_ref[...] = jnp.zeros_like(st_ref)

    _recurrence_chunks(T, H, K, V, zq_ref, lambda rows, cs: k_ref[rows, cs], ld_ref,
                       zi_ref, zg_ref, gain_ref[...], mall_ref, lvl_ref, st_ref, e_ref, o_ref)
    _write_state(tb, st_ref, s_ref, H)


def _gla_log2_decay(glr, wg, bg):
    g_hi, g_lo = _split2(glr)
    w_hi, w_lo = _split2(wg)
    pre = _dot(g_hi, w_hi) + _dot(g_hi, w_lo) + _dot(g_lo, w_hi) + bg
    return _log_sigmoid(pre) * (LOG2E / GLA_GATE_NORM)


def _gla_prompt_kernel(gqk_ref, gv_ref, gr_ref, glr_ref, wg_ref, bg_ref, gain_ref, mall_ref,
                       lvl_ref, o_ref, s_ref, st_ref, ld_ref, e_ref, *, T, H, K, V):
    tb = pl.program_id(1)

    @pl.when(tb == 0)
    def _():
        st_ref[...] = jnp.zeros_like(st_ref)

    ld_ref[...] = _gla_log2_decay(glr_ref[...], wg_ref[...], bg_ref[...])
    KW = H * K
    k_ref = gqk_ref.at[:, KW:2 * KW]
    _recurrence_chunks(T, H, K, V, gqk_ref, lambda rows, cs: k_ref[rows, cs].astype(f32),
                       ld_ref, gv_ref, gr_ref, gain_ref[...], mall_ref, lvl_ref,
                       st_ref, e_ref, o_ref)
    _write_state(tb, st_ref, s_ref, H)


def _hgrn_prompt(zb, ld, kh, hgrn_norm, mall, lvl, layer, B, L, H, K, V):
    HW = H * K
    T = min(1024, L)
    nT = L // T
    tok = lambda col: (lambda b, t: (b * nT + t, col))
    cst = lambda b, t: (0, 0)
    kern = functools.partial(_hgrn_prompt_kernel, T=T, H=H, K=K, V=V)
    return pl.pallas_call(
        kern,
        grid=(B, nT),
        in_specs=[pl.BlockSpec((T, HW), tok(0)),
                  pl.BlockSpec((T, HW), tok(1)),
                  pl.BlockSpec((T, HW), tok(2)),
                  pl.BlockSpec((T, HW), tok(0)),
                  pl.BlockSpec((T, HW), tok(0)),
                  pl.BlockSpec((None, 1, V), lambda b, t: (layer, 0, 0)),
                  pl.BlockSpec(mall.shape, cst),
                  pl.BlockSpec(lvl.shape, cst)],
        out_specs=[pl.BlockSpec((T, HW), tok(0)),
                   pl.BlockSpec((1, H, K, V), lambda b, t: (b, 0, 0, 0))],
        out_shape=[jax.ShapeDtypeStruct((zb.shape[0], H * V), bf16),
                   jax.ShapeDtypeStruct((B, H, K, V), f32)],
        scratch_shapes=[pltpu.VMEM((H, V, K), f32),
                        pltpu.VMEM((mall.shape[0], HW), f32)],
        compiler_params=_params(("parallel", "arbitrary")),
        name="hgrn_prompt",
    )(zb, zb, zb, ld, kh, hgrn_norm, mall, lvl)


def _gla_prompt(zb, glr, wgg, bgg, gla_norm, mall, lvl, layer, B, L, H, K, V):
    KW, VW = H * K, H * V
    T = min(1024, L)
    nT = L // T
    tok = lambda col: (lambda b, t: (b * nT + t, col))
    cst = lambda b, t: (0, 0)
    kern = functools.partial(_gla_prompt_kernel, T=T, H=H, K=K, V=V)
    return pl.pallas_call(
        kern,
        grid=(B, nT),
        in_specs=[pl.BlockSpec((T, 2 * KW), tok(3)),
                  pl.BlockSpec((T, VW), tok(4)),
                  pl.BlockSpec((T, VW), tok(5)),
                  pl.BlockSpec((T, LANES), tok(0)),
                  pl.BlockSpec((None, LANES, KW), lambda b, t: (layer, 0, 0)),
                  pl.BlockSpec((None, 1, KW), lambda b, t: (layer, 0, 0)),
                  pl.BlockSpec((None, 1, V), lambda b, t: (layer, 0, 0)),
                  pl.BlockSpec(mall.shape, cst),
                  pl.BlockSpec(lvl.shape, cst)],
        out_specs=[pl.BlockSpec((T, VW), tok(0)),
                   pl.BlockSpec((1, H, K, V), lambda b, t: (b, 0, 0, 0))],
        out_shape=[jax.ShapeDtypeStruct((zb.shape[0], VW), bf16),
                   jax.ShapeDtypeStruct((B, H, K, V), f32)],
        scratch_shapes=[pltpu.VMEM((H, V, K), f32),
                        pltpu.VMEM((T, KW), f32),
                        pltpu.VMEM((mall.shape[0], KW), f32)],
        compiler_params=_params(("parallel", "arbitrary")),
        name="gla_prompt",
    )(zb, zb, zb, glr, wgg, bgg, gla_norm, mall, lvl)


def _step_rows(Bt, H, K, V, a_ref, k_ref, qa_ref, v_ref, s_in_ref, s_out_ref, oi_ref):
    r = lax.broadcasted_iota(jnp.int32, (K, K), 0)
    c = lax.broadcasted_iota(jnp.int32, (K, K), 1)
    eye = r == c
    ones = jnp.ones((K, LANES), bf16)
    nv = V // LANES

    def diag(x):
        return jnp.where(eye, jnp.broadcast_to(x, (K, K)), 0.0)

    def body(b, carry):
        pieces = []
        for h in range(H):
            cs = slice(h * K, (h + 1) * K)
            a = a_ref[b, :, cs]
            a_hi = a.astype(bf16).astype(f32)
            pieces += [diag(a_hi), diag(a - a_hi), diag(k_ref[b, :, cs]), diag(qa_ref[b, :, cs])]
        cb = _dot(jnp.concatenate(pieces, axis=0).astype(bf16), ones)
        for h in range(H):
            r0 = 4 * h * K
            a_col = cb[r0:r0 + K] + cb[r0 + K:r0 + 2 * K]
            k_col, q_col = cb[r0 + 2 * K:r0 + 3 * K], cb[r0 + 3 * K:r0 + 4 * K]
            for j in range(nv):
                ls = slice(h * V + j * LANES, h * V + (j + 1) * LANES)
                vj = slice(j * LANES, (j + 1) * LANES)
                s = s_in_ref[b, h, :, vj]
                s_out_ref[b, h, :, vj] = a_col * s + k_col * v_ref[b, :, ls]
                oi_ref[b, :, ls] = jnp.sum(q_col * s, axis=0, keepdims=True)
        return carry

    lax.fori_loop(0, Bt, body, 0, unroll=8)


def _store_rows(ref, x):
    for b in range(x.shape[0]):
        ref[b] = x[b:b + 1, :]


def _step_finish(H, K, V, q, k, v, g, gain, oi_ref, o_ref):
    ones = jnp.ones((K, LANES), bf16)
    nv = V // LANES
    for h in range(H):
        cs = slice(h * K, (h + 1) * K)
        vs = slice(h * V, (h + 1) * V)
        qk = _dot((q[:, cs] * k[:, cs]).astype(bf16), ones)
        if nv > 1:
            qk = jnp.concatenate([qk] * nv, axis=1)
        oi = jnp.concatenate([oi_ref[b, :, vs] for b in range(q.shape[0])], axis=0)
        o = qk * v[:, vs] + oi
        o_ref[:, vs] = (_rms(o, gain) * g[:, vs]).astype(o_ref.dtype)


def _step_common(Bt, H, K, V, a, q, k, v, g, gain, a_ref, k_ref, qa_ref, v_ref, oi_ref,
                 s_in_ref, s_out_ref, o_ref):
    _store_rows(a_ref, a)
    _store_rows(k_ref, k)
    _store_rows(qa_ref, q * a)
    _store_rows(v_ref, v)
    _step_rows(Bt, H, K, V, a_ref, k_ref, qa_ref, v_ref, s_in_ref, s_out_ref, oi_ref)
    _step_finish(H, K, V, q, k, v, g, gain, oi_ref, o_ref)


def _hgrn_step_kernel(*refs, Bt, H, K, V, aliased):
    (zq_ref, zi_ref, zg_ref, ld_ref, kh_ref, gain_ref, s_in_ref) = refs[:7]
    o_ref, s_out_ref, a_ref, k_ref, qa_ref, v_ref, oi_ref = refs[7 + aliased:]
    a = jnp.exp2(ld_ref[...])
    _step_common(Bt, H, K, V, a, zq_ref[...].astype(f32), kh_ref[...], zi_ref[...].astype(f32),
                 zg_ref[...].astype(f32), gain_ref[...], a_ref, k_ref, qa_ref, v_ref, oi_ref,
                 s_in_ref, s_out_ref, o_ref)


def _gla_step_kernel(*refs, Bt, H, K, V, aliased):
    (gqk_ref, gv_ref, gr_ref, glr_ref, wg_ref, bg_ref, gain_ref, s_in_ref) = refs[:8]
    o_ref, s_out_ref, a_ref, k_ref, qa_ref, v_ref, oi_ref = refs[8 + aliased:]
    KW = H * K
    a = jnp.exp2(_gla_log2_decay(glr_ref[...], wg_ref[...], bg_ref[...]))
    _step_common(Bt, H, K, V, a, gqk_ref[:, :KW].astype(f32), gqk_ref[:, KW:].astype(f32),
                 gv_ref[...].astype(f32), gr_ref[...].astype(f32), gain_ref[...],
                 a_ref, k_ref, qa_ref, v_ref, oi_ref, s_in_ref, s_out_ref, o_ref)


def _step_call(kern, name, ins, in_specs, state, o_prev, s_prev, layer, row0, Bt, H, K, V, KW):
    depth, Bs = state.shape[:2]
    VW = H * V
    blk0 = row0 // Bt
    in_specs = list(in_specs) + [pl.BlockSpec((None, Bt, H, K, V), lambda i: (layer, i, 0, 0, 0)),
                                 pl.BlockSpec(memory_space=pl.ANY)]
    args = list(ins) + [state, o_prev]
    aliases = {len(args) - 1: 0}
    if s_prev is not None:
        in_specs.append(pl.BlockSpec(memory_space=pl.ANY))
        args.append(s_prev)
        aliases[len(args) - 1] = 1
    return pl.pallas_call(
        functools.partial(kern, Bt=Bt, H=H, K=K, V=V, aliased=len(aliases)),
        grid=(Bs // Bt,),
        in_specs=in_specs,
        out_specs=[pl.BlockSpec((Bt, VW), lambda i: (blk0 + i, 0)),
                   pl.BlockSpec((None, Bt, H, K, V), lambda i: (layer, i, 0, 0, 0))],
        out_shape=[jax.ShapeDtypeStruct(o_prev.shape, bf16),
                   jax.ShapeDtypeStruct((depth, Bs, H, K, V), f32)],
        scratch_shapes=[pltpu.VMEM((Bt, 1, KW), f32)] * 3 + [pltpu.VMEM((Bt, 1, VW), f32)] * 2,
        input_output_aliases=aliases,
        compiler_params=_params(("arbitrary",)),
        name=name,
    )(*args)


def _hgrn_step(zb, ld, kh, hgrn_norm, state, o_prev, s_prev, layer, row0, H, K, V):
    HW = H * K
    Bt = min(16, state.shape[1])
    blk0 = row0 // Bt
    row = lambda col: (lambda i: (blk0 + i, col))
    in_specs = [pl.BlockSpec((Bt, HW), row(0)),
                pl.BlockSpec((Bt, HW), row(1)),
                pl.BlockSpec((Bt, HW), row(2)),
                pl.BlockSpec((Bt, HW), row(0)),
                pl.BlockSpec((Bt, HW), row(0)),
                pl.BlockSpec((None, 1, V), lambda i: (layer, 0, 0))]
    return _step_call(_hgrn_step_kernel, "hgrn_step", (zb, zb, zb, ld, kh, hgrn_norm),
                      in_specs, state, o_prev, s_prev, layer, row0, Bt, H, K, V, HW)


def _gla_step(zb, glr, wgg, bgg, gla_norm, state, o_prev, s_prev, layer, row0, H, K, V):
    KW, VW = H * K, H * V
    Bt = min(16, state.shape[1])
    blk0 = row0 // Bt
    row = lambda col: (lambda i: (blk0 + i, col))
    in_specs = [pl.BlockSpec((Bt, 2 * KW), row(3)),
                pl.BlockSpec((Bt, VW), row(4)),
                pl.BlockSpec((Bt, VW), row(5)),
                pl.BlockSpec((Bt, LANES), row(0)),
                pl.BlockSpec((None, LANES, KW), lambda i: (layer, 0, 0)),
                pl.BlockSpec((None, 1, KW), lambda i: (layer, 0, 0)),
                pl.BlockSpec((None, 1, V), lambda i: (layer, 0, 0))]
    return _step_call(_gla_step_kernel, "gla_step", (zb, zb, zb, glr, wgg, bgg, gla_norm),
                      in_specs, state, o_prev, s_prev, layer, row0, Bt, H, K, V, KW)


def _postmix_kernel(oh_ref, og_ref, mh_ref, mg_ref, h_ref, whu_ref, wgu_ref, wout_ref,
                    npost_ref, npre_ref, h1_ref, c_ref):
    yh = _dot(oh_ref[...], whu_ref[...])
    yg = _dot(og_ref[...], wgu_ref[...])
    merged = mh_ref[...].astype(f32) * yh + mg_ref[...].astype(f32) * yg
    t = _dot(merged.astype(bf16), wout_ref[...])
    h1 = h_ref[...] + _rms(t, npost_ref[...])
    h1_ref[...] = h1
    c_ref[...] = _rms(h1, npre_ref[...]).astype(c_ref.dtype)


def _postmix(oh, og, zm, h, whu, wgu, wout, npost, npre, layer):
    M, D = h.shape
    HW, VW = oh.shape[1], og.shape[1]
    tm = _row_tile(M, ROW_CAP["postmix"])
    lay = lambda m: (layer, 0, 0)
    return pl.pallas_call(
        _postmix_kernel,
        grid=(M // tm,),
        in_specs=[pl.BlockSpec((tm, HW), lambda m: (m, 0)),
                  pl.BlockSpec((tm, VW), lambda m: (m, 0)),
                  pl.BlockSpec((tm, D), lambda m: (m, 0)),
                  pl.BlockSpec((tm, D), lambda m: (m, 1)),
                  pl.BlockSpec((tm, D), lambda m: (m, 0)),
                  _const_spec((None, HW, D), lay),
                  _const_spec((None, VW, D), lay),
                  _const_spec((None, D, D), lay),
                  pl.BlockSpec((None, 1, D), lay),
                  pl.BlockSpec((None, 1, D), lay)],
        out_specs=[pl.BlockSpec((tm, D), lambda m: (m, 0)),
                   pl.BlockSpec((tm, D), lambda m: (m, 0))],
        out_shape=[jax.ShapeDtypeStruct((M, D), f32),
                   jax.ShapeDtypeStruct((M, D), bf16)],
        compiler_params=_params(("parallel",)),
        name="postmix",
    )(oh, og, zm, zm, h, whu, wgu, wout, npost, npre)


def _ple_kernel(t_ref, h_ref, p_ref, wg_ref, wp_ref, npost_ref, nnext_ref, h3_ref, a_ref):
    h = h_ref[...] + _rms(t_ref[...], npost_ref[...])
    gate = _sigmoid(_dot(h.astype(bf16), wg_ref[...]))
    pe = _dot(p_ref[...].astype(bf16), wp_ref[...].astype(bf16))
    h3 = h + gate * pe
    h3_ref[...] = h3
    a_ref[...] = _rms(h3, nnext_ref[...]).astype(a_ref.dtype)


def _ple(t2, h1, p, wg, wp, npost, nnext, layer, next_layer):
    M, D = h1.shape
    P = p.shape[-1]
    tm = _row_tile(M, ROW_CAP["ple"])
    lay = lambda m: (layer, 0, 0)
    return pl.pallas_call(
        _ple_kernel,
        grid=(M // tm,),
        in_specs=[pl.BlockSpec((tm, D), lambda m: (m, 0)),
                  pl.BlockSpec((tm, D), lambda m: (m, 0)),
                  pl.BlockSpec((None, tm, P), lambda m: (layer, m, 0)),
                  _const_spec((None, D, D), lay),
                  _const_spec((None, P, D), lay),
                  pl.BlockSpec((None, 1, D), lay),
                  pl.BlockSpec((None, 1, D), lambda m: (next_layer, 0, 0))],
        out_specs=[pl.BlockSpec((tm, D), lambda m: (m, 0)),
                   pl.BlockSpec((tm, D), lambda m: (m, 0))],
        out_shape=[jax.ShapeDtypeStruct((M, D), f32),
                   jax.ShapeDtypeStruct((M, D), bf16)],
        compiler_params=_params(("parallel",)),
        name="ple",
    )(t2, h1, p, wg, wp, npost, nnext)


def kernel(x_prompt, x_sample, p_prompt, p_sample, state_hgrn, state_gla, norm_pre_mix,
           norm_post_mix, norm_pre_ffn, norm_post_ffn, w_in, lb_param, hgrn_norm, w_hgrn_up,
           w_gla_gate, b_gla_gate, gla_norm, w_gla_up, w_out, w_ff1, w_ff2, w_ple, w_ple_gate):
    B, L, D = x_prompt.shape
    Bs = x_sample.shape[0]
    depth = w_in.shape[0]
    _, _, HH, HK, HV = state_hgrn.shape
    _, _, GH, GK, GV = state_gla.shape
    HW, GKW, GVW = HH * HK, GH * GK, GH * GV
    R = w_gla_gate.shape[1]
    F = w_ff1.shape[2]
    assert HK == LANES and GK == LANES and HV % LANES == 0 and GV % LANES == 0
    assert HH * HV == HW and 2 * GKW == HW and GVW == HW and R <= LANES
    assert x_sample.shape[1] == 1 and L % CHUNK == 0 and D % HW == 0
    mixer_cols = 4 * HW + 2 * GKW + 2 * GVW

    assert w_in.shape[2] - mixer_cols >= LANES and (mixer_cols + R) % 8 == 0
    w_in_t = jnp.swapaxes(w_in, 1, 2)
    whu = w_hgrn_up.astype(bf16)
    wgu = w_gla_up.astype(bf16)
    wout = w_out.astype(bf16)
    wpg = w_ple_gate.astype(bf16)
    wgg = jnp.concatenate([w_gla_gate.astype(f32), jnp.zeros((depth, LANES - R, GKW), f32)], axis=1)

    lbs = jnp.cumsum(jax.nn.softmax(lb_param.astype(f32), axis=0), axis=0)
    lbs = (lbs - lbs[0:1]).reshape(depth, 1, HW)
    r3 = lambda t: t.astype(f32).reshape(depth, 1, t.shape[-1])
    n_pre_mix, n_post_mix, n_pre_ffn, n_post_ffn = map(
        r3, (norm_pre_mix, norm_post_mix, norm_pre_ffn, norm_post_ffn))
    hn, gn, bgg = r3(hgrn_norm), r3(gla_norm), r3(b_gla_gate)
    mall, lvl = _level_tables()
    seg_w = (HW, HW, HW, GKW, GKW, GVW, GVW)
    col_scale = jnp.asarray(np.concatenate(
        [np.full((1, w), v, np.float32) for w, v in zip(seg_w, (1, 1, 1, GK ** -0.5, 1, 1, 1))],
        axis=1))
    ep_main = (lambda n: (n == 0) | (n == 2) | (n == 5), _ep_silu, _ep_scale)
    tm_mg = _pick(D, (1024, 512, 256, 128))
    tf = _pick(F, (1024, 512, 256, 128))
    tn2 = _pick(D, (512, 256, 128))

    def in_projections(a):
        col_spec = pl.BlockSpec((1, HW), lambda n, m: (0, n))
        zb, = _proj(a, w_in_t, i, lambda n: (n + jnp.minimum(n, 1)) * HW, 6, HW, ep_main,
                    (col_scale,), (col_spec,), (bf16,), "in_proj", True)
        ld, kh = _proj(a, w_in_t, i, lambda n: HW, 1, HW, _ep_forget, (lbs,),
                       (pl.BlockSpec((None, 1, HW), lambda n, m: (i, 0, 0)),),
                       (f32, f32), "in_proj_forget", True)
        zm, = _proj(a, w_in_t, i, lambda n: mixer_cols + R + n * tm_mg, 2 * D // tm_mg, tm_mg,
                    _ep_sigmoid, (), (), (bf16,), "in_proj_merge", True)
        glr, = _proj(a, w_in_t, i, lambda n: mixer_cols, 1, LANES, _ep_copy, (), (), (f32,),
                     "in_proj_lowrank", True)
        return zb, ld, kh, zm, glr

    def dense_tail(h, oh, og, zm, p):
        h1, c = _postmix(oh, og, zm, h, whu, wgu, wout, n_post_mix, n_pre_ffn, i)
        u, = _proj(c, w_ff1, i, lambda n: n, F // tf, tf, _ep_relu2, (), (), (bf16,),
                   "ffn_up", False)
        t2, = _proj(u, w_ff2, i, lambda n: n, D // tn2, tn2, _ep_copy, (), (), (f32,),
                    "ffn_down", False, w_single=True)
        return _ple(t2, h1, p, wpg, w_ple, n_post_ffn, n_pre_mix, i, (i + 1) % depth)

    BL = B * L
    assert BL % 16 == 0 and Bs % 16 == 0
    h = jnp.concatenate([x_prompt.reshape(BL, D), x_sample.reshape(Bs, D)], axis=0)
    p = jnp.concatenate([p_prompt.reshape(depth, BL, -1), p_sample.reshape(depth, Bs, -1)], axis=1)
    a = _norm(h, n_pre_mix, 0)

    hgrn_p, gla_p = [], []
    hgrn_s = gla_s = None
    for i in range(depth):
        zb, ld, kh, zm, glr = in_projections(a)
        oh, sh = _hgrn_prompt(zb, ld, kh, hn, mall, lvl, i, B, L, HH, HK, HV)
        oh, hgrn_s = _hgrn_step(zb, ld, kh, hn, state_hgrn, oh, hgrn_s, i, BL, HH, HK, HV)
        og, sg = _gla_prompt(zb, glr, wgg, bgg, gn, mall, lvl, i, B, L, GH, GK, GV)
        og, gla_s = _gla_step(zb, glr, wgg, bgg, gn, state_gla, og, gla_s, i, BL, GH, GK, GV)
        h, a = dense_tail(h, oh, og, zm, p)
        hgrn_p.append(sh)
        gla_p.append(sg)

    return (h[:BL].reshape(B, L, D), h[BL:].reshape(Bs, 1, D),
            jnp.stack(hgrn_p).astype(state_hgrn.dtype), jnp.stack(gla_p).astype(state_gla.dtype),
            hgrn_s.astype(state_hgrn.dtype), gla_s.astype(state_gla.dtype))
```

```python
import functools

import numpy as np
import jax
import jax.numpy as jnp
from jax import lax
from jax.experimental import pallas as pl
from jax.experimental.pallas import tpu as pltpu

EPS = 1e-6
GLA_GATE_NORM = 16.0
LOG2E = 1.4426950408889634
LANES = 128
CHUNK = 64
N_LEVELS = 6
VMEM_LIMIT = 56 * 1024 * 1024
ROW_CAP = dict(norm=512, in_proj=2048, in_proj_forget=1024, in_proj_merge=2048,
               in_proj_lowrank=2048, ffn_up=2048, ffn_down=832, postmix=416, ple=512)

f32 = jnp.float32
bf16 = jnp.bfloat16


def _sigmoid(x):
    return 0.5 * jnp.tanh(0.5 * x) + 0.5


def _silu(x):
    hx = 0.5 * x
    return hx + hx * jnp.tanh(hx)


def _log_sigmoid(x):
    return jnp.minimum(x, 0.0) - jnp.log(1.0 + jnp.exp(-jnp.abs(x)))


def _rms(x, g):
    return x * lax.rsqrt(jnp.mean(x * x, axis=-1, keepdims=True) + EPS) * g


def _dot(a, b):
    return jnp.dot(a, b, preferred_element_type=f32)


def _dot_nt(a, b):
    return lax.dot_general(a, b, (((1,), (1,)), ((), ())), preferred_element_type=f32)


def _dot_tn(a, b):
    return lax.dot_general(a, b, (((0,), (0,)), ((), ())), preferred_element_type=f32)


def _split2(x):
    hi = x.astype(bf16)
    lo = (x - hi.astype(f32)).astype(bf16)
    return hi, lo


def _params(sem):
    return pltpu.CompilerParams(dimension_semantics=sem, vmem_limit_bytes=VMEM_LIMIT)


def _const_spec(shape, index_map):
    return pl.BlockSpec(shape, index_map, pipeline_mode=pl.Buffered(1))


def _pick(n, cands):
    for c in cands:
        if n % c == 0:
            return c
    return n


def _row_tile(m, cap):
    for t in range(min(cap, m), 0, -1):
        if m % t == 0 and t % 16 == 0:
            return t
    return m


def _level_tables():
    C = CHUNK
    t = np.arange(C)[:, None]
    j = np.arange(C)[None, :]
    mats = [(j <= t), (j > t)]
    lvl = np.full((C, C), -1, np.int32)
    lvl[t == j] = 0
    w = C // 2
    level = 1
    while w >= 1:
        start = (t // (2 * w)) * (2 * w)
        m = start + w - 1
        second = (t - start) >= w
        mats.append(np.where(second, (j > m) & (j <= t), (j > t) & (j <= m)))
        same = (t // (2 * w)) == (j // (2 * w))
        lvl[same & second & ((j - (j // (2 * w)) * (2 * w)) < w)] = level
        w //= 2
        level += 1
    mall = np.concatenate(mats[:-1], axis=0).astype(np.float32)
    return jnp.asarray(np.concatenate([mall, mall], axis=1), bf16), jnp.asarray(lvl)


def _norm_kernel(x_ref, g_ref, o_ref):
    o_ref[...] = _rms(x_ref[...], g_ref[...]).astype(o_ref.dtype)


def _norm(x, gains, layer):
    M, D = x.shape
    tm = _row_tile(M, ROW_CAP["norm"])
    return pl.pallas_call(
        _norm_kernel,
        grid=(M // tm,),
        in_specs=[pl.BlockSpec((tm, D), lambda m: (m, 0)),
                  pl.BlockSpec((None, 1, D), lambda m: (layer, 0, 0))],
        out_specs=pl.BlockSpec((tm, D), lambda m: (m, 0)),
        out_shape=jax.ShapeDtypeStruct((M, D), bf16),
        compiler_params=_params(("parallel",)),
        name="rmsnorm",
    )(x, gains)


def _proj_kernel(*refs, n_extra, epilogue, transposed, tn, slab):
    a_ref, w_ref = refs[0], refs[1]
    extra = refs[2:2 + n_extra]
    outs = refs[2 + n_extra:-1]
    wbf_ref = refs[-1]

    @pl.when(pl.program_id(1) == 0)
    def _():
        w = w_ref[0] if transposed else w_ref[...]
        wbf_ref[...] = w.astype(bf16)

    def body(ep):
        a = a_ref[...]
        for j in range(tn // slab):
            cols = slice(j * slab, (j + 1) * slab)
            if transposed:
                acc = _dot_nt(a, wbf_ref[cols, :])
            else:
                acc = _dot(a, wbf_ref[:, cols])
            ep(acc, cols, extra, outs)

    if isinstance(epilogue, tuple):
        pred_fn, ep_true, ep_false = epilogue
        pred = pred_fn(pl.program_id(0))
        pl.when(pred)(functools.partial(body, ep_true))
        pl.when(jnp.logical_not(pred))(functools.partial(body, ep_false))
    else:
        body(epilogue)


def _proj(a, w, layer, w_off, n_tiles, tn, epilogue, extra, extra_specs, out_dtypes, name,
          transposed, w_single=False):
    M, K = a.shape
    tm = _row_tile(M, ROW_CAP[name])
    if transposed:
        w_shape = (pl.Element(1), pl.Element(tn), pl.Element(K))
        def w_map(n, m):
            off = w_off(n)
            return (layer, off if isinstance(off, int) else pl.multiple_of(off, 8), 0)
        scratch = pltpu.VMEM((tn, K), bf16)
    else:
        w_shape = (None, K, tn)
        w_map = lambda n, m: (layer, 0, w_off(n))
        scratch = pltpu.VMEM((K, tn), bf16)
    w_spec = _const_spec(w_shape, w_map) if w_single else pl.BlockSpec(w_shape, w_map)
    kern = functools.partial(_proj_kernel, n_extra=len(extra), epilogue=epilogue,
                             transposed=transposed, tn=tn, slab=min(tn, 2 * LANES))
    return pl.pallas_call(
        kern,
        grid=(n_tiles, M // tm),
        in_specs=[pl.BlockSpec((tm, K), lambda n, m: (m, 0)), w_spec] + list(extra_specs),
        out_specs=[pl.BlockSpec((tm, tn), lambda n, m: (m, n)) for _ in out_dtypes],
        out_shape=[jax.ShapeDtypeStruct((M, n_tiles * tn), dt) for dt in out_dtypes],
        scratch_shapes=[scratch],
        compiler_params=_params(("parallel", "arbitrary")),
        name=name,
    )(a, w, *extra)


def _hgrn_gates(fl, lb):
    fl2 = fl * LOG2E
    e = jnp.exp2(-jnp.abs(fl2))
    t = 1.0 + e
    pos = fl >= 0.0
    num = jnp.where(pos, 1.0 + lb * e, lb + e)
    log2_num = jnp.where(pos | (lb > 0.0), jnp.log2(num), fl2)
    log2_f = log2_num - jnp.log2(t)
    r = 1.0 / t
    k = (1.0 - lb) * jnp.where(pos, e * r, r)
    return log2_f, k


def _ep_silu(acc, cols, extra, outs):
    outs[0][:, cols] = _silu(acc).astype(outs[0].dtype)


def _ep_scale(acc, cols, extra, outs):
    outs[0][:, cols] = (acc * extra[0][:, cols]).astype(outs[0].dtype)


def _ep_forget(acc, cols, extra, outs):
    ld_ref, k_ref = outs
    log2_f, k = _hgrn_gates(acc, extra[0][:, cols])
    ld_ref[:, cols] = log2_f
    k_ref[:, cols] = k


def _ep_sigmoid(acc, cols, extra, outs):
    outs[0][:, cols] = _sigmoid(acc).astype(outs[0].dtype)


def _ep_copy(acc, cols, extra, outs):
    outs[0][:, cols] = acc.astype(outs[0].dtype)


def _ep_relu2(acc, cols, extra, outs):
    outs[0][:, cols] = jnp.square(jnp.maximum(acc, 0.0)).astype(outs[0].dtype)


def _recurrence_chunks(T, H, K, V, q_ref, load_k, ld_ref, v_ref, g_ref, gain,
                       mall_ref, lvl_ref, st_ref, e_ref, o_ref):
    C = CHUNK

    def chunk(c, carry):
        r0 = pl.multiple_of(c * C, C)
        rows = pl.ds(r0, C)
        lvl = lvl_ref[...]
        hi, lo = _split2(ld_ref[rows, :])
        e_ref[...] = jnp.exp2(_dot(mall_ref[...], jnp.concatenate([hi, lo], axis=0)))
        heads = []
        for h in range(H):
            cs = slice(h * K, (h + 1) * K)
            vs = slice(h * V, (h + 1) * V)
            qc = q_ref[rows, cs].astype(f32)
            kc = load_k(rows, cs)
            vb = v_ref[rows, vs]
            st = st_ref[h]

            o = _dot_nt((qc * e_ref[0:C, cs]).astype(bf16), st.astype(bf16))

            diag = jnp.sum(qc * kc, axis=-1, keepdims=True)
            adj = jnp.sum(qc * pltpu.roll(kc, 1, axis=0) * jnp.exp2(ld_ref[rows, cs]),
                          axis=-1, keepdims=True)
            levels = []
            for l in range(1, N_LEVELS):
                el = e_ref[(l + 1) * C:(l + 2) * C, cs]
                levels.append(_dot_nt((qc * el).astype(bf16), (kc * el).astype(bf16)))

            kd = (kc * e_ref[C:2 * C, cs]).astype(bf16)
            alpha = e_ref[C - 1:C, cs]
            heads.append((h, vs, vb, st, kd, alpha, o, diag, adj, levels))

        outs = []
        for h, vs, vb, st, kd, alpha, o, diag, adj, levels in heads:
            a = jnp.where(lvl == 0, diag, jnp.where(lvl == N_LEVELS, adj, 0.0))
            for l, al in enumerate(levels, start=1):
                a = jnp.where(lvl == l, al, a)
            outs.append(o + _dot(a.astype(bf16), vb))

        for h, vs, vb, st, kd, alpha, o, diag, adj, levels in heads:
            st_ref[h] = st * alpha + _dot_tn(vb, kd)

        for (h, vs, *_), o in zip(heads, outs):
            gate = g_ref[rows, vs].astype(f32)
            o_ref[rows, vs] = (_rms(o, gain) * gate).astype(o_ref.dtype)
        return carry

    lax.fori_loop(0, T // C, chunk, 0, unroll=4)


def _write_state(tb, st_ref, s_ref, H):
    @pl.when(tb == pl.num_programs(1) - 1)
    def _():
        for h in range(H):
            s_ref[0, h] = st_ref[h].T


def _hgrn_prompt_kernel(zq_ref, zi_ref, zg_ref, ld_ref, k_ref, gain_ref, mall_ref,
                        lvl_ref, o_ref, s_ref, st_ref, e_ref, *, T, H, K, V):
    tb = pl.program_id(1)

    @pl.when(tb == 0)
    def _():
        st_ref[...] = jnp.zeros_like(st_ref)

    _recurrence_chunks(T, H, K, V, zq_ref, lambda rows, cs: k_ref[rows, cs], ld_ref,
                       zi_ref, zg_ref, gain_ref[...], mall_ref, lvl_ref, st_ref, e_ref, o_ref)
    _write_state(tb, st_ref, s_ref, H)


def _gla_log2_decay(glr, wg, bg):
    g_hi, g_lo = _split2(glr)
    w_hi, w_lo = _split2(wg)
    pre = _dot(g_hi, w_hi) + _dot(g_hi, w_lo) + _dot(g_lo, w_hi) + bg
    return _log_sigmoid(pre) * (LOG2E / GLA_GATE_NORM)


def _gla_prompt_kernel(gqk_ref, gv_ref, gr_ref, glr_ref, wg_ref, bg_ref, gain_ref, mall_ref,
                       lvl_ref, o_ref, s_ref, st_ref, ld_ref, e_ref, *, T, H, K, V):
    tb = pl.program_id(1)

    @pl.when(tb == 0)
    def _():
        st_ref[...] = jnp.zeros_like(st_ref)

    ld_ref[...] = _gla_log2_decay(glr_ref[...], wg_ref[...], bg_ref[...])
    KW = H * K
    k_ref = gqk_ref.at[:, KW:2 * KW]
    _recurrence_chunks(T, H, K, V, gqk_ref, lambda rows, cs: k_ref[rows, cs].astype(f32),
                       ld_ref, gv_ref, gr_ref, gain_ref[...], mall_ref, lvl_ref,
                       st_ref, e_ref, o_ref)
    _write_state(tb, st_ref, s_ref, H)


def _hgrn_prompt(zb, ld, kh, hgrn_norm, mall, lvl, layer, B, L, H, K, V):
    HW = H * K
    T = min(1024, L)
    nT = L // T
    tok = lambda col: (lambda b, t: (b * nT + t, col))
    cst = lambda b, t: (0, 0)
    kern = functools.partial(_hgrn_prompt_kernel, T=T, H=H, K=K, V=V)
    return pl.pallas_call(
        kern,
        grid=(B, nT),
        in_specs=[pl.BlockSpec((T, HW), tok(0)),
                  pl.BlockSpec((T, HW), tok(1)),
                  pl.BlockSpec((T, HW), tok(2)),
                  pl.BlockSpec((T, HW), tok(0)),
                  pl.BlockSpec((T, HW), tok(0)),
                  pl.BlockSpec((None, 1, V), lambda b, t: (layer, 0, 0)),
                  pl.BlockSpec(mall.shape, cst),
                  pl.BlockSpec(lvl.shape, cst)],
        out_specs=[pl.BlockSpec((T, HW), tok(0)),
                   pl.BlockSpec((1, H, K, V), lambda b, t: (b, 0, 0, 0))],
        out_shape=[jax.ShapeDtypeStruct((zb.shape[0], H * V), bf16),
                   jax.ShapeDtypeStruct((B, H, K, V), f32)],
        scratch_shapes=[pltpu.VMEM((H, V, K), f32),
                        pltpu.VMEM((mall.shape[0], HW), f32)],
        compiler_params=_params(("parallel", "arbitrary")),
        name="hgrn_prompt",
    )(zb, zb, zb, ld, kh, hgrn_norm, mall, lvl)


def _gla_prompt(zb, glr, wgg, bgg, gla_norm, mall, lvl, layer, B, L, H, K, V):
    KW, VW = H * K, H * V
    T = min(1024, L)
    nT = L // T
    tok = lambda col: (lambda b, t: (b * nT + t, col))
    cst = lambda b, t: (0, 0)
    kern = functools.partial(_gla_prompt_kernel, T=T, H=H, K=K, V=V)
    return pl.pallas_call(
        kern,
        grid=(B, nT),
        in_specs=[pl.BlockSpec((T, 2 * KW), tok(3)),
                  pl.BlockSpec((T, VW), tok(4)),
                  pl.BlockSpec((T, VW), tok(5)),
                  pl.BlockSpec((T, LANES), tok(0)),
                  pl.BlockSpec((None, LANES, KW), lambda b, t: (layer, 0, 0)),
                  pl.BlockSpec((None, 1, KW), lambda b, t: (layer, 0, 0)),
                  pl.BlockSpec((None, 1, V), lambda b, t: (layer, 0, 0)),
                  pl.BlockSpec(mall.shape, cst),
                  pl.BlockSpec(lvl.shape, cst)],
        out_specs=[pl.BlockSpec((T, VW), tok(0)),
                   pl.BlockSpec((1, H, K, V), lambda b, t: (b, 0, 0, 0))],
        out_shape=[jax.ShapeDtypeStruct((zb.shape[0], VW), bf16),
                   jax.ShapeDtypeStruct((B, H, K, V), f32)],
        scratch_shapes=[pltpu.VMEM((H, V, K), f32),
                        pltpu.VMEM((T, KW), f32),
                        pltpu.VMEM((mall.shape[0], KW), f32)],
        compiler_params=_params(("parallel", "arbitrary")),
        name="gla_prompt",
    )(zb, zb, zb, glr, wgg, bgg, gla_norm, mall, lvl)


def _step_rows(Bt, H, K, V, a_ref, k_ref, qa_ref, v_ref, s_in_ref, s_out_ref, oi_ref):
    r = lax.broadcasted_iota(jnp.int32, (K, K), 0)
    c = lax.broadcasted_iota(jnp.int32, (K, K), 1)
    eye = r == c
    ones = jnp.ones((K, LANES), bf16)
    nv = V // LANES

    def diag(x):
        return jnp.where(eye, jnp.broadcast_to(x, (K, K)), 0.0)

    def body(b, carry):
        pieces = []
        for h in range(H):
            cs = slice(h * K, (h + 1) * K)
            a = a_ref[b, :, cs]
            a_hi = a.astype(bf16).astype(f32)
            pieces += [diag(a_hi), diag(a - a_hi), diag(k_ref[b, :, cs]), diag(qa_ref[b, :, cs])]
        cb = _dot(jnp.concatenate(pieces, axis=0).astype(bf16), ones)
        for h in range(H):
            r0 = 4 * h * K
            a_col = cb[r0:r0 + K] + cb[r0 + K:r0 + 2 * K]
            k_col, q_col = cb[r0 + 2 * K:r0 + 3 * K], cb[r0 + 3 * K:r0 + 4 * K]
            for j in range(nv):
                ls = slice(h * V + j * LANES, h * V + (j + 1) * LANES)
                vj = slice(j * LANES, (j + 1) * LANES)
                s = s_in_ref[b, h, :, vj]
                s_out_ref[b, h, :, vj] = a_col * s + k_col * v_ref[b, :, ls]
                oi_ref[b, :, ls] = jnp.sum(q_col * s, axis=0, keepdims=True)
        return carry

    lax.fori_loop(0, Bt, body, 0, unroll=True)


def _store_rows(ref, x):
    for b in range(x.shape[0]):
        ref[b] = x[b:b + 1, :]


def _step_finish(H, K, V, q, k, v, g, gain, oi_ref, o_ref):
    ones = jnp.ones((K, LANES), bf16)
    nv = V // LANES
    for h in range(H):
        cs = slice(h * K, (h + 1) * K)
        vs = slice(h * V, (h + 1) * V)
        qk = _dot((q[:, cs] * k[:, cs]).astype(bf16), ones)
        if nv > 1:
            qk = jnp.concatenate([qk] * nv, axis=1)
        oi = jnp.concatenate([oi_ref[b, :, vs] for b in range(q.shape[0])], axis=0)
        o = qk * v[:, vs] + oi
        o_ref[:, vs] = (_rms(o, gain) * g[:, vs]).astype(o_ref.dtype)


def _step_common(Bt, H, K, V, a, q, k, v, g, gain, a_ref, k_ref, qa_ref, v_ref, oi_ref,
                 s_in_ref, s_out_ref, o_ref):
    _store_rows(a_ref, a)
    _store_rows(k_ref, k)
    _store_rows(qa_ref, q * a)
    _store_rows(v_ref, v)
    _step_rows(Bt, H, K, V, a_ref, k_ref, qa_ref, v_ref, s_in_ref, s_out_ref, oi_ref)
    _step_finish(H, K, V, q, k, v, g, gain, oi_ref, o_ref)


def _hgrn_step_kernel(*refs, Bt, H, K, V, aliased):
    (zq_ref, zi_ref, zg_ref, ld_ref, kh_ref, gain_ref, s_in_ref) = refs[:7]
    o_ref, s_out_ref, a_ref, k_ref, qa_ref, v_ref, oi_ref = refs[7 + aliased:]
    a = jnp.exp2(ld_ref[...])
    _step_common(Bt, H, K, V, a, zq_ref[...].astype(f32), kh_ref[...], zi_ref[...].astype(f32),
                 zg_ref[...].astype(f32), gain_ref[...], a_ref, k_ref, qa_ref, v_ref, oi_ref,
                 s_in_ref, s_out_ref, o_ref)


def _gla_step_kernel(*refs, Bt, H, K, V, aliased):
    (gqk_ref, gv_ref, gr_ref, glr_ref, wg_ref, bg_ref, gain_ref, s_in_ref) = refs[:8]
    o_ref, s_out_ref, a_ref, k_ref, qa_ref, v_ref, oi_ref = refs[8 + aliased:]
    KW = H * K
    a = jnp.exp2(_gla_log2_decay(glr_ref[...], wg_ref[...], bg_ref[...]))
    _step_common(Bt, H, K, V, a, gqk_ref[:, :KW].astype(f32), gqk_ref[:, KW:].astype(f32),
                 gv_ref[...].astype(f32), gr_ref[...].astype(f32), gain_ref[...],
                 a_ref, k_ref, qa_ref, v_ref, oi_ref, s_in_ref, s_out_ref, o_ref)


def _step_call(kern, name, ins, in_specs, state, o_prev, s_prev, layer, row0, Bt, H, K, V, KW):
    depth, Bs = state.shape[:2]
    VW = H * V
    blk0 = row0 // Bt
    in_specs = list(in_specs) + [pl.BlockSpec((None, Bt, H, K, V), lambda i: (layer, i, 0, 0, 0)),
                                 pl.BlockSpec(memory_space=pl.ANY)]
    args = list(ins) + [state, o_prev]
    aliases = {len(args) - 1: 0}
    if s_prev is not None:
        in_specs.append(pl.BlockSpec(memory_space=pl.ANY))
        args.append(s_prev)
        aliases[len(args) - 1] = 1
    return pl.pallas_call(
        functools.partial(kern, Bt=Bt, H=H, K=K, V=V, aliased=len(aliases)),
        grid=(Bs // Bt,),
        in_specs=in_specs,
        out_specs=[pl.BlockSpec((Bt, VW), lambda i: (blk0 + i, 0)),
                   pl.BlockSpec((None, Bt, H, K, V), lambda i: (layer, i, 0, 0, 0))],
        out_shape=[jax.ShapeDtypeStruct(o_prev.shape, bf16),
                   jax.ShapeDtypeStruct((depth, Bs, H, K, V), f32)],
        scratch_shapes=[pltpu.VMEM((Bt, 1, KW), f32)] * 3 + [pltpu.VMEM((Bt, 1, VW), f32)] * 2,
        input_output_aliases=aliases,
        compiler_params=_params(("arbitrary",)),
        name=name,
    )(*args)


def _hgrn_step(zb, ld, kh, hgrn_norm, state, o_prev, s_prev, layer, row0, H, K, V):
    HW = H * K
    Bt = min(16, state.shape[1])
    blk0 = row0 // Bt
    row = lambda col: (lambda i: (blk0 + i, col))
    in_specs = [pl.BlockSpec((Bt, HW), row(0)),
                pl.BlockSpec((Bt, HW), row(1)),
                pl.BlockSpec((Bt, HW), row(2)),
                pl.BlockSpec((Bt, HW), row(0)),
                pl.BlockSpec((Bt, HW), row(0)),
                pl.BlockSpec((None, 1, V), lambda i: (layer, 0, 0))]
    return _step_call(_hgrn_step_kernel, "hgrn_step", (zb, zb, zb, ld, kh, hgrn_norm),
                      in_specs, state, o_prev, s_prev, layer, row0, Bt, H, K, V, HW)


def _gla_step(zb, glr, wgg, bgg, gla_norm, state, o_prev, s_prev, layer, row0, H, K, V):
    KW, VW = H * K, H * V
    Bt = min(16, state.shape[1])
    blk0 = row0 // Bt
    row = lambda col: (lambda i: (blk0 + i, col))
    in_specs = [pl.BlockSpec((Bt, 2 * KW), row(3)),
                pl.BlockSpec((Bt, VW), row(4)),
                pl.BlockSpec((Bt, VW), row(5)),
                pl.BlockSpec((Bt, LANES), row(0)),
                pl.BlockSpec((None, LANES, KW), lambda i: (layer, 0, 0)),
                pl.BlockSpec((None, 1, KW), lambda i: (layer, 0, 0)),
                pl.BlockSpec((None, 1, V), lambda i: (layer, 0, 0))]
    return _step_call(_gla_step_kernel, "gla_step", (zb, zb, zb, glr, wgg, bgg, gla_norm),
                      in_specs, state, o_prev, s_prev, layer, row0, Bt, H, K, V, KW)


def _postmix_kernel(oh_ref, og_ref, mh_ref, mg_ref, h_ref, whu_ref, wgu_ref, wout_ref,
                    npost_ref, npre_ref, h1_ref, c_ref):
    yh = _dot(oh_ref[...], whu_ref[...])
    yg = _dot(og_ref[...], wgu_ref[...])
    merged = mh_ref[...].astype(f32) * yh + mg_ref[...].astype(f32) * yg
    t = _dot(merged.astype(bf16), wout_ref[...])
    h1 = h_ref[...] + _rms(t, npost_ref[...])
    h1_ref[...] = h1
    c_ref[...] = _rms(h1, npre_ref[...]).astype(c_ref.dtype)


def _postmix(oh, og, zm, h, whu, wgu, wout, npost, npre, layer):
    M, D = h.shape
    HW, VW = oh.shape[1], og.shape[1]
    tm = _row_tile(M, ROW_CAP["postmix"])
    lay = lambda m: (layer, 0, 0)
    return pl.pallas_call(
        _postmix_kernel,
        grid=(M // tm,),
        in_specs=[pl.BlockSpec((tm, HW), lambda m: (m, 0)),
                  pl.BlockSpec((tm, VW), lambda m: (m, 0)),
                  pl.BlockSpec((tm, D), lambda m: (m, 0)),
                  pl.BlockSpec((tm, D), lambda m: (m, 1)),
                  pl.BlockSpec((tm, D), lambda m: (m, 0)),
                  _const_spec((None, HW, D), lay),
                  _const_spec((None, VW, D), lay),
                  _const_spec((None, D, D), lay),
                  pl.BlockSpec((None, 1, D), lay),
                  pl.BlockSpec((None, 1, D), lay)],
        out_specs=[pl.BlockSpec((tm, D), lambda m: (m, 0)),
                   pl.BlockSpec((tm, D), lambda m: (m, 0))],
        out_shape=[jax.ShapeDtypeStruct((M, D), f32),
                   jax.ShapeDtypeStruct((M, D), bf16)],
        compiler_params=_params(("parallel",)),
        name="postmix",
    )(oh, og, zm, zm, h, whu, wgu, wout, npost, npre)


def _ple_kernel(t_ref, h_ref, p_ref, wg_ref, wp_ref, npost_ref, nnext_ref, h3_ref, a_ref):
    h = h_ref[...] + _rms(t_ref[...], npost_ref[...])
    gate = _sigmoid(_dot(h.astype(bf16), wg_ref[...]))
    pe = _dot(p_ref[...].astype(bf16), wp_ref[...].astype(bf16))
    h3 = h + gate * pe
    h3_ref[...] = h3
    a_ref[...] = _rms(h3, nnext_ref[...]).astype(a_ref.dtype)


def _ple(t2, h1, p, wg, wp, npost, nnext, layer, next_layer):
    M, D = h1.shape
    P = p.shape[-1]
    tm = _row_tile(M, ROW_CAP["ple"])
    lay = lambda m: (layer, 0, 0)
    return pl.pallas_call(
        _ple_kernel,
        grid=(M // tm,),
        in_specs=[pl.BlockSpec((tm, D), lambda m: (m, 0)),
                  pl.BlockSpec((tm, D), lambda m: (m, 0)),
                  pl.BlockSpec((None, tm, P), lambda m: (layer, m, 0)),
                  _const_spec((None, D, D), lay),
                  _const_spec((None, P, D), lay),
                  pl.BlockSpec((None, 1, D), lay),
                  pl.BlockSpec((None, 1, D), lambda m: (next_layer, 0, 0))],
        out_specs=[pl.BlockSpec((tm, D), lambda m: (m, 0)),
                   pl.BlockSpec((tm, D), lambda m: (m, 0))],
        out_shape=[jax.ShapeDtypeStruct((M, D), f32),
                   jax.ShapeDtypeStruct((M, D), bf16)],
        compiler_params=_params(("parallel",)),
        name="ple",
    )(t2, h1, p, wg, wp, npost, nnext)


def kernel(x_prompt, x_sample, p_prompt, p_sample, state_hgrn, state_gla, norm_pre_mix,
           norm_post_mix, norm_pre_ffn, norm_post_ffn, w_in, lb_param, hgrn_norm, w_hgrn_up,
           w_gla_gate, b_gla_gate, gla_norm, w_gla_up, w_out, w_ff1, w_ff2, w_ple, w_ple_gate):
    B, L, D = x_prompt.shape
    Bs = x_sample.shape[0]
    depth = w_in.shape[0]
    _, _, HH, HK, HV = state_hgrn.shape
    _, _, GH, GK, GV = state_gla.shape
    HW, GKW, GVW = HH * HK, GH * GK, GH * GV
    R = w_gla_gate.shape[1]
    F = w_ff1.shape[2]
    assert HK == LANES and GK == LANES and HV % LANES == 0 and GV % LANES == 0
    assert HH * HV == HW and 2 * GKW == HW and GVW == HW and R <= LANES
    assert x_sample.shape[1] == 1 and L % CHUNK == 0 and D % HW == 0
    mixer_cols = 4 * HW + 2 * GKW + 2 * GVW

    assert w_in.shape[2] - mixer_cols >= LANES and (mixer_cols + R) % 8 == 0
    w_in_t = jnp.swapaxes(w_in, 1, 2)
    whu = w_hgrn_up.astype(bf16)
    wgu = w_gla_up.astype(bf16)
    wout = w_out.astype(bf16)
    wpg = w_ple_gate.astype(bf16)
    wgg = jnp.concatenate([w_gla_gate.astype(f32), jnp.zeros((depth, LANES - R, GKW), f32)], axis=1)

    lbs = jnp.cumsum(jax.nn.softmax(lb_param.astype(f32), axis=0), axis=0)
    lbs = (lbs - lbs[0:1]).reshape(depth, 1, HW)
    r3 = lambda t: t.astype(f32).reshape(depth, 1, t.shape[-1])
    n_pre_mix, n_post_mix, n_pre_ffn, n_post_ffn = map(
        r3, (norm_pre_mix, norm_post_mix, norm_pre_ffn, norm_post_ffn))
    hn, gn, bgg = r3(hgrn_norm), r3(gla_norm), r3(b_gla_gate)
    mall, lvl = _level_tables()
    seg_w = (HW, HW, HW, GKW, GKW, GVW, GVW)
    col_scale = jnp.asarray(np.concatenate(
        [np.full((1, w), v, np.float32) for w, v in zip(seg_w, (1, 1, 1, GK ** -0.5, 1, 1, 1))],
        axis=1))
    ep_main = (lambda n: (n == 0) | (n == 2) | (n == 5), _ep_silu, _ep_scale)
    tm_mg = _pick(D, (1024, 512, 256, 128))
    tf = _pick(F, (1024, 512, 256, 128))
    tn2 = _pick(D, (512, 256, 128))

    def in_projections(a):
        col_spec = pl.BlockSpec((1, HW), lambda n, m: (0, n))
        zb, = _proj(a, w_in_t, i, lambda n: (n + jnp.minimum(n, 1)) * HW, 6, HW, ep_main,
                    (col_scale,), (col_spec,), (bf16,), "in_proj", True)
        ld, kh = _proj(a, w_in_t, i, lambda n: HW, 1, HW, _ep_forget, (lbs,),
                       (pl.BlockSpec((None, 1, HW), lambda n, m: (i, 0, 0)),),
                       (f32, f32), "in_proj_forget", True)
        zm, = _proj(a, w_in_t, i, lambda n: mixer_cols + R + n * tm_mg, 2 * D // tm_mg, tm_mg,
                    _ep_sigmoid, (), (), (bf16,), "in_proj_merge", True)
        glr, = _proj(a, w_in_t, i, lambda n: mixer_cols, 1, LANES, _ep_copy, (), (), (f32,),
                     "in_proj_lowrank", True)
        return zb, ld, kh, zm, glr

    def dense_tail(h, oh, og, zm, p):
        h1, c = _postmix(oh, og, zm, h, whu, wgu, wout, n_post_mix, n_pre_ffn, i)
        u, = _proj(c, w_ff1, i, lambda n: n, F // tf, tf, _ep_relu2, (), (), (bf16,),
                   "ffn_up", False)
        t2, = _proj(u, w_ff2, i, lambda n: n, D // tn2, tn2, _ep_copy, (), (), (f32,),
                    "ffn_down", False, w_single=True)
        return _ple(t2, h1, p, wpg, w_ple, n_post_ffn, n_pre_mix, i, (i + 1) % depth)

    BL = B * L
    assert BL % 16 == 0 and Bs % 16 == 0
    h = jnp.concatenate([x_prompt.reshape(BL, D), x_sample.reshape(Bs, D)], axis=0)
    p = jnp.concatenate([p_prompt.reshape(depth, BL, -1), p_sample.reshape(depth, Bs, -1)], axis=1)
    a = _norm(h, n_pre_mix, 0)

    hgrn_p, gla_p = [], []
    hgrn_s = gla_s = None
    for i in range(depth):
        zb, ld, kh, zm, glr = in_projections(a)
        oh, sh = _hgrn_prompt(zb, ld, kh, hn, mall, lvl, i, B, L, HH, HK, HV)
        oh, hgrn_s = _hgrn_step(zb, ld, kh, hn, state_hgrn, oh, hgrn_s, i, BL, HH, HK, HV)
        og, sg = _gla_prompt(zb, glr, wgg, bgg, gn, mall, lvl, i, B, L, GH, GK, GV)
        og, gla_s = _gla_step(zb, glr, wgg, bgg, gn, state_gla, og, gla_s, i, BL, GH, GK, GV)
        h, a = dense_tail(h, oh, og, zm, p)
        hgrn_p.append(sh)
        gla_p.append(sg)

    return (h[:BL].reshape(B, L, D), h[BL:].reshape(Bs, 1, D),
            jnp.stack(hgrn_p).astype(state_hgrn.dtype), jnp.stack(gla_p).astype(state_gla.dtype),
            hgrn_s.astype(state_hgrn.dtype), gla_s.astype(state_gla.dtype))
```
